```python
import math
import functools
import jax
import jax.numpy as jnp
from jax import lax
import numpy as np

D_MODEL = 1024
BATCH = 4
SEQ = 8192
DEPTH = 2
DEC_BATCH = 32
DEC_SEQ = 16
PAST_LEN = 4096

CHUNK = 64
A_HEADS = 4
A_DIM = 64
A_WIDTH = A_HEADS * 2 * A_DIM
ROT_DIM = A_DIM // 4
ROPE_THETA = 500000.0
Q_BLOCK = 128
B_HEADS = 8
B_DIM = 64
B_WIDTH = B_HEADS * B_DIM
LEFT_CHUNKS = 8
B_REACH = LEFT_CHUNKS * CHUNK
BAND = B_REACH + CHUNK
REL_CLIP = 128
N_REL = 2 * REL_CLIP + 1
MIX_WIDTH = A_WIDTH + B_WIDTH
IN_SPLITS = [A_WIDTH, 2 * A_WIDTH, 3 * A_WIDTH, 3 * A_WIDTH + B_WIDTH, 3 * A_WIDTH + 2 * B_WIDTH]
IN_WIDTH = 3 * A_WIDTH + 3 * B_WIDTH
PK_HEADS = 8
N_KEYS = 128
N_EXPERTS = N_KEYS * N_KEYS
PK_TOPK = 16
PK_QDIM = 256
PK_HALF = PK_QDIM // 2
PEER_BLOCK = 128
EPS = 1e-6

kernel_name = 'hybrid_diffattn_chunkband_peer_stream_step'


def rms_norm(x, g):
    x32 = x.astype(jnp.float32)
    y = x32 * lax.rsqrt(jnp.mean(x32 * x32, axis=-1, keepdims=True) + EPS)
    return y.astype(x.dtype) * g


def rope_partial(x, pos):
    half = ROT_DIM // 2
    inv = ROPE_THETA ** (-jnp.arange(0, ROT_DIM, 2, dtype=jnp.float32) / ROT_DIM)
    ang = pos.astype(jnp.float32)[:, None] * inv[None, :]
    cos = jnp.cos(ang)[:, None, None, :].astype(x.dtype)
    sin = jnp.sin(ang)[:, None, None, :].astype(x.dtype)
    x1 = x[..., :half]
    x2 = x[..., half:ROT_DIM]
    return jnp.concatenate([x1 * cos - x2 * sin, x2 * cos + x1 * sin, x[..., ROT_DIM:]], axis=-1)


def diff_lambda(lq1, lk1, lq2, lk2, lam_init):
    f = lambda a: a.astype(jnp.float32)
    return jnp.exp(jnp.sum(f(lq1) * f(lk1))) - jnp.exp(jnp.sum(f(lq2) * f(lk2))) + lam_init


def diff_attend(q, k, v, lam, mask):
    s = jnp.einsum('bqhmd,bkhmd->bhmqk', q, k).astype(jnp.float32) * (A_DIM ** -0.5)
    if mask is not None:
        s = jnp.where(mask, s, -jnp.inf)
    p = jax.nn.softmax(s, axis=-1)
    a = p[:, :, 0] - lam * p[:, :, 1]
    return jnp.einsum('bhqk,bkhe->bqhe', a.astype(v.dtype), v)


def diff_attn_prompt(q, k, v, lam):
    bn, s = q.shape[0], q.shape[1]
    nb = s // Q_BLOCK
    qb = q.reshape(bn, nb, Q_BLOCK, A_HEADS, 2, A_DIM).swapaxes(0, 1)
    kchunk = jnp.arange(s) // CHUNK

    def one(args):
        qi, bi = args
        qchunk = (bi * Q_BLOCK + jnp.arange(Q_BLOCK)) // CHUNK
        mask = kchunk[None, :] <= qchunk[:, None]
        return diff_attend(qi, k, v, lam, mask)

    o = lax.map(one, (qb, jnp.arange(nb)))
    return o.swapaxes(0, 1).reshape(bn, s, A_HEADS, 2 * A_DIM)


def rel_bias_matrix(table, qpos, kpos):
    d = jnp.clip(qpos[:, None] - kpos[None, :], -REL_CLIP, REL_CLIP) + REL_CLIP
    return table[:, d].astype(jnp.float32)


def band_attend(q, k, v, bias, valid):
    s = jnp.einsum('...qhd,...khd->...hqk', q, k).astype(jnp.float32) * (B_DIM ** -0.5) + bias
    if valid is not None:
        s = jnp.where(valid, s, -jnp.inf)
    p = jax.nn.softmax(s, axis=-1)
    return jnp.einsum('...hqk,...khd->...qhd', p.astype(v.dtype), v)


def chunk_band_prompt(q, k, v, table):
    bn, s = q.shape[0], q.shape[1]
    nc = s // CHUNK
    pad = ((0, 0), (B_REACH, 0), (0, 0), (0, 0))
    kp = jnp.pad(k, pad).reshape(bn, nc + LEFT_CHUNKS, CHUNK, B_HEADS, B_DIM)
    vp = jnp.pad(v, pad).reshape(bn, nc + LEFT_CHUNKS, CHUNK, B_HEADS, B_DIM)
    qc = q.reshape(bn, nc, CHUNK, B_HEADS, B_DIM)
    idx = jnp.arange(nc)[:, None] + jnp.arange(LEFT_CHUNKS + 1)[None, :]
    bias = rel_bias_matrix(table, jnp.arange(CHUNK), jnp.arange(BAND) - B_REACH)
    kpos = jnp.arange(nc)[:, None] * CHUNK - B_REACH + jnp.arange(BAND)[None, :]
    valid = (kpos >= 0)[:, None, None, :]

    def one(args):
        qi, ki, vi = args
        kb = ki[idx].reshape(nc, BAND, B_HEADS, B_DIM)
        vb = vi[idx].reshape(nc, BAND, B_HEADS, B_DIM)
        return band_attend(qi, kb, vb, bias, valid)

    o = lax.map(one, (qc, kp, vp))
    return o.reshape(bn, s, B_HEADS, B_DIM)


def mixer_prompt(qa, ka, va, qb, kb, vb, lam, rel_table):
    bn, s = qa.shape[0], qa.shape[1]
    oa = diff_attn_prompt(qa, ka, va, lam)
    ob = chunk_band_prompt(qb, kb, vb, rel_table)
    keep = min(B_REACH, s)
    return oa, ob, (ka.reshape(bn, s, A_HEADS, 2 * A_DIM), va, kb[:, s - keep:], vb[:, s - keep:])


def mixer_sample(qa, ka, va, qb, kb, vb, lam, rel_table, cache_ak, cache_av, cache_bk, cache_bv):
    bn, t = qa.shape[0], qa.shape[1]
    p_len = cache_ak.shape[1]
    k_all = jnp.concatenate([cache_ak.reshape(bn, p_len, A_HEADS, 2, A_DIM), ka], axis=1)
    v_all = jnp.concatenate([cache_av, va], axis=1)
    oa = diff_attend(qa, k_all, v_all, lam, None)
    wb = cache_bk.shape[1]
    kb_all = jnp.concatenate([cache_bk, kb], axis=1)
    vb_all = jnp.concatenate([cache_bv, vb], axis=1)
    qpos = PAST_LEN + jnp.arange(t)
    kpos = PAST_LEN - wb + jnp.arange(wb + t)
    ob = band_attend(qb, kb_all, vb_all, rel_bias_matrix(rel_table, qpos, kpos), None)
    return oa, ob, (ka.reshape(bn, t, A_HEADS, 2 * A_DIM), va, kb_all[:, t:], vb_all[:, t:])


def peer_ffn(h, wq, sub_keys, u_tab, v_tab):
    n = h.shape[0]
    nb = -(-n // PEER_BLOCK)
    hp = jnp.pad(h, ((0, nb * PEER_BLOCK - n), (0, 0))).reshape(nb, PEER_BLOCK, D_MODEL)

    def one(hb):
        q = (hb @ wq).reshape(PEER_BLOCK, PK_HEADS, 2, PK_HALF)
        s = jnp.einsum('thmk,hmnk->thmn', q, sub_keys)
        sv, si = lax.top_k(s, PK_TOPK)
        cand = sv[..., 0, :, None] + sv[..., 1, None, :]
        cidx = si[..., 0, :, None] * N_KEYS + si[..., 1, None, :]
        cv, ci = lax.top_k(cand.reshape(PEER_BLOCK, PK_HEADS, PK_TOPK * PK_TOPK), PK_TOPK)
        eidx = jnp.take_along_axis(cidx.reshape(PEER_BLOCK, PK_HEADS, PK_TOPK * PK_TOPK), ci, axis=-1)
        g = jax.nn.softmax(cv.astype(jnp.float32), axis=-1)
        u = jnp.take(u_tab, eidx, axis=0)
        a = jax.nn.gelu(jnp.einsum('td,thkd->thk', hb, u))
        w = (g * a.astype(jnp.float32)).astype(hb.dtype)
        return jnp.einsum('thk,thkd->td', w, jnp.take(v_tab, eidx, axis=0))

    out = lax.map(one, hp).reshape(nb * PEER_BLOCK, D_MODEL)
    return out[:n]


def layer(x, c, pos, mixer, lam_init, w_ada, b_ada, g_attn, g_ffn, w_in, a_gain, b_gain, w_out,
          pk_wq, pk_keys, pk_u, pk_v):
    bn, L = x.shape[0], x.shape[1]
    mod = (jax.nn.silu(c) @ w_ada + b_ada)[:, None, :]
    sh1, sc1, gt1, sh2, sc2, gt2 = jnp.split(mod, 6, axis=-1)
    h = rms_norm(x, g_attn) * (1 + sc1) + sh1
    qa, ka, va, qb, kb, vb = jnp.split(h @ w_in, IN_SPLITS, axis=-1)
    qa = rope_partial(qa.reshape(bn, L, A_HEADS, 2, A_DIM), pos)
    ka = rope_partial(ka.reshape(bn, L, A_HEADS, 2, A_DIM), pos)
    va = va.reshape(bn, L, A_HEADS, 2 * A_DIM)
    qb = qb.reshape(bn, L, B_HEADS, B_DIM)
    kb = kb.reshape(bn, L, B_HEADS, B_DIM)
    vb = vb.reshape(bn, L, B_HEADS, B_DIM)
    oa, ob, state = mixer(qa, ka, va, qb, kb, vb)
    oa = (rms_norm(oa, a_gain) * (1.0 - lam_init)).reshape(bn, L, A_WIDTH)
    ob = rms_norm(ob.reshape(bn, L, B_WIDTH), b_gain)
    x = x + gt1 * (jnp.concatenate([oa, ob], axis=-1) @ w_out)
    h = rms_norm(x, g_ffn) * (1 + sc2) + sh2
    f = peer_ffn(h.reshape(bn * L, D_MODEL), pk_wq, pk_keys, pk_u, pk_v).reshape(bn, L, D_MODEL)
    x = x + gt2 * f
    return x, state


def setup_inputs(seed: int = 0) -> dict:
    key = jax.random.key(seed)
    ks = jax.random.split(key, 32)
    nrm = lambda k, shape, s: jax.random.normal(k, shape, jnp.float32) * s
    wb = min(B_REACH, PAST_LEN)
    return {
        'x_prompt': nrm(ks[0], (BATCH, SEQ, D_MODEL), 1.0),
        'x_sample': nrm(ks[1], (DEC_BATCH, DEC_SEQ, D_MODEL), 1.0),
        'c_prompt': nrm(ks[2], (BATCH, D_MODEL), 1.0),
        'c_sample': nrm(ks[3], (DEC_BATCH, D_MODEL), 1.0),
        'cache_a_k': nrm(ks[4], (DEPTH, DEC_BATCH, PAST_LEN, A_HEADS, 2 * A_DIM), 1.0),
        'cache_a_v': nrm(ks[5], (DEPTH, DEC_BATCH, PAST_LEN, A_HEADS, 2 * A_DIM), 1.0),
        'cache_b_k': nrm(ks[6], (DEPTH, DEC_BATCH, wb, B_HEADS, B_DIM), 1.0),
        'cache_b_v': nrm(ks[7], (DEPTH, DEC_BATCH, wb, B_HEADS, B_DIM), 1.0),
        'w_ada': nrm(ks[8], (DEPTH, D_MODEL, 6 * D_MODEL), 0.5 * D_MODEL ** -0.5),
        'b_ada': nrm(ks[9], (DEPTH, 6 * D_MODEL), 0.02),
        'g_attn': 1.0 + nrm(ks[10], (DEPTH, D_MODEL), 0.05),
        'g_ffn': 1.0 + nrm(ks[11], (DEPTH, D_MODEL), 0.05),
        'w_in': nrm(ks[12], (DEPTH, D_MODEL, IN_WIDTH), D_MODEL ** -0.5),
        'lam_q1': nrm(ks[13], (DEPTH, A_DIM), 0.1),
        'lam_k1': nrm(ks[14], (DEPTH, A_DIM), 0.1),
        'lam_q2': nrm(ks[15], (DEPTH, A_DIM), 0.1),
        'lam_k2': nrm(ks[16], (DEPTH, A_DIM), 0.1),
        'a_gain': 1.0 + nrm(ks[17], (DEPTH, 2 * A_DIM), 0.05),
        'rel_bias': nrm(ks[18], (DEPTH, B_HEADS, N_REL), 0.5),
        'b_gain': 1.0 + nrm(ks[19], (DEPTH, B_WIDTH), 0.05),
        'w_out': nrm(ks[20], (DEPTH, MIX_WIDTH, D_MODEL), MIX_WIDTH ** -0.5),
        'pk_wq': nrm(ks[21], (DEPTH, D_MODEL, PK_HEADS * PK_QDIM), D_MODEL ** -0.5),
        'pk_keys': nrm(ks[22], (DEPTH, PK_HEADS, 2, N_KEYS, PK_HALF), PK_HALF ** -0.5),
        'pk_u': nrm(ks[23], (DEPTH, N_EXPERTS, D_MODEL), D_MODEL ** -0.5),
        'pk_v': nrm(ks[24], (DEPTH, N_EXPERTS, D_MODEL), 0.3),
        'g_final': 1.0 + nrm(ks[25], (D_MODEL,), 0.05),
    }


def reference(x_prompt, x_sample, c_prompt, c_sample, cache_a_k, cache_a_v, cache_b_k, cache_b_v,
              w_ada, b_ada, g_attn, g_ffn, w_in, lam_q1, lam_k1, lam_q2, lam_k2, a_gain, rel_bias,
              b_gain, w_out, pk_wq, pk_keys, pk_u, pk_v, g_final):
    s_len = x_prompt.shape[1]
    t_len = x_sample.shape[1]
    pos_p = jnp.arange(s_len, dtype=jnp.int32)
    pos_s = PAST_LEN + jnp.arange(t_len, dtype=jnp.int32)
    xp, xs = x_prompt, x_sample
    st_p, st_s = [], []
    for l in range(DEPTH):
        lam_init = 0.8 - 0.6 * math.exp(-0.3 * l)
        lam = diff_lambda(lam_q1[l], lam_k1[l], lam_q2[l], lam_k2[l], lam_init)
        weights = (w_ada[l], b_ada[l], g_attn[l], g_ffn[l], w_in[l], a_gain[l], b_gain[l], w_out[l],
                   pk_wq[l], pk_keys[l], pk_u[l], pk_v[l])
        mix_p = functools.partial(mixer_prompt, lam=lam, rel_table=rel_bias[l])
        mix_s = functools.partial(mixer_sample, lam=lam, rel_table=rel_bias[l],
                                  cache_ak=cache_a_k[l], cache_av=cache_a_v[l],
                                  cache_bk=cache_b_k[l], cache_bv=cache_b_v[l])
        xp, sp = layer(xp, c_prompt, pos_p, mix_p, lam_init, *weights)
        xs, ss = layer(xs, c_sample, pos_s, mix_s, lam_init, *weights)
        st_p.append(sp)
        st_s.append(ss)
    y_prompt = rms_norm(xp, g_final)
    y_sample = rms_norm(xs, g_final)
    new_a_k_prompt = jnp.stack([s[0] for s in st_p])
    new_a_v_prompt = jnp.stack([s[1] for s in st_p])
    new_b_k_prompt = jnp.stack([s[2] for s in st_p])
    new_b_v_prompt = jnp.stack([s[3] for s in st_p])
    new_a_k_sample = jnp.stack([s[0] for s in st_s])
    new_a_v_sample = jnp.stack([s[1] for s in st_s])
    new_b_k_sample = jnp.stack([s[2] for s in st_s])
    new_b_v_sample = jnp.stack([s[3] for s in st_s])
    return (y_prompt, y_sample, new_a_k_prompt, new_a_v_prompt, new_b_k_prompt, new_b_v_prompt,
            new_a_k_sample, new_a_v_sample, new_b_k_sample, new_b_v_sample)
```

```python
import functools
import math

import jax
import jax.numpy as jnp
from jax import lax
from jax.experimental import pallas as pl
from jax.experimental.pallas import tpu as pltpu

_F32 = jnp.float32
_BF = jnp.bfloat16

CHUNK = 64
A_HEADS = 4
A_DIM = 64
ROT_DIM = A_DIM // 4
ROPE_THETA = 500000.0
B_HEADS = 8
B_DIM = 64
LEFT_CHUNKS = 8
B_REACH = LEFT_CHUNKS * CHUNK
REL_CLIP = 128
PK_HEADS = 8
N_KEYS = 128
PK_TOPK = 16
PK_HALF = 128
EPS = 1e-6

LANES = 128
MASKED = -1e30
VMEM_LIMIT = 56 * 1024 * 1024

A_TILE = 256
B_TILE = 256
PEER_TOKENS = 512
PEER_EXPERTS = 1024
ROW_TILE = 512


def _params(*sem):
    return pltpu.CompilerParams(dimension_semantics=sem, vmem_limit_bytes=VMEM_LIMIT)


def _resident(shape, index_map):
    return pl.BlockSpec(shape, index_map, pipeline_mode=pl.Buffered(1))


def _rms(x):
    return x * lax.rsqrt(jnp.mean(x * x, axis=-1, keepdims=True) + EPS)


def _dot(a, b):
    return jnp.dot(a, b, preferred_element_type=_F32)


def _dot_nt(a, b):
    return lax.dot_general(a, b, (((1,), (1,)), ((), ())), preferred_element_type=_F32)


def _mod_kernel(c_ref, w_ref, b_ref, o_ref):
    c = c_ref[...]
    s = c * (1.0 / (1.0 + jnp.exp(-c)))
    o_ref[...] = _dot(s.astype(_BF), w_ref[...].astype(_BF)) + b_ref[...]


def _modulation(c_all, w_ada, b_ada):
    depth, d, n = w_ada.shape
    rows = c_all.shape[0]
    tn = n // 4
    return pl.pallas_call(
        _mod_kernel,
        grid=(depth, n // tn),
        in_specs=[
            pl.BlockSpec((rows, d), lambda l, j: (0, 0)),
            pl.BlockSpec((None, d, tn), lambda l, j: (l, 0, j)),
            pl.BlockSpec((None, 1, tn), lambda l, j: (l, 0, j)),
        ],
        out_specs=pl.BlockSpec((None, rows, tn), lambda l, j: (l, 0, j)),
        out_shape=jax.ShapeDtypeStruct((depth, rows, n), _F32),
        compiler_params=_params("arbitrary", "arbitrary"),
        name="adaln_mod",
    )(c_all, w_ada, b_ada.reshape(depth, 1, n))


def _inproj_kernel(x_ref, sh_ref, sc_ref, g_ref, w_ref, cos_ref, sa_ref, sb_ref,
                   qa_ref, kaf_ref, vaf_ref, kab_ref, vab_ref,
                   qb_ref, kbf_ref, vbf_ref, kbb_ref, vbb_ref, *, width):
    h = _rms(x_ref[...]) * g_ref[...] * (1.0 + sc_ref[...]) + sh_ref[...]
    hb = h.astype(_BF)
    reps = width // LANES
    cos = jnp.tile(cos_ref[...], (1, reps))
    sa = jnp.tile(sa_ref[...], (1, reps))
    sb = jnp.tile(sb_ref[...], (1, reps))

    def rope(z):
        return z * cos + pltpu.roll(z, width - ROT_DIM // 2, 1) * sa + pltpu.roll(z, ROT_DIM // 2, 1) * sb

    def proj(i):
        return _dot(hb, w_ref[:, i * width:(i + 1) * width])

    qa_ref[...] = (rope(proj(0)) * (A_DIM ** -0.5)).astype(_BF)
    ka = rope(proj(1))
    kaf_ref[...] = ka
    kab_ref[...] = ka.astype(_BF)
    va = proj(2)
    vaf_ref[...] = va
    vab_ref[...] = va.astype(_BF)
    qb_ref[...] = (proj(3) * (B_DIM ** -0.5)).astype(_BF)
    kb = proj(4)
    kbf_ref[...] = kb
    kbb_ref[...] = kb.astype(_BF)
    vb = proj(5)
    vbf_ref[...] = vb
    vbb_ref[...] = vb.astype(_BF)


def _row_mod_specs(per_row, tm, d, tiles_per_batch):
    if per_row:
        return pl.BlockSpec((tm, d), lambda t: (t, 0))
    return pl.BlockSpec((None, 1, d), lambda t: (t // tiles_per_batch, 0, 0))


def _in_projection(x, shift, scale, gain, w_in_bf, rope_tabs, *, per_row, seq):
    t_all, d = x.shape
    width = w_in_bf.shape[1] // 6
    tm = min(ROW_TILE, t_all)
    tiles_per_batch = max(seq // tm, 1)
    tab_tiles = rope_tabs[0].shape[0] // tm
    mod_spec = _row_mod_specs(per_row, tm, d, tiles_per_batch)
    tab_spec = pl.BlockSpec((tm, LANES), lambda t: (t % tab_tiles, 0))
    row = lambda dt: jax.ShapeDtypeStruct((t_all, width), dt)
    out_spec = pl.BlockSpec((tm, width), lambda t: (t, 0))
    dts = [_BF, _F32, _F32, _BF, _BF, _BF, _F32, _F32, _BF, _BF]
    return pl.pallas_call(
        functools.partial(_inproj_kernel, width=width),
        grid=(t_all // tm,),
        in_specs=[
            pl.BlockSpec((tm, d), lambda t: (t, 0)),
            mod_spec, mod_spec,
            _resident((1, d), lambda t: (0, 0)),
            _resident(w_in_bf.shape, lambda t: (0, 0)),
            tab_spec, tab_spec, tab_spec,
        ],
        out_specs=[out_spec] * 10,
        out_shape=[row(dt) for dt in dts],
        compiler_params=_params("arbitrary"),
        name="in_projection",
    )(x, shift, scale, gain, w_in_bf, *rope_tabs)


def _diff_lambda(lamv_ref, lam_init):
    v = lamv_ref[...]
    d1 = jnp.sum(v[0:1] * v[1:2], axis=1, keepdims=True)
    d2 = jnp.sum(v[2:3] * v[3:4], axis=1, keepdims=True)
    return jnp.exp(d1) - jnp.exp(d2) + lam_init


def _split_maps(q):
    lane = lax.broadcasted_iota(jnp.int32, q.shape, 1)
    zero = jnp.zeros_like(q)
    return jnp.where(lane < A_DIM, q, zero), jnp.where(lane >= A_DIM, q, zero)


def _attn_a_kernel(lamv_ref, q_ref, k_ref, v_ref, gain_ref, o_ref,
                   m1, l1, acc1, m2, l2, acc2, *, lam_init):
    qi = pl.program_id(2)
    tq = q_ref.shape[0]
    q1, q2 = _split_maps(q_ref[...])
    for m_scr, l_scr, acc in ((m1, l1, acc1), (m2, l2, acc2)):
        m_scr[...] = jnp.full(m_scr.shape, MASKED, _F32)
        l_scr[...] = jnp.zeros(l_scr.shape, _F32)
        acc[...] = jnp.zeros(acc.shape, _F32)

    def update(s, m_scr, l_scr, acc, vb):
        m_prev = m_scr[...]
        m_next = jnp.maximum(m_prev, jnp.max(s, axis=1, keepdims=True))
        p = jnp.exp(s - jnp.tile(m_next, (1, tq // LANES)))
        alpha = jnp.exp(m_prev - m_next)
        l_scr[...] = alpha * l_scr[...] + jnp.sum(p, axis=1, keepdims=True)
        acc[...] = acc[...] * alpha + _dot(p.astype(_BF), vb)
        m_scr[...] = m_next

    def block(j, diagonal):
        rows = pl.ds(pl.multiple_of(j * tq, tq), tq)
        kb = k_ref[rows, :]
        vb = v_ref[rows, :]
        s1 = _dot_nt(q1, kb)
        s2 = _dot_nt(q2, kb)
        if diagonal:
            qc = lax.broadcasted_iota(jnp.int32, s1.shape, 0) // CHUNK
            kc = lax.broadcasted_iota(jnp.int32, s1.shape, 1) // CHUNK
            s1 = jnp.where(kc <= qc, s1, MASKED)
            s2 = jnp.where(kc <= qc, s2, MASKED)
        update(s1, m1, l1, acc1, vb)
        update(s2, m2, l2, acc2, vb)

    def body(j, carry):
        block(j, False)
        return carry

    lax.fori_loop(0, qi, body, 0)
    block(qi, True)

    lam = _diff_lambda(lamv_ref, lam_init)
    o = acc1[...] / l1[...] - lam * (acc2[...] / l2[...])
    o_ref[...] = (_rms(o) * gain_ref[...] * (1.0 - lam_init)).astype(_BF)


def _attn_a_prompt(qa, ka, va, lamv, a_gain, *, batch, seq, lam_init):
    t_all, width = qa.shape
    hd = 2 * A_DIM
    tq = A_TILE
    nq = seq // tq
    return pl.pallas_call(
        functools.partial(_attn_a_kernel, lam_init=lam_init),
        grid=(batch, A_HEADS, nq),
        in_specs=[
            _resident(lamv.shape, lambda b, h, i: (0, 0)),
            pl.BlockSpec((tq, hd), lambda b, h, i: (b * nq + i, h)),
            pl.BlockSpec((seq, hd), lambda b, h, i: (b, h)),
            pl.BlockSpec((seq, hd), lambda b, h, i: (b, h)),
            _resident((1, hd), lambda b, h, i: (0, 0)),
        ],
        out_specs=pl.BlockSpec((tq, hd), lambda b, h, i: (b * nq + i, h)),
        out_shape=jax.ShapeDtypeStruct((t_all, width), _BF),
        scratch_shapes=[pltpu.VMEM((tq, hd), _F32)] * 6,
        compiler_params=_params("arbitrary", "arbitrary", "arbitrary"),
        name="diff_attn_prompt",
    )(lamv, qa, ka, va, a_gain)


def _head_lanes(shape, head, dim):
    lane = lax.broadcasted_iota(jnp.int32, shape, 1)
    return (lane >= head * dim) & (lane < (head + 1) * dim)


def _attn_b_kernel(q_ref, k0_ref, k1_ref, k2_ref, v0_ref, v1_ref, v2_ref, bias_ref, gain_ref, o_ref):
    qi = pl.program_id(1)
    tq = q_ref.shape[0]
    k_refs = (k0_ref, k1_ref, k2_ref)
    v_refs = (v0_ref, v1_ref, v2_ref)
    pen = (jnp.where(qi >= 2, 0.0, MASKED), jnp.where(qi >= 1, 0.0, MASKED), 0.0)
    outs = []
    for pair in range(B_HEADS // 2):
        cols = slice(pair * LANES, (pair + 1) * LANES)
        qp = q_ref[:, cols]
        zero = jnp.zeros_like(qp)
        halves = []
        for sub in range(2):
            head = 2 * pair + sub
            qh = jnp.where(_head_lanes(qp.shape, sub, B_DIM), qp, zero)
            s = [_dot_nt(qh, k_refs[b][:, cols]) + bias_ref[head, :, b * tq:(b + 1) * tq] + pen[b]
                 for b in range(3)]
            m = jnp.maximum(jnp.maximum(jnp.max(s[0], axis=1, keepdims=True),
                                        jnp.max(s[1], axis=1, keepdims=True)),
                            jnp.max(s[2], axis=1, keepdims=True))
            p = [jnp.exp(sb - m) for sb in s]
            l = (jnp.sum(p[0], axis=1, keepdims=True) + jnp.sum(p[1], axis=1, keepdims=True)
                 + jnp.sum(p[2], axis=1, keepdims=True))
            o = (_dot(p[0].astype(_BF), v_refs[0][:, cols]) + _dot(p[1].astype(_BF), v_refs[1][:, cols])
                 + _dot(p[2].astype(_BF), v_refs[2][:, cols]))
            halves.append(o / l)
        outs.append(jnp.where(_head_lanes(halves[0].shape, 0, B_DIM), halves[0], halves[1]))
    o = jnp.concatenate(outs, axis=1)
    o_ref[...] = (_rms(o) * gain_ref[...]).astype(_BF)


def _band_bias(table, tq):
    nk = B_REACH + tq
    qpos = jnp.arange(tq)[:, None]
    kpos = jnp.arange(nk)[None, :] - B_REACH
    qc = qpos // CHUNK
    kc = jnp.floor_divide(kpos, CHUNK)
    valid = (kc <= qc) & (kc >= qc - LEFT_CHUNKS)
    d = jnp.clip(qpos - kpos, -REL_CLIP, REL_CLIP) + REL_CLIP
    return jnp.where(valid[None], table[:, d].astype(_F32), MASKED)


def _attn_b_prompt(qb, kb, vb, rel_table, b_gain, *, batch, seq):
    t_all, width = qb.shape
    tq = B_TILE
    assert B_REACH == 2 * tq
    nq = seq // tq
    bias = _band_bias(rel_table, tq)
    qmap = lambda b, i: (b * nq + i, 0)
    kmap = lambda back: (lambda b, i: (b * nq + jnp.maximum(i - back, 0), 0))
    blk = lambda f: pl.BlockSpec((tq, width), f)
    return pl.pallas_call(
        _attn_b_kernel,
        grid=(batch, nq),
        in_specs=[blk(qmap), blk(kmap(2)), blk(kmap(1)), blk(kmap(0)),
                  blk(kmap(2)), blk(kmap(1)), blk(kmap(0)),
                  _resident(bias.shape, lambda b, i: (0, 0, 0)),
                  _resident((1, width), lambda b, i: (0, 0))],
        out_specs=blk(qmap),
        out_shape=jax.ShapeDtypeStruct((t_all, width), _BF),
        compiler_params=_params("arbitrary", "arbitrary"),
        name="band_attn_prompt",
    )(qb, kb, kb, kb, vb, vb, vb, bias, b_gain)


def _attn_a_sample_kernel(lamv_ref, q_ref, kc_ref, vc_ref, kn_ref, vn_ref, gain_ref, o_ref, *, lam_init):
    q = q_ref[...]
    t = q.shape[0]
    hd = 2 * A_DIM
    zero = jnp.zeros_like(q)
    rows = []
    for hm in range(2 * A_HEADS):
        rows.append(jnp.where(_head_lanes(q.shape, hm, A_DIM), q, zero))
    qall = jnp.concatenate(rows, axis=0)
    s_c = _dot_nt(qall, kc_ref[...].astype(_BF))
    s_n = _dot_nt(qall, kn_ref[...].astype(_BF))
    m = jnp.maximum(jnp.max(s_c, axis=1, keepdims=True), jnp.max(s_n, axis=1, keepdims=True))
    p_c = jnp.exp(s_c - m)
    p_n = jnp.exp(s_n - m)
    l = jnp.sum(p_c, axis=1, keepdims=True) + jnp.sum(p_n, axis=1, keepdims=True)
    p_c = p_c / l
    p_n = p_n / l
    lam = _diff_lambda(lamv_ref, lam_init)

    def diff(p):
        return jnp.concatenate(
            [p[2 * h * t:(2 * h + 1) * t] - lam * p[(2 * h + 1) * t:(2 * h + 2) * t] for h in range(A_HEADS)],
            axis=0)

    o_all = (_dot(diff(p_c).astype(_BF), vc_ref[...].astype(_BF))
             + _dot(diff(p_n).astype(_BF), vn_ref[...].astype(_BF)))
    outs = []
    for h in range(A_HEADS):
        o = o_all[h * t:(h + 1) * t, h * hd:(h + 1) * hd]
        outs.append(_rms(o) * gain_ref[...] * (1.0 - lam_init))
    o_ref[...] = jnp.concatenate(outs, axis=1).astype(_BF)


def _attn_a_sample(qa, cache_k, cache_v, kn, vn, lamv, a_gain, *, layer, lam_init, t_new):
    t_all, width = qa.shape
    _, nb, past, _ = cache_k.shape
    tok = pl.BlockSpec((t_new, width), lambda b: (b, 0))
    cache = pl.BlockSpec((None, None, past, width), lambda b: (layer, b, 0, 0))
    return pl.pallas_call(
        functools.partial(_attn_a_sample_kernel, lam_init=lam_init),
        grid=(nb,),
        in_specs=[_resident(lamv.shape, lambda b: (0, 0)), tok, cache, cache, tok, tok,
                  _resident((1, 2 * A_DIM), lambda b: (0, 0))],
        out_specs=tok,
        out_shape=jax.ShapeDtypeStruct((t_all, width), _BF),
        compiler_params=_params("arbitrary"),
        name="diff_attn_sample",
    )(lamv, qa, cache_k, cache_v, kn, vn, a_gain)


def _attn_b_sample_kernel(q_ref, kc_ref, vc_ref, kn_ref, vn_ref, bc_ref, bn_ref, gain_ref,
                          o_ref, ko_ref, vo_ref):
    q = q_ref[...]
    t = q.shape[0]
    keep = kc_ref.shape[0] - t
    zero = jnp.zeros_like(q)
    qall = jnp.concatenate(
        [jnp.where(_head_lanes(q.shape, h, B_DIM), q, zero) for h in range(B_HEADS)], axis=0)
    kc = kc_ref[...]
    vc = vc_ref[...]
    kn = kn_ref[...]
    vn = vn_ref[...]
    s_c = _dot_nt(qall, kc.astype(_BF)) + bc_ref[...]
    s_n = _dot_nt(qall, kn.astype(_BF)) + bn_ref[...]
    m = jnp.maximum(jnp.max(s_c, axis=1, keepdims=True), jnp.max(s_n, axis=1, keepdims=True))
    p_c = jnp.exp(s_c - m)
    p_n = jnp.exp(s_n - m)
    l = jnp.sum(p_c, axis=1, keepdims=True) + jnp.sum(p_n, axis=1, keepdims=True)
    o_all = (_dot(p_c.astype(_BF), vc.astype(_BF)) + _dot(p_n.astype(_BF), vn.astype(_BF))) / l
    o = jnp.zeros((t, q.shape[1]), _F32)
    for h in range(B_HEADS):
        o = jnp.where(_head_lanes(o.shape, h, B_DIM), o_all[h * t:(h + 1) * t], o)
    o_ref[...] = (_rms(o) * gain_ref[...]).astype(_BF)
    ko_ref[0:keep, :] = kc[t:, :]
    ko_ref[keep:, :] = kn
    vo_ref[0:keep, :] = vc[t:, :]
    vo_ref[keep:, :] = vn


def _sample_bias(table, t_new, wb):
    qpos = jnp.arange(t_new)[:, None] + wb
    kpos = jnp.arange(wb + t_new)[None, :]
    d = jnp.clip(qpos - kpos, -REL_CLIP, REL_CLIP) + REL_CLIP
    bias = table[:, d].astype(_F32).reshape(B_HEADS * t_new, wb + t_new)
    return bias[:, :wb], bias[:, wb:]


def _attn_b_sample(qb, cache_k, cache_v, kn, vn, rel_table, b_gain, *, layer, t_new):
    t_all, width = qb.shape
    _, nb, wb, _ = cache_k.shape
    bias_c, bias_n = _sample_bias(rel_table, t_new, wb)
    tok = pl.BlockSpec((t_new, width), lambda b: (b, 0))
    cache = pl.BlockSpec((None, None, wb, width), lambda b: (layer, b, 0, 0))
    state = pl.BlockSpec((None, wb, width), lambda b: (b, 0, 0))
    return pl.pallas_call(
        _attn_b_sample_kernel,
        grid=(nb,),
        in_specs=[tok, cache, cache, tok, tok,
                  _resident(bias_c.shape, lambda b: (0, 0)), _resident(bias_n.shape, lambda b: (0, 0)),
                  _resident((1, width), lambda b: (0, 0))],
        out_specs=[tok, state, state],
        out_shape=[jax.ShapeDtypeStruct((t_all, width), _BF),
                   jax.ShapeDtypeStruct((nb, wb, width), _F32),
                   jax.ShapeDtypeStruct((nb, wb, width), _F32)],
        compiler_params=_params("arbitrary"),
        name="band_attn_sample",
    )(qb, cache_k, cache_v, kn, vn, bias_c, bias_n, b_gain)


def _outproj_kernel(x_ref, oa_ref, ob_ref, w_ref, gt_ref, sh_ref, sc_ref, g_ref, x1_ref, h_ref):
    half = oa_ref.shape[1]
    y = _dot(oa_ref[...], w_ref[0:half, :]) + _dot(ob_ref[...], w_ref[half:, :])
    x1 = x_ref[...] + gt_ref[...] * y
    x1_ref[...] = x1
    h_ref[...] = (_rms(x1) * g_ref[...] * (1.0 + sc_ref[...]) + sh_ref[...]).astype(_BF)


def _out_projection(x, oa, ob, w_out_bf, gate, shift, scale, gain, *, per_row, seq):
    t_all, d = x.shape
    half = oa.shape[1]
    tm = min(ROW_TILE, t_all)
    mod_spec = _row_mod_specs(per_row, tm, d, max(seq // tm, 1))
    rows = lambda w: pl.BlockSpec((tm, w), lambda t: (t, 0))
    return pl.pallas_call(
        _outproj_kernel,
        grid=(t_all // tm,),
        in_specs=[rows(d), rows(half), rows(half), _resident(w_out_bf.shape, lambda t: (0, 0)),
                  mod_spec, mod_spec, mod_spec, _resident((1, d), lambda t: (0, 0))],
        out_specs=[rows(d), rows(d)],
        out_shape=[jax.ShapeDtypeStruct((t_all, d), _F32), jax.ShapeDtypeStruct((t_all, d), _BF)],
        compiler_params=_params("arbitrary"),
        name="out_projection",
    )(x, oa, ob, w_out_bf, gate, shift, scale, gain)


def _top_values(s, k):
    vals = []
    for _ in range(k):
        m = jnp.max(s, axis=0, keepdims=True)
        vals.append(m)
        s = jnp.where(s == m, -jnp.inf, s)
    return vals


def _stack_rows(vals):
    rows = lax.broadcasted_iota(jnp.int32, (len(vals), LANES), 0)
    out = jnp.zeros((len(vals), LANES), _F32)
    for i, v in enumerate(vals):
        out = jnp.where(rows == i, v, out)
    return out


def _route_kernel(ht_ref, wq_ref, keys_ref, th_ref, e1_ref, s2_ref, e2_ref, q_scr, s_scr):
    tt = ht_ref.shape[1]
    q_scr[...] = _dot(wq_ref[...], ht_ref[...]).astype(_BF)
    for hm in range(2 * PK_HEADS):
        s_scr[hm] = _dot(keys_ref[hm], q_scr[hm * PK_HALF:(hm + 1) * PK_HALF, :])

    def head(h, carry):
        for c in range(tt // LANES):
            cols = slice(c * LANES, (c + 1) * LANES)
            s1 = s_scr[2 * h, :, cols]
            s2 = s_scr[2 * h + 1, :, cols]
            a = _top_values(s1, PK_TOPK)
            b = _top_values(s2, PK_TOPK)
            amat = _stack_rows(a)
            bmat = _stack_rows(b)
            cand = [a[0] + bmat]
            cand += [a[p] + bmat[0:8] for p in range(1, 8)]
            cand += [amat[8:16] + b[0]]
            top = _top_values(jnp.concatenate(cand, axis=0), PK_TOPK)
            tau = top[PK_TOPK - 1]
            z = jnp.ones_like(tau)
            for k in range(1, PK_TOPK):
                z = z + jnp.exp(top[k] - top[0])
            theta = jnp.full(s1.shape, jnp.inf, _F32)
            for p in range(PK_TOPK):
                reach = jnp.min(jnp.where(a[p] + bmat >= tau, bmat, jnp.inf), axis=0, keepdims=True)
                theta = jnp.where(s1 == a[p], reach, theta)
            th_ref[h, :, cols] = theta
            e1_ref[h, :, cols] = jnp.exp(s1 - a[0]) / z
            s2_ref[h, :, cols] = s2
            e2_ref[h, :, cols] = jnp.exp(s2 - b[0])
        return carry

    lax.fori_loop(0, PK_HEADS, head, 0)


def _peer_route(ht, wq_t_bf, keys_bf):
    d, t_all = ht.shape
    tt = min(PEER_TOKENS, t_all)
    nq = wq_t_bf.shape[0]
    out = jax.ShapeDtypeStruct((PK_HEADS, N_KEYS, t_all), _F32)
    tile = pl.BlockSpec((PK_HEADS, N_KEYS, tt), lambda t: (0, 0, t))
    return pl.pallas_call(
        _route_kernel,
        grid=(t_all // tt,),
        in_specs=[pl.BlockSpec((d, tt), lambda t: (0, t)),
                  _resident(wq_t_bf.shape, lambda t: (0, 0)),
                  _resident(keys_bf.shape, lambda t: (0, 0, 0))],
        out_specs=[tile] * 4,
        out_shape=[out] * 4,
        scratch_shapes=[pltpu.VMEM((nq, tt), _BF), pltpu.VMEM((2 * PK_HEADS, N_KEYS, tt), _F32)],
        compiler_params=_params("arbitrary"),
        name="peer_route",
    )(ht, wq_t_bf, keys_bf)


def _gelu_tanh(x):
    return 0.5 * x * (1.0 + jnp.tanh(0.7978845608028654 * (x + 0.044715 * (x * x * x))))


def _peer_kernel(ht_ref, u_ref, vt_ref, th_ref, e1_ref, s2_ref, e2_ref, x_ref, gt_ref, gf_ref,
                 o_ref, at_scr, wt_scr, acc, *, final_norm):
    e = pl.program_id(1)
    tt = ht_ref.shape[1]
    groups = u_ref.shape[0] // N_KEYS

    @pl.when(e == 0)
    def _():
        acc[...] = jnp.zeros(acc.shape, _F32)

    at_scr[...] = _dot(u_ref[...], ht_ref[...])

    first = pl.ds(pl.multiple_of(e * groups, groups), groups)

    def chunk(c, carry):
        cols = pl.ds(pl.multiple_of(c * LANES, LANES), LANES)
        th = [th_ref[h, first, cols] for h in range(PK_HEADS)]
        e1 = [e1_ref[h, first, cols] for h in range(PK_HEADS)]
        for il in range(groups):
            rows = slice(il * N_KEYS, (il + 1) * N_KEYS)
            g = jnp.zeros((N_KEYS, LANES), _F32)
            for h in range(PK_HEADS):
                sel = jnp.where(s2_ref[h, :, cols] >= th[h][il:il + 1], e2_ref[h, :, cols], 0.0)
                g = g + sel * e1[h][il:il + 1]
            wt_scr[rows, cols] = (g * _gelu_tanh(at_scr[rows, cols])).astype(_BF)
        return carry

    lax.fori_loop(0, tt // LANES, chunk, 0)
    acc[...] += _dot(vt_ref[...], wt_scr[...])

    @pl.when(e == pl.num_programs(1) - 1)
    def _():
        y = x_ref[...] + gt_ref[...] * acc[...].T
        if final_norm:
            y = _rms(y) * gf_ref[...]
        o_ref[...] = y


def _peer_experts(ht, u_bf, vt_bf, route, x1, gate, g_final, *, per_row, seq, final_norm):
    d, t_all = ht.shape
    n_exp = u_bf.shape[0]
    tt = min(PEER_TOKENS, t_all)
    eb = PEER_EXPERTS
    gate_spec = (pl.BlockSpec((tt, d), lambda t, e: (t, 0)) if per_row else
                 pl.BlockSpec((None, 1, d), lambda t, e: (t // max(seq // tt, 1), 0, 0)))
    tile = pl.BlockSpec((PK_HEADS, N_KEYS, tt), lambda t, e: (0, 0, t))
    return pl.pallas_call(
        functools.partial(_peer_kernel, final_norm=final_norm),
        grid=(t_all // tt, n_exp // eb),
        in_specs=[pl.BlockSpec((d, tt), lambda t, e: (0, t)),
                  pl.BlockSpec((eb, d), lambda t, e: (e, 0)),
                  pl.BlockSpec((d, eb), lambda t, e: (0, e)),
                  tile, tile, tile, tile,
                  pl.BlockSpec((tt, d), lambda t, e: (t, 0)),
                  gate_spec,
                  _resident((1, d), lambda t, e: (0, 0))],
        out_specs=pl.BlockSpec((tt, d), lambda t, e: (t, 0)),
        out_shape=jax.ShapeDtypeStruct((t_all, d), _F32),
        scratch_shapes=[pltpu.VMEM((eb, tt), _F32), pltpu.VMEM((eb, tt), _BF), pltpu.VMEM((d, tt), _F32)],
        compiler_params=_params("arbitrary", "arbitrary"),
        name="peer_experts",
    )(ht, u_bf, vt_bf, *route, x1, gate, g_final)


def _rope_tables(pos):
    half = ROT_DIM // 2
    inv = ROPE_THETA ** (-jnp.arange(0, ROT_DIM, 2, dtype=_F32) / ROT_DIM)
    ang = pos.astype(_F32)[:, None] * inv[None, :]
    cos, sin = jnp.cos(ang), jnp.sin(ang)
    n = pos.shape[0]
    pad = jnp.zeros((n, A_DIM - ROT_DIM), _F32)
    zeros = jnp.zeros((n, half), _F32)
    c = jnp.concatenate([cos, cos, pad + 1.0], axis=1)
    sa = jnp.concatenate([-sin, zeros, pad], axis=1)
    sb = jnp.concatenate([zeros, sin, pad], axis=1)
    return tuple(jnp.tile(t, (1, LANES // A_DIM)) for t in (c, sa, sb))


def _stream_layer(x, mod, lw, rope_tabs, mixer, *, per_row, seq, final_norm, g_final):
    shift1, scale1, gate1, shift2, scale2, gate2 = mod
    proj = _in_projection(x, shift1, scale1, lw["g_attn"], lw["w_in"], rope_tabs, per_row=per_row, seq=seq)
    qa, kaf, vaf, kab, vab, qb, kbf, vbf, kbb, vbb = proj
    oa, ob, extra = mixer(qa, kaf, vaf, kab, vab, qb, kbf, vbf, kbb, vbb)
    x1, h2 = _out_projection(x, oa, ob, lw["w_out"], gate1, shift2, scale2, lw["g_ffn"],
                             per_row=per_row, seq=seq)
    ht = h2.T
    route = _peer_route(ht, lw["pk_wq_t"], lw["pk_keys"])
    x2 = _peer_experts(ht, lw["pk_u"], lw["pk_v_t"], route, x1, gate2, g_final,
                       per_row=per_row, seq=seq, final_norm=final_norm)
    return x2, (kaf, vaf, kbf, vbf) + extra


def kernel(x_prompt, x_sample, c_prompt, c_sample, cache_a_k, cache_a_v, cache_b_k, cache_b_v, w_ada, b_ada, g_attn, g_ffn, w_in, lam_q1, lam_k1, lam_q2, lam_k2, a_gain, rel_bias, b_gain, w_out, pk_wq, pk_keys, pk_u, pk_v, g_final):
    batch, seq, d = x_prompt.shape
    nb, t_new, _ = x_sample.shape
    depth = w_ada.shape[0]
    past = cache_a_k.shape[2]
    wb = cache_b_k.shape[2]
    a_width = A_HEADS * 2 * A_DIM
    b_width = B_HEADS * B_DIM

    c_all = jnp.concatenate([c_prompt, c_sample], axis=0)
    pad_rows = (-c_all.shape[0]) % 16
    c_all = jnp.pad(c_all, ((0, pad_rows), (0, 0)))
    mod = _modulation(c_all, w_ada, b_ada)

    tabs_p = _rope_tables(jnp.arange(seq, dtype=jnp.int32))
    pos_s = past + jnp.arange(t_new, dtype=jnp.int32)
    tabs_s = tuple(jnp.tile(t, (nb, 1)) for t in _rope_tables(pos_s))

    cak = cache_a_k.reshape(depth, nb, past, a_width)
    cav = cache_a_v.reshape(depth, nb, past, a_width)
    cbk = cache_b_k.reshape(depth, nb, wb, b_width)
    cbv = cache_b_v.reshape(depth, nb, wb, b_width)

    xp = x_prompt.reshape(batch * seq, d)
    xs = x_sample.reshape(nb * t_new, d)
    g_fin = g_final.reshape(1, d)
    st_p, st_s = [], []
    for l in range(depth):
        lam_init = 0.8 - 0.6 * math.exp(-0.3 * l)
        lamv = jnp.stack([lam_q1[l], lam_k1[l], lam_q2[l], lam_k2[l]])
        lw = {
            "g_attn": g_attn[l].reshape(1, d),
            "g_ffn": g_ffn[l].reshape(1, d),
            "w_in": w_in[l].astype(_BF),
            "w_out": w_out[l].astype(_BF),
            "pk_wq_t": pk_wq[l].T.astype(_BF),
            "pk_keys": pk_keys[l].reshape(2 * PK_HEADS, N_KEYS, PK_HALF).astype(_BF),
            "pk_u": pk_u[l].astype(_BF),
            "pk_v_t": pk_v[l].T.astype(_BF),
        }
        gain_a = a_gain[l].reshape(1, 2 * A_DIM)
        gain_b = b_gain[l].reshape(1, b_width)
        mod_p = tuple(m.reshape(batch, 1, d) for m in jnp.split(mod[l, :batch], 6, axis=-1))
        mod_s = tuple(jnp.repeat(m, t_new, axis=0) for m in jnp.split(mod[l, batch:batch + nb], 6, axis=-1))
        last = l == depth - 1

        def mix_p(qa, kaf, vaf, kab, vab, qb, kbf, vbf, kbb, vbb):
            oa = _attn_a_prompt(qa, kab, vab, lamv, gain_a, batch=batch, seq=seq, lam_init=lam_init)
            ob = _attn_b_prompt(qb, kbb, vbb, rel_bias[l], gain_b, batch=batch, seq=seq)
            return oa, ob, ()

        def mix_s(qa, kaf, vaf, kab, vab, qb, kbf, vbf, kbb, vbb):
            oa = _attn_a_sample(qa, cak, cav, kaf, vaf, lamv, gain_a, layer=l, lam_init=lam_init, t_new=t_new)
            ob, nbk, nbv = _attn_b_sample(qb, cbk, cbv, kbf, vbf, rel_bias[l], gain_b, layer=l, t_new=t_new)
            return oa, ob, (nbk, nbv)

        xp, sp = _stream_layer(xp, mod_p, lw, tabs_p, mix_p, per_row=False, seq=seq,
                               final_norm=last, g_final=g_fin)
        xs, ss = _stream_layer(xs, mod_s, lw, tabs_s, mix_s, per_row=True, seq=t_new,
                               final_norm=last, g_final=g_fin)
        st_p.append(sp)
        st_s.append(ss)

    keep = min(B_REACH, seq)
    a_shape_p = (batch, seq, A_HEADS, 2 * A_DIM)
    b_tail = lambda s: s.reshape(batch, seq, B_HEADS, B_DIM)[:, seq - keep:]
    a_shape_s = (nb, t_new, A_HEADS, 2 * A_DIM)
    b_shape_s = (nb, wb, B_HEADS, B_DIM)
    return (
        xp.reshape(batch, seq, d),
        xs.reshape(nb, t_new, d),
        jnp.stack([s[0].reshape(a_shape_p) for s in st_p]),
        jnp.stack([s[1].reshape(a_shape_p) for s in st_p]),
        jnp.stack([b_tail(s[2]) for s in st_p]),
        jnp.stack([b_tail(s[3]) for s in st_p]),
        jnp.stack([s[0].reshape(a_shape_s) for s in st_s]),
        jnp.stack([s[1].reshape(a_shape_s) for s in st_s]),
        jnp.stack([s[4].reshape(b_shape_s) for s in st_s]),
        jnp.stack([s[5].reshape(b_shape_s) for s in st_s]),
    )
```

```python
import functools
import math

import jax
import jax.numpy as jnp
import numpy as np
from jax import lax
from jax.experimental import pallas as pl
from jax.experimental.pallas import tpu as pltpu

_F32 = jnp.float32
_BF = jnp.bfloat16

CHUNK = 64
A_HEADS = 4
A_DIM = 64
ROT_DIM = A_DIM // 4
ROPE_THETA = 500000.0
B_HEADS = 8
B_DIM = 64
LEFT_CHUNKS = 8
B_REACH = LEFT_CHUNKS * CHUNK
REL_CLIP = 128
PK_HEADS = 8
N_KEYS = 128
PK_TOPK = 16
PK_HALF = 128
EPS = 1e-6

LANES = 128
MASKED = -1e30
VMEM_LIMIT = 56 * 1024 * 1024

A_TILE = 512
B_TILE = 256
PEER_TOKENS = 512
PEER_EXPERTS = 1024
PEER_CHUNK = 256
ROW_TILE = 512


def _params(*sem):
    return pltpu.CompilerParams(dimension_semantics=sem, vmem_limit_bytes=VMEM_LIMIT)


def _resident(shape, index_map):
    return pl.BlockSpec(shape, index_map, pipeline_mode=pl.Buffered(1))


def _rms(x):
    return x * lax.rsqrt(jnp.mean(x * x, axis=-1, keepdims=True) + EPS)


def _dot(a, b):
    return jnp.dot(a, b, preferred_element_type=_F32)


def _dot_nt(a, b):
    return lax.dot_general(a, b, (((1,), (1,)), ((), ())), preferred_element_type=_F32)


def _mod_kernel(c_ref, w_ref, b_ref, o_ref):
    c = c_ref[...]
    s = c * (1.0 / (1.0 + jnp.exp(-c)))
    o_ref[...] = _dot(s.astype(_BF), w_ref[...].astype(_BF)) + b_ref[...]


def _modulation(c_all, w_ada, b_ada):
    depth, d, n = w_ada.shape
    rows = c_all.shape[0]
    tn = n // 4
    return pl.pallas_call(
        _mod_kernel,
        grid=(depth, n // tn),
        in_specs=[
            pl.BlockSpec((rows, d), lambda l, j: (0, 0)),
            pl.BlockSpec((None, d, tn), lambda l, j: (l, 0, j)),
            pl.BlockSpec((None, 1, tn), lambda l, j: (l, 0, j)),
        ],
        out_specs=pl.BlockSpec((None, rows, tn), lambda l, j: (l, 0, j)),
        out_shape=jax.ShapeDtypeStruct((depth, rows, n), _F32),
        compiler_params=_params("arbitrary", "arbitrary"),
        name="adaln_mod",
    )(c_all, w_ada, b_ada.reshape(depth, 1, n))


def _inproj_kernel(x_ref, sh_ref, sc_ref, g_ref, w_ref, cos_ref, sa_ref, sb_ref,
                   qa_ref, kaf_ref, vaf_ref, kab_ref, vab_ref,
                   qb_ref, kbf_ref, vbf_ref, kbb_ref, vbb_ref, *, width):
    h = _rms(x_ref[...]) * g_ref[...] * (1.0 + sc_ref[...]) + sh_ref[...]
    hb = h.astype(_BF)
    reps = width // LANES
    cos = jnp.tile(cos_ref[...], (1, reps))
    sa = jnp.tile(sa_ref[...], (1, reps))
    sb = jnp.tile(sb_ref[...], (1, reps))

    def rope(z):
        return z * cos + pltpu.roll(z, width - ROT_DIM // 2, 1) * sa + pltpu.roll(z, ROT_DIM // 2, 1) * sb

    def proj(i):
        return _dot(hb, w_ref[:, i * width:(i + 1) * width])

    qa_ref[...] = (rope(proj(0)) * (A_DIM ** -0.5)).astype(_BF)
    ka = rope(proj(1))
    kaf_ref[...] = ka
    kab_ref[...] = ka.astype(_BF)
    va = proj(2)
    vaf_ref[...] = va
    vab_ref[...] = va.astype(_BF)
    qb_ref[...] = (proj(3) * (B_DIM ** -0.5)).astype(_BF)
    kb = proj(4)
    kbf_ref[...] = kb
    kbb_ref[...] = kb.astype(_BF)
    vb = proj(5)
    vbf_ref[...] = vb
    vbb_ref[...] = vb.astype(_BF)


def _row_mod_specs(per_row, tm, d, tiles_per_batch):
    if per_row:
        return pl.BlockSpec((tm, d), lambda t: (t, 0))
    return pl.BlockSpec((None, 1, d), lambda t: (t // tiles_per_batch, 0, 0))


def _in_projection(x, shift, scale, gain, w_in_bf, rope_tabs, *, per_row, seq):
    t_all, d = x.shape
    width = w_in_bf.shape[1] // 6
    tm = min(ROW_TILE, t_all)
    tiles_per_batch = max(seq // tm, 1)
    tab_tiles = rope_tabs[0].shape[0] // tm
    mod_spec = _row_mod_specs(per_row, tm, d, tiles_per_batch)
    tab_spec = pl.BlockSpec((tm, LANES), lambda t: (t % tab_tiles, 0))
    row = lambda dt: jax.ShapeDtypeStruct((t_all, width), dt)
    out_spec = pl.BlockSpec((tm, width), lambda t: (t, 0))
    dts = [_BF, _F32, _F32, _BF, _BF, _BF, _F32, _F32, _BF, _BF]
    return pl.pallas_call(
        functools.partial(_inproj_kernel, width=width),
        grid=(t_all // tm,),
        in_specs=[
            pl.BlockSpec((tm, d), lambda t: (t, 0)),
            mod_spec, mod_spec,
            _resident((1, d), lambda t: (0, 0)),
            _resident(w_in_bf.shape, lambda t: (0, 0)),
            tab_spec, tab_spec, tab_spec,
        ],
        out_specs=[out_spec] * 10,
        out_shape=[row(dt) for dt in dts],
        compiler_params=_params("arbitrary"),
        name="in_projection",
    )(x, shift, scale, gain, w_in_bf, *rope_tabs)


def _diff_lambda(lamv_ref, lam_init):
    v = lamv_ref[...]
    d1 = jnp.sum(v[0:1] * v[1:2], axis=1, keepdims=True)
    d2 = jnp.sum(v[2:3] * v[3:4], axis=1, keepdims=True)
    return jnp.exp(d1) - jnp.exp(d2) + lam_init


def _split_maps(q):
    lane = lax.broadcasted_iota(jnp.int32, q.shape, 1)
    zero = jnp.zeros_like(q)
    return jnp.where(lane < A_DIM, q, zero), jnp.where(lane >= A_DIM, q, zero)


def _attn_a_kernel(lamv_ref, q_ref, k_ref, v_ref, gain_ref, o_ref,
                   sa, sb, m1, l1, acc1, m2, l2, acc2, *, lam_init):
    n_full = pl.program_id(2)
    tq = q_ref.shape[0]
    q1, q2 = _split_maps(q_ref[...])
    for m_scr, l_scr, acc in ((m1, l1, acc1), (m2, l2, acc2)):
        m_scr[...] = jnp.full(m_scr.shape, MASKED, _F32)
        l_scr[...] = jnp.zeros(l_scr.shape, _F32)
        acc[...] = jnp.zeros(acc.shape, _F32)

    def rows(j):
        return pl.ds(pl.multiple_of(j * tq, tq), tq)

    def scores(j, s_scr):
        kb = k_ref[rows(j), :]
        s_scr[0] = _dot_nt(q1, kb)
        s_scr[1] = _dot_nt(q2, kb)

    def update(s, m_scr, l_scr, acc, vb):
        m_prev = m_scr[...]
        m_next = jnp.maximum(m_prev, jnp.max(s, axis=1, keepdims=True))
        p = jnp.exp(s - jnp.tile(m_next, (1, tq // LANES)))
        alpha = jnp.exp(m_prev - m_next)
        part = p[:, 0:LANES]
        for c in range(1, tq // LANES):
            part = part + p[:, c * LANES:(c + 1) * LANES]
        l_scr[...] = alpha * l_scr[...] + part
        acc[...] = acc[...] * alpha + _dot(p.astype(_BF), vb)
        m_scr[...] = m_next

    def process(j, s_scr, diagonal):
        vb = v_ref[rows(j), :]
        for mp, (m_scr, l_scr, acc) in enumerate(((m1, l1, acc1), (m2, l2, acc2))):
            s = s_scr[mp]
            if diagonal:
                qc = lax.broadcasted_iota(jnp.int32, s.shape, 0) // CHUNK
                kc = lax.broadcasted_iota(jnp.int32, s.shape, 1) // CHUNK
                s = jnp.where(kc <= qc, s, MASKED)
            update(s, m_scr, l_scr, acc, vb)

    scores(0, sa)

    def pair(i, carry):
        scores(2 * i + 1, sb)
        process(2 * i, sa, False)
        scores(2 * i + 2, sa)
        process(2 * i + 1, sb, False)
        return carry

    lax.fori_loop(0, n_full // 2, pair, 0)

    @pl.when(n_full % 2 == 1)
    def _():
        scores(n_full, sb)
        process(n_full - 1, sa, False)
        process(n_full, sb, True)

    @pl.when(n_full % 2 == 0)
    def _():
        process(n_full, sa, True)

    lam = _diff_lambda(lamv_ref, lam_init)
    d1 = jnp.sum(l1[...], axis=1, keepdims=True)
    d2 = jnp.sum(l2[...], axis=1, keepdims=True)
    o = acc1[...] / d1 - lam * (acc2[...] / d2)
    o_ref[...] = (_rms(o) * gain_ref[...] * (1.0 - lam_init)).astype(_BF)


def _attn_a_prompt(qa, ka, va, lamv, a_gain, *, batch, seq, lam_init):
    t_all, width = qa.shape
    hd = 2 * A_DIM
    tq = A_TILE
    nq = seq // tq
    return pl.pallas_call(
        functools.partial(_attn_a_kernel, lam_init=lam_init),
        grid=(batch, A_HEADS, nq),
        in_specs=[
            _resident(lamv.shape, lambda b, h, i: (0, 0)),
            pl.BlockSpec((tq, hd), lambda b, h, i: (b * nq + i, h)),
            pl.BlockSpec((seq, hd), lambda b, h, i: (b, h)),
            pl.BlockSpec((seq, hd), lambda b, h, i: (b, h)),
            _resident((1, hd), lambda b, h, i: (0, 0)),
        ],
        out_specs=pl.BlockSpec((tq, hd), lambda b, h, i: (b * nq + i, h)),
        out_shape=jax.ShapeDtypeStruct((t_all, width), _BF),
        scratch_shapes=[pltpu.VMEM((2, tq, tq), _F32)] * 2 + [pltpu.VMEM((tq, hd), _F32)] * 6,
        compiler_params=_params("arbitrary", "arbitrary", "arbitrary"),
        name="diff_attn_prompt",
    )(lamv, qa, ka, va, a_gain)


def _head_lanes(shape, head, dim):
    lane = lax.broadcasted_iota(jnp.int32, shape, 1)
    return (lane >= head * dim) & (lane < (head + 1) * dim)


def _attn_b_kernel(q_ref, k0_ref, k1_ref, k2_ref, v0_ref, v1_ref, v2_ref, bias_ref, gain_ref, o_ref):
    qi = pl.program_id(1)
    tq = q_ref.shape[0]
    k_refs = (k0_ref, k1_ref, k2_ref)
    v_refs = (v0_ref, v1_ref, v2_ref)
    pen = (jnp.where(qi >= 2, 0.0, MASKED), jnp.where(qi >= 1, 0.0, MASKED), 0.0)
    outs = []
    for pair in range(B_HEADS // 2):
        cols = slice(pair * LANES, (pair + 1) * LANES)
        qp = q_ref[:, cols]
        zero = jnp.zeros_like(qp)
        halves = []
        for sub in range(2):
            head = 2 * pair + sub
            qh = jnp.where(_head_lanes(qp.shape, sub, B_DIM), qp, zero)
            s = [_dot_nt(qh, k_refs[b][:, cols]) + bias_ref[head, :, b * tq:(b + 1) * tq] + pen[b]
                 for b in range(3)]
            m = jnp.maximum(jnp.maximum(jnp.max(s[0], axis=1, keepdims=True),
                                        jnp.max(s[1], axis=1, keepdims=True)),
                            jnp.max(s[2], axis=1, keepdims=True))
            p = [jnp.exp(sb - m) for sb in s]
            l = (jnp.sum(p[0], axis=1, keepdims=True) + jnp.sum(p[1], axis=1, keepdims=True)
                 + jnp.sum(p[2], axis=1, keepdims=True))
            o = (_dot(p[0].astype(_BF), v_refs[0][:, cols]) + _dot(p[1].astype(_BF), v_refs[1][:, cols])
                 + _dot(p[2].astype(_BF), v_refs[2][:, cols]))
            halves.append(o / l)
        outs.append(jnp.where(_head_lanes(halves[0].shape, 0, B_DIM), halves[0], halves[1]))
    o = jnp.concatenate(outs, axis=1)
    o_ref[...] = (_rms(o) * gain_ref[...]).astype(_BF)


def _band_bias(table, tq):
    nk = B_REACH + tq
    qpos = np.arange(tq)[:, None]
    kpos = np.arange(nk)[None, :] - B_REACH
    qc = qpos // CHUNK
    kc = np.floor_divide(kpos, CHUNK)
    valid = (kc <= qc) & (kc >= qc - LEFT_CHUNKS)
    span = nk + tq - 1
    dist = (span - 1 - np.arange(span)) - (tq - 1)
    line = table[:, np.clip(dist, -REL_CLIP, REL_CLIP) + REL_CLIP].astype(_F32)
    line = jnp.pad(line, ((0, 0), (0, 1)))
    skew = jnp.tile(line, (1, tq + 1))[:, :tq * (span + 2)].reshape(-1, tq, span + 2)
    bias = skew[:, ::-1, :nk]
    return jnp.where(valid[None], bias, MASKED)


def _attn_b_prompt(qb, kb, vb, rel_table, b_gain, *, batch, seq):
    t_all, width = qb.shape
    tq = B_TILE
    assert B_REACH == 2 * tq
    nq = seq // tq
    bias = _band_bias(rel_table, tq)
    qmap = lambda b, i: (b * nq + i, 0)
    kmap = lambda back: (lambda b, i: (b * nq + jnp.maximum(i - back, 0), 0))
    blk = lambda f: pl.BlockSpec((tq, width), f)
    return pl.pallas_call(
        _attn_b_kernel,
        grid=(batch, nq),
        in_specs=[blk(qmap), blk(kmap(2)), blk(kmap(1)), blk(kmap(0)),
                  blk(kmap(2)), blk(kmap(1)), blk(kmap(0)),
                  _resident(bias.shape, lambda b, i: (0, 0, 0)),
                  _resident((1, width), lambda b, i: (0, 0))],
        out_specs=blk(qmap),
        out_shape=jax.ShapeDtypeStruct((t_all, width), _BF),
        compiler_params=_params("arbitrary", "arbitrary"),
        name="band_attn_prompt",
    )(qb, kb, kb, kb, vb, vb, vb, bias, b_gain)


def _attn_a_sample_kernel(lamv_ref, q_ref, kc_ref, vc_ref, kn_ref, vn_ref, gain_ref, o_ref, *, lam_init):
    q = q_ref[...]
    t = q.shape[0]
    hd = 2 * A_DIM
    past = kc_ref.shape[0] // A_HEADS
    lam = _diff_lambda(lamv_ref, lam_init)
    outs = []
    for h in range(A_HEADS):
        cols = slice(h * hd, (h + 1) * hd)
        head_rows = pl.ds(h, past, stride=A_HEADS)
        q12 = jnp.concatenate(_split_maps(q[:, cols]), axis=0)
        kn = kn_ref[:, cols].astype(_BF)
        vn = vn_ref[:, cols].astype(_BF)
        s_c = _dot_nt(q12, kc_ref[head_rows, :].astype(_BF))
        s_n = _dot_nt(q12, kn)
        m = jnp.maximum(jnp.max(s_c, axis=1, keepdims=True), jnp.max(s_n, axis=1, keepdims=True))
        p_c = jnp.exp(s_c - m)
        p_n = jnp.exp(s_n - m)
        l = jnp.sum(p_c, axis=1, keepdims=True) + jnp.sum(p_n, axis=1, keepdims=True)
        p_c = p_c / l
        p_n = p_n / l
        a_c = p_c[0:t] - lam * p_c[t:2 * t]
        a_n = p_n[0:t] - lam * p_n[t:2 * t]
        o = _dot(a_c.astype(_BF), vc_ref[head_rows, :].astype(_BF)) + _dot(a_n.astype(_BF), vn)
        outs.append(_rms(o) * gain_ref[...] * (1.0 - lam_init))
    o_ref[...] = jnp.concatenate(outs, axis=1).astype(_BF)


def _attn_a_sample(qa, cache_k, cache_v, kn, vn, lamv, a_gain, *, layer, lam_init, t_new):
    t_all, width = qa.shape
    _, nb, cache_rows, hd = cache_k.shape
    tok = pl.BlockSpec((t_new, width), lambda b: (b, 0))
    cache = pl.BlockSpec((None, None, cache_rows, hd), lambda b: (layer, b, 0, 0))
    return pl.pallas_call(
        functools.partial(_attn_a_sample_kernel, lam_init=lam_init),
        grid=(nb,),
        in_specs=[_resident(lamv.shape, lambda b: (0, 0)), tok, cache, cache, tok, tok,
                  _resident((1, 2 * A_DIM), lambda b: (0, 0))],
        out_specs=tok,
        out_shape=jax.ShapeDtypeStruct((t_all, width), _BF),
        compiler_params=_params("arbitrary"),
        name="diff_attn_sample",
    )(lamv, qa, cache_k, cache_v, kn, vn, a_gain)


def _attn_b_sample_kernel(q_ref, kc_ref, vc_ref, kn_ref, vn_ref, bc_ref, bn_ref, gain_ref,
                          o_ref, ko_ref, vo_ref):
    q = q_ref[...]
    t = q.shape[0]
    keep = kc_ref.shape[0] - t
    zero = jnp.zeros_like(q)
    qall = jnp.concatenate(
        [jnp.where(_head_lanes(q.shape, h, B_DIM), q, zero) for h in range(B_HEADS)], axis=0)
    kc = kc_ref[...]
    vc = vc_ref[...]
    kn = kn_ref[...]
    vn = vn_ref[...]
    s_c = _dot_nt(qall, kc.astype(_BF)) + bc_ref[...]
    s_n = _dot_nt(qall, kn.astype(_BF)) + bn_ref[...]
    m = jnp.maximum(jnp.max(s_c, axis=1, keepdims=True), jnp.max(s_n, axis=1, keepdims=True))
    p_c = jnp.exp(s_c - m)
    p_n = jnp.exp(s_n - m)
    l = jnp.sum(p_c, axis=1, keepdims=True) + jnp.sum(p_n, axis=1, keepdims=True)
    o_all = (_dot(p_c.astype(_BF), vc.astype(_BF)) + _dot(p_n.astype(_BF), vn.astype(_BF))) / l
    o = jnp.zeros((t, q.shape[1]), _F32)
    for h in range(B_HEADS):
        o = jnp.where(_head_lanes(o.shape, h, B_DIM), o_all[h * t:(h + 1) * t], o)
    o_ref[...] = (_rms(o) * gain_ref[...]).astype(_BF)
    ko_ref[0:keep, :] = kc[t:, :]
    ko_ref[keep:, :] = kn
    vo_ref[0:keep, :] = vc[t:, :]
    vo_ref[keep:, :] = vn


def _sample_bias(table, t_new, wb):
    qpos = jnp.arange(t_new)[:, None] + wb
    kpos = jnp.arange(wb + t_new)[None, :]
    d = jnp.clip(qpos - kpos, -REL_CLIP, REL_CLIP) + REL_CLIP
    bias = table[:, d].astype(_F32).reshape(B_HEADS * t_new, wb + t_new)
    return bias[:, :wb], bias[:, wb:]


def _attn_b_sample(qb, cache_k, cache_v, kn, vn, rel_table, b_gain, *, layer, t_new):
    t_all, width = qb.shape
    _, nb, wb, _ = cache_k.shape
    bias_c, bias_n = _sample_bias(rel_table, t_new, wb)
    tok = pl.BlockSpec((t_new, width), lambda b: (b, 0))
    cache = pl.BlockSpec((None, None, wb, width), lambda b: (layer, b, 0, 0))
    state = pl.BlockSpec((None, wb, width), lambda b: (b, 0, 0))
    return pl.pallas_call(
        _attn_b_sample_kernel,
        grid=(nb,),
        in_specs=[tok, cache, cache, tok, tok,
                  _resident(bias_c.shape, lambda b: (0, 0)), _resident(bias_n.shape, lambda b: (0, 0)),
                  _resident((1, width), lambda b: (0, 0))],
        out_specs=[tok, state, state],
        out_shape=[jax.ShapeDtypeStruct((t_all, width), _BF),
                   jax.ShapeDtypeStruct((nb, wb, width), _F32),
                   jax.ShapeDtypeStruct((nb, wb, width), _F32)],
        compiler_params=_params("arbitrary"),
        name="band_attn_sample",
    )(qb, cache_k, cache_v, kn, vn, bias_c, bias_n, b_gain)


def _outproj_kernel(x_ref, oa_ref, ob_ref, w_ref, gt_ref, sh_ref, sc_ref, g_ref, x1_ref, h_ref):
    half = oa_ref.shape[1]
    y = _dot(oa_ref[...], w_ref[0:half, :]) + _dot(ob_ref[...], w_ref[half:, :])
    x1 = x_ref[...] + gt_ref[...] * y
    x1_ref[...] = x1
    h_ref[...] = (_rms(x1) * g_ref[...] * (1.0 + sc_ref[...]) + sh_ref[...]).astype(_BF)


def _out_projection(x, oa, ob, w_out_bf, gate, shift, scale, gain, *, per_row, seq):
    t_all, d = x.shape
    half = oa.shape[1]
    tm = min(ROW_TILE, t_all)
    mod_spec = _row_mod_specs(per_row, tm, d, max(seq // tm, 1))
    rows = lambda w: pl.BlockSpec((tm, w), lambda t: (t, 0))
    return pl.pallas_call(
        _outproj_kernel,
        grid=(t_all // tm,),
        in_specs=[rows(d), rows(half), rows(half), _resident(w_out_bf.shape, lambda t: (0, 0)),
                  mod_spec, mod_spec, mod_spec, _resident((1, d), lambda t: (0, 0))],
        out_specs=[rows(d), rows(d)],
        out_shape=[jax.ShapeDtypeStruct((t_all, d), _F32), jax.ShapeDtypeStruct((t_all, d), _BF)],
        compiler_params=_params("arbitrary"),
        name="out_projection",
    )(x, oa, ob, w_out_bf, gate, shift, scale, gain)


def _top_values(s, k, want_rank=False):
    vals = []
    rank = jnp.full(s.shape, float(k), _F32)
    for i in range(k):
        m = jnp.max(s, axis=0, keepdims=True)
        vals.append(m)
        hit = s == m
        if want_rank:
            rank = jnp.where(hit, float(i), rank)
        s = jnp.where(hit, -jnp.inf, s)
    return (vals, rank) if want_rank else vals


def _stack_rows(vals):
    rows = lax.broadcasted_iota(jnp.int32, (len(vals), LANES), 0)
    out = jnp.zeros((len(vals), LANES), _F32)
    for i, v in enumerate(vals):
        out = jnp.where(rows == i, v, out)
    return out


def _route_kernel(ht_ref, wq_ref, keys_ref, n1_ref, e1_ref, r2_ref, e2_ref, q_scr, s_scr):
    tt = ht_ref.shape[1]
    q_scr[...] = _dot(wq_ref[...], ht_ref[...]).astype(_BF)
    for hm in range(2 * PK_HEADS):
        s_scr[hm] = _dot(keys_ref[hm], q_scr[hm * PK_HALF:(hm + 1) * PK_HALF, :])

    def head(h, carry):
        for c in range(tt // LANES):
            cols = slice(c * LANES, (c + 1) * LANES)
            s1 = s_scr[2 * h, :, cols]
            s2 = s_scr[2 * h + 1, :, cols]
            a, rank1 = _top_values(s1, PK_TOPK, want_rank=True)
            b, rank2 = _top_values(s2, PK_TOPK, want_rank=True)
            amat = _stack_rows(a)
            bmat = _stack_rows(b)
            cand = [a[0] + bmat]
            cand += [a[p] + bmat[0:8] for p in range(1, 8)]
            cand += [amat[8:16] + b[0]]
            top = _top_values(jnp.concatenate(cand, axis=0), PK_TOPK)
            tau = top[PK_TOPK - 1]
            z = jnp.ones_like(tau)
            for k in range(1, PK_TOPK):
                z = z + jnp.exp(top[k] - top[0])
            count1 = jnp.zeros(s1.shape, _F32)
            for p in range(PK_TOPK):
                reach = jnp.sum(jnp.where(a[p] + bmat >= tau, 1.0, 0.0), axis=0, keepdims=True)
                count1 = jnp.where(rank1 == float(p), reach, count1)
            n1_ref[h, :, cols] = count1
            e1_ref[h, :, cols] = jnp.exp(s1 - a[0]) * (0.5 / z)
            r2_ref[h, :, cols] = rank2.astype(_BF)
            e2_ref[h, :, cols] = jnp.exp(s2 - b[0]).astype(_BF)
        return carry

    lax.fori_loop(0, PK_HEADS, head, 0)


def _peer_route(ht, wq_t_bf, keys_bf):
    d, t_all = ht.shape
    tt = min(PEER_TOKENS, t_all)
    nq = wq_t_bf.shape[0]
    out = lambda dt: jax.ShapeDtypeStruct((PK_HEADS, N_KEYS, t_all), dt)
    tile = pl.BlockSpec((PK_HEADS, N_KEYS, tt), lambda t: (0, 0, t))
    return pl.pallas_call(
        _route_kernel,
        grid=(t_all // tt,),
        in_specs=[pl.BlockSpec((d, tt), lambda t: (0, t)),
                  _resident(wq_t_bf.shape, lambda t: (0, 0)),
                  _resident(keys_bf.shape, lambda t: (0, 0, 0))],
        out_specs=[tile] * 4,
        out_shape=[out(_F32), out(_F32), out(_BF), out(_BF)],
        scratch_shapes=[pltpu.VMEM((nq, tt), _BF), pltpu.VMEM((2 * PK_HEADS, N_KEYS, tt), _F32)],
        compiler_params=_params("arbitrary"),
        name="peer_route",
    )(ht, wq_t_bf, keys_bf)


def _twice_gelu_tanh(x):
    return x * (1.0 + jnp.tanh(x * (0.7978845608028654 + (0.7978845608028654 * 0.044715) * (x * x))))


def _row_to_tile(block, row):
    packed = jnp.broadcast_to(block[row:row + 1, :], (16, LANES)).astype(_BF)
    return jnp.tile(packed, (N_KEYS // 16, 1))


def _peer_kernel(ht_ref, u_ref, vt_ref, n1_ref, e1_ref, r2_ref, e2_ref, x_ref, gt_ref, gf_ref,
                 o_ref, at_scr, wt_scr, acc, r2_scr, e2_scr, *, final_norm):
    e = pl.program_id(1)
    tt = ht_ref.shape[1]
    groups = u_ref.shape[0] // N_KEYS
    span = at_scr.shape[2]
    n_chunks = tt // span

    @pl.when(e == 0)
    def _():
        acc[...] = jnp.zeros(acc.shape, _F32)
        r2_scr[...] = r2_ref[...]
        e2_scr[...] = e2_ref[...]

    first = pl.ds(pl.multiple_of(e * groups, groups), groups)

    for c in range(n_chunks):
        at_scr[c] = _dot(u_ref[...], ht_ref[:, c * span:(c + 1) * span])
    for c in range(n_chunks):
        for sub in range(span // LANES):
            cols = slice(c * span + sub * LANES, c * span + (sub + 1) * LANES)
            sub_cols = slice(sub * LANES, (sub + 1) * LANES)
            n1 = [n1_ref[h, first, cols] for h in range(PK_HEADS)]
            e1 = [e1_ref[h, first, cols] for h in range(PK_HEADS)]
            zero = jnp.zeros((N_KEYS, LANES), _BF)
            for il in range(groups):
                rows = slice(il * N_KEYS, (il + 1) * N_KEYS)
                g = None
                for h in range(PK_HEADS):
                    sel = jnp.where(r2_scr[h, :, cols] < _row_to_tile(n1[h], il), e2_scr[h, :, cols], zero)
                    term = sel * _row_to_tile(e1[h], il)
                    g = term if g is None else g + term
                act = _twice_gelu_tanh(at_scr[c, rows, sub_cols]).astype(_BF)
                wt_scr[c, rows, sub_cols] = g * act
        acc[c] += _dot(vt_ref[...], wt_scr[c])

    @pl.when(e == pl.num_programs(1) - 1)
    def _():
        f = jnp.concatenate([acc[c].T for c in range(n_chunks)], axis=0)
        y = x_ref[...] + gt_ref[...] * f
        if final_norm:
            y = _rms(y) * gf_ref[...]
        o_ref[...] = y


def _peer_experts(ht, u_bf, vt_bf, route, x1, gate, g_final, *, per_row, seq, final_norm):
    d, t_all = ht.shape
    n_exp = u_bf.shape[0]
    tt = min(PEER_TOKENS, t_all)
    eb = PEER_EXPERTS
    span = min(PEER_CHUNK, tt)
    assert eb == 8 * N_KEYS
    gate_spec =(pl.BlockSpec((tt, d), lambda t, e: (t, 0)) if per_row else
                 pl.BlockSpec((None, 1, d), lambda t, e: (t // max(seq // tt, 1), 0, 0)))
    tile = pl.BlockSpec((PK_HEADS, N_KEYS, tt), lambda t, e: (0, 0, t))
    return pl.pallas_call(
        functools.partial(_peer_kernel, final_norm=final_norm),
        grid=(t_all // tt, n_exp // eb),
        in_specs=[pl.BlockSpec((d, tt), lambda t, e: (0, t)),
                  pl.BlockSpec((eb, d), lambda t, e: (e, 0)),
                  pl.BlockSpec((d, eb), lambda t, e: (0, e)),
                  tile, tile, tile, tile,
                  pl.BlockSpec((tt, d), lambda t, e: (t, 0)),
                  gate_spec,
                  _resident((1, d), lambda t, e: (0, 0))],
        out_specs=pl.BlockSpec((tt, d), lambda t, e: (t, 0)),
        out_shape=jax.ShapeDtypeStruct((t_all, d), _F32),
        scratch_shapes=[pltpu.VMEM((tt // span, eb, span), _F32), pltpu.VMEM((tt // span, eb, span), _BF),
                        pltpu.VMEM((tt // span, d, span), _F32),
                        pltpu.VMEM((PK_HEADS, N_KEYS, tt), _BF), pltpu.VMEM((PK_HEADS, N_KEYS, tt), _BF)],
        compiler_params=_params("arbitrary", "arbitrary"),
        name="peer_experts",
    )(ht, u_bf, vt_bf, *route, x1, gate, g_final)


def _rope_tables(pos):
    half = ROT_DIM // 2
    inv = ROPE_THETA ** (-jnp.arange(0, ROT_DIM, 2, dtype=_F32) / ROT_DIM)
    ang = pos.astype(_F32)[:, None] * inv[None, :]
    cos, sin = jnp.cos(ang), jnp.sin(ang)
    n = pos.shape[0]
    pad = jnp.zeros((n, A_DIM - ROT_DIM), _F32)
    zeros = jnp.zeros((n, half), _F32)
    c = jnp.concatenate([cos, cos, pad + 1.0], axis=1)
    sa = jnp.concatenate([-sin, zeros, pad], axis=1)
    sb = jnp.concatenate([zeros, sin, pad], axis=1)
    return tuple(jnp.tile(t, (1, LANES // A_DIM)) for t in (c, sa, sb))


def _stream_layer(x, mod, lw, rope_tabs, mixer, *, per_row, seq, final_norm, g_final):
    shift1, scale1, gate1, shift2, scale2, gate2 = mod
    proj = _in_projection(x, shift1, scale1, lw["g_attn"], lw["w_in"], rope_tabs, per_row=per_row, seq=seq)
    qa, kaf, vaf, kab, vab, qb, kbf, vbf, kbb, vbb = proj
    oa, ob, extra = mixer(qa, kaf, vaf, kab, vab, qb, kbf, vbf, kbb, vbb)
    x1, h2 = _out_projection(x, oa, ob, lw["w_out"], gate1, shift2, scale2, lw["g_ffn"],
                             per_row=per_row, seq=seq)
    ht = h2.T
    route = _peer_route(ht, lw["pk_wq_t"], lw["pk_keys"])
    x2 = _peer_experts(ht, lw["pk_u"], lw["pk_v_t"], route, x1, gate2, g_final,
                       per_row=per_row, seq=seq, final_norm=final_norm)
    return x2, (kaf, vaf, kbf, vbf) + extra


def kernel(x_prompt, x_sample, c_prompt, c_sample, cache_a_k, cache_a_v, cache_b_k, cache_b_v, w_ada, b_ada, g_attn, g_ffn, w_in, lam_q1, lam_k1, lam_q2, lam_k2, a_gain, rel_bias, b_gain, w_out, pk_wq, pk_keys, pk_u, pk_v, g_final):
    batch, seq, d = x_prompt.shape
    nb, t_new, _ = x_sample.shape
    depth = w_ada.shape[0]
    past = cache_a_k.shape[2]
    wb = cache_b_k.shape[2]
    a_width = A_HEADS * 2 * A_DIM
    b_width = B_HEADS * B_DIM

    c_all = jnp.concatenate([c_prompt, c_sample], axis=0)
    pad_rows = (-c_all.shape[0]) % 16
    c_all = jnp.pad(c_all, ((0, pad_rows), (0, 0)))
    mod = _modulation(c_all, w_ada, b_ada)

    tabs_p = _rope_tables(jnp.arange(seq, dtype=jnp.int32))
    pos_s = past + jnp.arange(t_new, dtype=jnp.int32)
    tabs_s = tuple(jnp.tile(t, (nb, 1)) for t in _rope_tables(pos_s))

    cak = cache_a_k.reshape(depth, nb, past * A_HEADS, 2 * A_DIM)
    cav = cache_a_v.reshape(depth, nb, past * A_HEADS, 2 * A_DIM)
    cbk = cache_b_k.reshape(depth, nb, wb, b_width)
    cbv = cache_b_v.reshape(depth, nb, wb, b_width)

    xp = x_prompt.reshape(batch * seq, d)
    xs = x_sample.reshape(nb * t_new, d)
    g_fin = g_final.reshape(1, d)
    st_p, st_s = [], []
    for l in range(depth):
        lam_init = 0.8 - 0.6 * math.exp(-0.3 * l)
        lamv = jnp.stack([lam_q1[l], lam_k1[l], lam_q2[l], lam_k2[l]])
        lw = {
            "g_attn": g_attn[l].reshape(1, d),
            "g_ffn": g_ffn[l].reshape(1, d),
            "w_in": w_in[l].astype(_BF),
            "w_out": w_out[l].astype(_BF),
            "pk_wq_t": pk_wq[l].T.astype(_BF),
            "pk_keys": pk_keys[l].reshape(2 * PK_HEADS, N_KEYS, PK_HALF).astype(_BF),
            "pk_u": pk_u[l].astype(_BF),
            "pk_v_t": pk_v[l].T.astype(_BF),
        }
        gain_a = a_gain[l].reshape(1, 2 * A_DIM)
        gain_b = b_gain[l].reshape(1, b_width)
        mod_p = tuple(m.reshape(batch, 1, d) for m in jnp.split(mod[l, :batch], 6, axis=-1))
        mod_s = tuple(jnp.repeat(m, t_new, axis=0) for m in jnp.split(mod[l, batch:batch + nb], 6, axis=-1))
        last = l == depth - 1

        def mix_p(qa, kaf, vaf, kab, vab, qb, kbf, vbf, kbb, vbb):
            oa = _attn_a_prompt(qa, kab, vab, lamv, gain_a, batch=batch, seq=seq, lam_init=lam_init)
            ob = _attn_b_prompt(qb, kbb, vbb, rel_bias[l], gain_b, batch=batch, seq=seq)
            return oa, ob, ()

        def mix_s(qa, kaf, vaf, kab, vab, qb, kbf, vbf, kbb, vbb):
            oa = _attn_a_sample(qa, cak, cav, kaf, vaf, lamv, gain_a, layer=l, lam_init=lam_init, t_new=t_new)
            ob, nbk, nbv = _attn_b_sample(qb, cbk, cbv, kbf, vbf, rel_bias[l], gain_b, layer=l, t_new=t_new)
            return oa, ob, (nbk, nbv)

        xp, sp = _stream_layer(xp, mod_p, lw, tabs_p, mix_p, per_row=False, seq=seq,
                               final_norm=last, g_final=g_fin)
        xs, ss = _stream_layer(xs, mod_s, lw, tabs_s, mix_s, per_row=True, seq=t_new,
                               final_norm=last, g_final=g_fin)
        st_p.append(sp)
        st_s.append(ss)

    keep = min(B_REACH, seq)
    a_shape_p = (batch, seq, A_HEADS, 2 * A_DIM)
    b_tail = lambda s: s.reshape(batch, seq, B_HEADS, B_DIM)[:, seq - keep:]
    a_shape_s = (nb, t_new, A_HEADS, 2 * A_DIM)
    b_shape_s = (nb, wb, B_HEADS, B_DIM)
    return (
        xp.reshape(batch, seq, d),
        xs.reshape(nb, t_new, d),
        jnp.stack([s[0].reshape(a_shape_p) for s in st_p]),
        jnp.stack([s[1].reshape(a_shape_p) for s in st_p]),
        jnp.stack([b_tail(s[2]) for s in st_p]),
        jnp.stack([b_tail(s[3]) for s in st_p]),
        jnp.stack([s[0].reshape(a_shape_s) for s in st_s]),
        jnp.stack([s[1].reshape(a_shape_s) for s in st_s]),
        jnp.stack([s[4].reshape(b_shape_s) for s in st_s]),
        jnp.stack([s[5].reshape(b_shape_s) for s in st_s]),
    )
```

```python
import functools
import math

import jax
import jax.numpy as jnp
import numpy as np
from jax import lax
from jax.experimental import pallas as pl
from jax.experimental.pallas import tpu as pltpu

_F32 = jnp.float32
_BF = jnp.bfloat16

CHUNK = 64
A_HEADS = 4
A_DIM = 64
ROT_DIM = A_DIM // 4
ROPE_THETA = 500000.0
B_HEADS = 8
B_DIM = 64
LEFT_CHUNKS = 8
B_REACH = LEFT_CHUNKS * CHUNK
REL_CLIP = 128
PK_HEADS = 8
N_KEYS = 128
PK_TOPK = 16
PK_HALF = 128
EPS = 1e-6

LANES = 128
MASKED = -1e30
VMEM_LIMIT = 56 * 1024 * 1024

A_TILE = 512
B_TILE = 256
PEER_TOKENS = 512
PEER_EXPERTS = 1024
PEER_CHUNK = 256
ROW_TILE = 512


def _params(*sem):
    return pltpu.CompilerParams(dimension_semantics=sem, vmem_limit_bytes=VMEM_LIMIT)


def _resident(shape, index_map):
    return pl.BlockSpec(shape, index_map, pipeline_mode=pl.Buffered(1))


def _rms(x):
    return x * lax.rsqrt(jnp.mean(x * x, axis=-1, keepdims=True) + EPS)


def _dot(a, b):
    return jnp.dot(a, b, preferred_element_type=_F32)


def _dot_nt(a, b):
    return lax.dot_general(a, b, (((1,), (1,)), ((), ())), preferred_element_type=_F32)


def _mod_kernel(c_ref, w_ref, b_ref, o_ref):
    c = c_ref[...]
    s = c * (1.0 / (1.0 + jnp.exp(-c)))
    o_ref[...] = _dot(s.astype(_BF), w_ref[...].astype(_BF)) + b_ref[...]


def _modulation(c_all, w_ada, b_ada):
    depth, d, n = w_ada.shape
    rows = c_all.shape[0]
    tn = n // 4
    return pl.pallas_call(
        _mod_kernel,
        grid=(depth, n // tn),
        in_specs=[
            pl.BlockSpec((rows, d), lambda l, j: (0, 0)),
            pl.BlockSpec((None, d, tn), lambda l, j: (l, 0, j)),
            pl.BlockSpec((None, 1, tn), lambda l, j: (l, 0, j)),
        ],
        out_specs=pl.BlockSpec((None, rows, tn), lambda l, j: (l, 0, j)),
        out_shape=jax.ShapeDtypeStruct((depth, rows, n), _F32),
        compiler_params=_params("arbitrary", "arbitrary"),
        name="adaln_mod",
    )(c_all, w_ada, b_ada.reshape(depth, 1, n))


def _inproj_kernel(x_ref, sh_ref, sc_ref, g_ref, w_ref, cos_ref, sa_ref, sb_ref,
                   qa_ref, kaf_ref, vaf_ref, kab_ref, vab_ref,
                   qb_ref, kbf_ref, vbf_ref, kbb_ref, vbb_ref, *, width):
    h = _rms(x_ref[...]) * g_ref[...] * (1.0 + sc_ref[...]) + sh_ref[...]
    hb = h.astype(_BF)
    reps = width // LANES
    cos = jnp.tile(cos_ref[...], (1, reps))
    sa = jnp.tile(sa_ref[...], (1, reps))
    sb = jnp.tile(sb_ref[...], (1, reps))

    def rope(z):
        return z * cos + pltpu.roll(z, width - ROT_DIM // 2, 1) * sa + pltpu.roll(z, ROT_DIM // 2, 1) * sb

    def proj(i):
        return _dot(hb, w_ref[:, i * width:(i + 1) * width])

    qa_ref[...] = (rope(proj(0)) * (A_DIM ** -0.5)).astype(_BF)
    ka = rope(proj(1))
    kaf_ref[...] = ka
    kab_ref[...] = ka.astype(_BF)
    va = proj(2)
    vaf_ref[...] = va
    vab_ref[...] = va.astype(_BF)
    qb_ref[...] = (proj(3) * (B_DIM ** -0.5)).astype(_BF)
    kb = proj(4)
    kbf_ref[...] = kb
    kbb_ref[...] = kb.astype(_BF)
    vb = proj(5)
    vbf_ref[...] = vb
    vbb_ref[...] = vb.astype(_BF)


def _row_mod_specs(per_row, tm, d, tiles_per_batch):
    if per_row:
        return pl.BlockSpec((tm, d), lambda t: (t, 0))
    return pl.BlockSpec((None, 1, d), lambda t: (t // tiles_per_batch, 0, 0))


def _in_projection(x, shift, scale, gain, w_in_bf, rope_tabs, *, per_row, seq):
    t_all, d = x.shape
    width = w_in_bf.shape[1] // 6
    tm = min(ROW_TILE, t_all)
    tiles_per_batch = max(seq // tm, 1)
    tab_tiles = rope_tabs[0].shape[0] // tm
    mod_spec = _row_mod_specs(per_row, tm, d, tiles_per_batch)
    tab_spec = pl.BlockSpec((tm, LANES), lambda t: (t % tab_tiles, 0))
    row = lambda dt: jax.ShapeDtypeStruct((t_all, width), dt)
    out_spec = pl.BlockSpec((tm, width), lambda t: (t, 0))
    dts = [_BF, _F32, _F32, _BF, _BF, _BF, _F32, _F32, _BF, _BF]
    return pl.pallas_call(
        functools.partial(_inproj_kernel, width=width),
        grid=(t_all // tm,),
        in_specs=[
            pl.BlockSpec((tm, d), lambda t: (t, 0)),
            mod_spec, mod_spec,
            _resident((1, d), lambda t: (0, 0)),
            _resident(w_in_bf.shape, lambda t: (0, 0)),
            tab_spec, tab_spec, tab_spec,
        ],
        out_specs=[out_spec] * 10,
        out_shape=[row(dt) for dt in dts],
        compiler_params=_params("arbitrary"),
        name="in_projection",
    )(x, shift, scale, gain, w_in_bf, *rope_tabs)


def _diff_lambda(lamv_ref, lam_init):
    v = lamv_ref[...]
    d1 = jnp.sum(v[0:1] * v[1:2], axis=1, keepdims=True)
    d2 = jnp.sum(v[2:3] * v[3:4], axis=1, keepdims=True)
    return jnp.exp(d1) - jnp.exp(d2) + lam_init


def _split_maps(q):
    lane = lax.broadcasted_iota(jnp.int32, q.shape, 1)
    zero = jnp.zeros_like(q)
    return jnp.where(lane < A_DIM, q, zero), jnp.where(lane >= A_DIM, q, zero)


def _attn_a_kernel(lamv_ref, q_ref, k_ref, v_ref, gain_ref, o_ref,
                   sa, sb, m1, l1, acc1, m2, l2, acc2, *, lam_init):
    n_full = pl.program_id(2)
    tq = q_ref.shape[0]
    q1, q2 = _split_maps(q_ref[...])
    for m_scr, l_scr, acc in ((m1, l1, acc1), (m2, l2, acc2)):
        m_scr[...] = jnp.full(m_scr.shape, MASKED, _F32)
        l_scr[...] = jnp.zeros(l_scr.shape, _F32)
        acc[...] = jnp.zeros(acc.shape, _F32)

    def rows(j):
        return pl.ds(pl.multiple_of(j * tq, tq), tq)

    def scores(j, s_scr):
        kb = k_ref[rows(j), :]
        s_scr[0] = _dot_nt(q1, kb)
        s_scr[1] = _dot_nt(q2, kb)

    def update(s, m_scr, l_scr, acc, vb):
        m_prev = m_scr[...]
        m_next = jnp.maximum(m_prev, jnp.max(s, axis=1, keepdims=True))
        p = jnp.exp(s - jnp.tile(m_next, (1, tq // LANES)))
        alpha = jnp.exp(m_prev - m_next)
        part = p[:, 0:LANES]
        for c in range(1, tq // LANES):
            part = part + p[:, c * LANES:(c + 1) * LANES]
        l_scr[...] = alpha * l_scr[...] + part
        acc[...] = acc[...] * alpha + _dot(p.astype(_BF), vb)
        m_scr[...] = m_next

    def process(j, s_scr, diagonal):
        vb = v_ref[rows(j), :]
        for mp, (m_scr, l_scr, acc) in enumerate(((m1, l1, acc1), (m2, l2, acc2))):
            s = s_scr[mp]
            if diagonal:
                qc = lax.broadcasted_iota(jnp.int32, s.shape, 0) // CHUNK
                kc = lax.broadcasted_iota(jnp.int32, s.shape, 1) // CHUNK
                s = jnp.where(kc <= qc, s, MASKED)
            update(s, m_scr, l_scr, acc, vb)

    scores(0, sa)

    def pair(i, carry):
        scores(2 * i + 1, sb)
        process(2 * i, sa, False)
        scores(2 * i + 2, sa)
        process(2 * i + 1, sb, False)
        return carry

    lax.fori_loop(0, n_full // 2, pair, 0)

    @pl.when(n_full % 2 == 1)
    def _():
        scores(n_full, sb)
        process(n_full - 1, sa, False)
        process(n_full, sb, True)

    @pl.when(n_full % 2 == 0)
    def _():
        process(n_full, sa, True)

    lam = _diff_lambda(lamv_ref, lam_init)
    d1 = jnp.sum(l1[...], axis=1, keepdims=True)
    d2 = jnp.sum(l2[...], axis=1, keepdims=True)
    o = acc1[...] / d1 - lam * (acc2[...] / d2)
    o_ref[...] = (_rms(o) * gain_ref[...] * (1.0 - lam_init)).astype(_BF)


def _attn_a_prompt(qa, ka, va, lamv, a_gain, *, batch, seq, lam_init):
    t_all, width = qa.shape
    hd = 2 * A_DIM
    tq = A_TILE
    nq = seq // tq
    return pl.pallas_call(
        functools.partial(_attn_a_kernel, lam_init=lam_init),
        grid=(batch, A_HEADS, nq),
        in_specs=[
            _resident(lamv.shape, lambda b, h, i: (0, 0)),
            pl.BlockSpec((tq, hd), lambda b, h, i: (b * nq + i, h)),
            pl.BlockSpec((seq, hd), lambda b, h, i: (b, h)),
            pl.BlockSpec((seq, hd), lambda b, h, i: (b, h)),
            _resident((1, hd), lambda b, h, i: (0, 0)),
        ],
        out_specs=pl.BlockSpec((tq, hd), lambda b, h, i: (b * nq + i, h)),
        out_shape=jax.ShapeDtypeStruct((t_all, width), _BF),
        scratch_shapes=[pltpu.VMEM((2, tq, tq), _F32)] * 2 + [pltpu.VMEM((tq, hd), _F32)] * 6,
        compiler_params=_params("arbitrary", "arbitrary", "arbitrary"),
        name="diff_attn_prompt",
    )(lamv, qa, ka, va, a_gain)


def _head_lanes(shape, head, dim):
    lane = lax.broadcasted_iota(jnp.int32, shape, 1)
    return (lane >= head * dim) & (lane < (head + 1) * dim)


def _attn_b_kernel(q_ref, k0_ref, k1_ref, k2_ref, v0_ref, v1_ref, v2_ref, bias_ref, gain_ref, o_ref):
    qi = pl.program_id(1)
    tq = q_ref.shape[0]
    k_refs = (k0_ref, k1_ref, k2_ref)
    v_refs = (v0_ref, v1_ref, v2_ref)
    pen = (jnp.where(qi >= 2, 0.0, MASKED), jnp.where(qi >= 1, 0.0, MASKED), 0.0)
    outs = []
    for pair in range(B_HEADS // 2):
        cols = slice(pair * LANES, (pair + 1) * LANES)
        qp = q_ref[:, cols]
        zero = jnp.zeros_like(qp)
        halves = []
        for sub in range(2):
            head = 2 * pair + sub
            qh = jnp.where(_head_lanes(qp.shape, sub, B_DIM), qp, zero)
            s = [_dot_nt(qh, k_refs[b][:, cols]) + bias_ref[head, :, b * tq:(b + 1) * tq] + pen[b]
                 for b in range(3)]
            m = jnp.maximum(jnp.maximum(jnp.max(s[0], axis=1, keepdims=True),
                                        jnp.max(s[1], axis=1, keepdims=True)),
                            jnp.max(s[2], axis=1, keepdims=True))
            p = [jnp.exp(sb - m) for sb in s]
            l = (jnp.sum(p[0], axis=1, keepdims=True) + jnp.sum(p[1], axis=1, keepdims=True)
                 + jnp.sum(p[2], axis=1, keepdims=True))
            o = (_dot(p[0].astype(_BF), v_refs[0][:, cols]) + _dot(p[1].astype(_BF), v_refs[1][:, cols])
                 + _dot(p[2].astype(_BF), v_refs[2][:, cols]))
            halves.append(o / l)
        outs.append(jnp.where(_head_lanes(halves[0].shape, 0, B_DIM), halves[0], halves[1]))
    o = jnp.concatenate(outs, axis=1)
    o_ref[...] = (_rms(o) * gain_ref[...]).astype(_BF)


def _band_bias(table, tq):
    nk = B_REACH + tq
    qpos = np.arange(tq)[:, None]
    kpos = np.arange(nk)[None, :] - B_REACH
    qc = qpos // CHUNK
    kc = np.floor_divide(kpos, CHUNK)
    valid = (kc <= qc) & (kc >= qc - LEFT_CHUNKS)
    span = nk + tq - 1
    dist = (span - 1 - np.arange(span)) - (tq - 1)
    line = table[:, np.clip(dist, -REL_CLIP, REL_CLIP) + REL_CLIP].astype(_F32)
    line = jnp.pad(line, ((0, 0), (0, 1)))
    skew = jnp.tile(line, (1, tq))[:, :tq * span].reshape(-1, tq, span)
    bias = skew[:, :, tq - 1:]
    return jnp.where(valid[None], bias, MASKED)


def _attn_b_prompt(qb, kb, vb, rel_table, b_gain, *, batch, seq):
    t_all, width = qb.shape
    tq = B_TILE
    assert B_REACH == 2 * tq
    nq = seq // tq
    bias = _band_bias(rel_table, tq)
    qmap = lambda b, i: (b * nq + i, 0)
    kmap = lambda back: (lambda b, i: (b * nq + jnp.maximum(i - back, 0), 0))
    blk = lambda f: pl.BlockSpec((tq, width), f)
    return pl.pallas_call(
        _attn_b_kernel,
        grid=(batch, nq),
        in_specs=[blk(qmap), blk(kmap(2)), blk(kmap(1)), blk(kmap(0)),
                  blk(kmap(2)), blk(kmap(1)), blk(kmap(0)),
                  _resident(bias.shape, lambda b, i: (0, 0, 0)),
                  _resident((1, width), lambda b, i: (0, 0))],
        out_specs=blk(qmap),
        out_shape=jax.ShapeDtypeStruct((t_all, width), _BF),
        compiler_params=_params("arbitrary", "arbitrary"),
        name="band_attn_prompt",
    )(qb, kb, kb, kb, vb, vb, vb, bias, b_gain)


def _attn_a_sample_kernel(lamv_ref, q_ref, kc_ref, vc_ref, kn_ref, vn_ref, gain_ref, o_ref, *, lam_init):
    q = q_ref[...]
    t = q.shape[0]
    hd = 2 * A_DIM
    past = kc_ref.shape[0] // A_HEADS
    lam = _diff_lambda(lamv_ref, lam_init)
    outs = []
    for h in range(A_HEADS):
        cols = slice(h * hd, (h + 1) * hd)
        head_rows = pl.ds(h, past, stride=A_HEADS)
        q12 = jnp.concatenate(_split_maps(q[:, cols]), axis=0)
        kn = kn_ref[:, cols].astype(_BF)
        vn = vn_ref[:, cols].astype(_BF)
        s_c = _dot_nt(q12, kc_ref[head_rows, :].astype(_BF))
        s_n = _dot_nt(q12, kn)
        m = jnp.maximum(jnp.max(s_c, axis=1, keepdims=True), jnp.max(s_n, axis=1, keepdims=True))
        p_c = jnp.exp(s_c - m)
        p_n = jnp.exp(s_n - m)
        l = jnp.sum(p_c, axis=1, keepdims=True) + jnp.sum(p_n, axis=1, keepdims=True)
        p_c = p_c / l
        p_n = p_n / l
        a_c = p_c[0:t] - lam * p_c[t:2 * t]
        a_n = p_n[0:t] - lam * p_n[t:2 * t]
        o = _dot(a_c.astype(_BF), vc_ref[head_rows, :].astype(_BF)) + _dot(a_n.astype(_BF), vn)
        outs.append(_rms(o) * gain_ref[...] * (1.0 - lam_init))
    o_ref[...] = jnp.concatenate(outs, axis=1).astype(_BF)


def _attn_a_sample(qa, cache_k, cache_v, kn, vn, lamv, a_gain, *, layer, lam_init, t_new):
    t_all, width = qa.shape
    _, nb, cache_rows, hd = cache_k.shape
    tok = pl.BlockSpec((t_new, width), lambda b: (b, 0))
    cache = pl.BlockSpec((None, None, cache_rows, hd), lambda b: (layer, b, 0, 0))
    return pl.pallas_call(
        functools.partial(_attn_a_sample_kernel, lam_init=lam_init),
        grid=(nb,),
        in_specs=[_resident(lamv.shape, lambda b: (0, 0)), tok, cache, cache, tok, tok,
                  _resident((1, 2 * A_DIM), lambda b: (0, 0))],
        out_specs=tok,
        out_shape=jax.ShapeDtypeStruct((t_all, width), _BF),
        compiler_params=_params("arbitrary"),
        name="diff_attn_sample",
    )(lamv, qa, cache_k, cache_v, kn, vn, a_gain)


def _attn_b_sample_kernel(q_ref, kc_ref, vc_ref, kn_ref, vn_ref, bc_ref, bn_ref, gain_ref,
                          o_ref, ko_ref, vo_ref):
    q = q_ref[...]
    t = q.shape[0]
    keep = kc_ref.shape[0] - t
    zero = jnp.zeros_like(q)
    qall = jnp.concatenate(
        [jnp.where(_head_lanes(q.shape, h, B_DIM), q, zero) for h in range(B_HEADS)], axis=0)
    kc = kc_ref[...]
    vc = vc_ref[...]
    kn = kn_ref[...]
    vn = vn_ref[...]
    s_c = _dot_nt(qall, kc.astype(_BF)) + bc_ref[...]
    s_n = _dot_nt(qall, kn.astype(_BF)) + bn_ref[...]
    m = jnp.maximum(jnp.max(s_c, axis=1, keepdims=True), jnp.max(s_n, axis=1, keepdims=True))
    p_c = jnp.exp(s_c - m)
    p_n = jnp.exp(s_n - m)
    l = jnp.sum(p_c, axis=1, keepdims=True) + jnp.sum(p_n, axis=1, keepdims=True)
    o_all = (_dot(p_c.astype(_BF), vc.astype(_BF)) + _dot(p_n.astype(_BF), vn.astype(_BF))) / l
    o = jnp.zeros((t, q.shape[1]), _F32)
    for h in range(B_HEADS):
        o = jnp.where(_head_lanes(o.shape, h, B_DIM), o_all[h * t:(h + 1) * t], o)
    o_ref[...] = (_rms(o) * gain_ref[...]).astype(_BF)
    ko_ref[0:keep, :] = kc[t:, :]
    ko_ref[keep:, :] = kn
    vo_ref[0:keep, :] = vc[t:, :]
    vo_ref[keep:, :] = vn


def _sample_bias(table, t_new, wb):
    qpos = jnp.arange(t_new)[:, None] + wb
    kpos = jnp.arange(wb + t_new)[None, :]
    d = jnp.clip(qpos - kpos, -REL_CLIP, REL_CLIP) + REL_CLIP
    bias = table[:, d].astype(_F32).reshape(B_HEADS * t_new, wb + t_new)
    return bias[:, :wb], bias[:, wb:]


def _attn_b_sample(qb, cache_k, cache_v, kn, vn, rel_table, b_gain, *, layer, t_new):
    t_all, width = qb.shape
    _, nb, wb, _ = cache_k.shape
    bias_c, bias_n = _sample_bias(rel_table, t_new, wb)
    tok = pl.BlockSpec((t_new, width), lambda b: (b, 0))
    cache = pl.BlockSpec((None, None, wb, width), lambda b: (layer, b, 0, 0))
    state = pl.BlockSpec((None, wb, width), lambda b: (b, 0, 0))
    return pl.pallas_call(
        _attn_b_sample_kernel,
        grid=(nb,),
        in_specs=[tok, cache, cache, tok, tok,
                  _resident(bias_c.shape, lambda b: (0, 0)), _resident(bias_n.shape, lambda b: (0, 0)),
                  _resident((1, width), lambda b: (0, 0))],
        out_specs=[tok, state, state],
        out_shape=[jax.ShapeDtypeStruct((t_all, width), _BF),
                   jax.ShapeDtypeStruct((nb, wb, width), _F32),
                   jax.ShapeDtypeStruct((nb, wb, width), _F32)],
        compiler_params=_params("arbitrary"),
        name="band_attn_sample",
    )(qb, cache_k, cache_v, kn, vn, bias_c, bias_n, b_gain)


def _outproj_kernel(x_ref, oa_ref, ob_ref, w_ref, gt_ref, sh_ref, sc_ref, g_ref, x1_ref, ht_ref):
    half = oa_ref.shape[1]
    y = _dot(oa_ref[...], w_ref[0:half, :]) + _dot(ob_ref[...], w_ref[half:, :])
    x1 = x_ref[...] + gt_ref[...] * y
    x1_ref[...] = x1
    h = _rms(x1) * g_ref[...] * (1.0 + sc_ref[...]) + sh_ref[...]
    ht_ref[...] = h.T.astype(_BF)


def _out_projection(x, oa, ob, w_out_bf, gate, shift, scale, gain, *, per_row, seq):
    t_all, d = x.shape
    half = oa.shape[1]
    tm = min(ROW_TILE, t_all)
    mod_spec = _row_mod_specs(per_row, tm, d, max(seq // tm, 1))
    rows = lambda w: pl.BlockSpec((tm, w), lambda t: (t, 0))
    return pl.pallas_call(
        _outproj_kernel,
        grid=(t_all // tm,),
        in_specs=[rows(d), rows(half), rows(half), _resident(w_out_bf.shape, lambda t: (0, 0)),
                  mod_spec, mod_spec, mod_spec, _resident((1, d), lambda t: (0, 0))],
        out_specs=[rows(d), pl.BlockSpec((None, d, tm), lambda t: (t, 0, 0))],
        out_shape=[jax.ShapeDtypeStruct((t_all, d), _F32),
                   jax.ShapeDtypeStruct((t_all // tm, d, tm), _BF)],
        compiler_params=_params("arbitrary"),
        name="out_projection",
    )(x, oa, ob, w_out_bf, gate, shift, scale, gain)


def _top_values(s, k, want_rank=False):
    vals = []
    rank = jnp.full(s.shape, float(k), _F32)
    for i in range(k):
        m = jnp.max(s, axis=0, keepdims=True)
        vals.append(m)
        hit = s == m
        if want_rank:
            rank = jnp.where(hit, float(i), rank)
        s = jnp.where(hit, -jnp.inf, s)
    return (vals, rank) if want_rank else vals


def _stack_rows(vals):
    rows = lax.broadcasted_iota(jnp.int32, (len(vals), LANES), 0)
    out = jnp.zeros((len(vals), LANES), _F32)
    for i, v in enumerate(vals):
        out = jnp.where(rows == i, v, out)
    return out


def _route_kernel(ht_ref, wq_ref, keys_ref, n1_ref, e1_ref, r2_ref, e2_ref, q_scr, s_scr):
    tt = ht_ref.shape[1]
    q_scr[...] = _dot(wq_ref[...], ht_ref[...]).astype(_BF)
    for hm in range(2 * PK_HEADS):
        s_scr[hm] = _dot(keys_ref[hm], q_scr[hm * PK_HALF:(hm + 1) * PK_HALF, :])

    def head(h, carry):
        for c in range(tt // LANES):
            cols = slice(c * LANES, (c + 1) * LANES)
            s1 = s_scr[2 * h, :, cols]
            s2 = s_scr[2 * h + 1, :, cols]
            a = _top_values(s1, PK_TOPK)
            b, rank2 = _top_values(s2, PK_TOPK, want_rank=True)
            amat = _stack_rows(a)
            bmat = _stack_rows(b)
            cand = [a[0] + bmat]
            cand += [a[p] + bmat[0:8] for p in range(1, 8)]
            cand += [amat[8:16] + b[0]]
            top = _top_values(jnp.concatenate(cand, axis=0), PK_TOPK)
            tau = top[PK_TOPK - 1]
            z = jnp.ones_like(tau)
            for k in range(1, PK_TOPK):
                z = z + jnp.exp(top[k] - top[0])
            count1 = jnp.zeros(s1.shape, _F32)
            for p in range(PK_TOPK):
                reach = jnp.sum(jnp.where(a[p] + bmat >= tau, 1.0, 0.0), axis=0, keepdims=True)
                count1 = jnp.where(s1 == a[p], reach, count1)
            n1_ref[h, :, cols] = count1
            e1_ref[h, :, cols] = jnp.exp(s1 - a[0]) * (0.5 / z)
            r2_ref[h, :, cols] = rank2.astype(_BF)
            e2_ref[h, :, cols] = jnp.exp(s2 - b[0]).astype(_BF)
        return carry

    lax.fori_loop(0, PK_HEADS, head, 0)


def _peer_route(ht, wq_t_bf, keys_bf):
    tiles, d, tt = ht.shape
    t_all = tiles * tt
    nq = wq_t_bf.shape[0]
    out = lambda dt: jax.ShapeDtypeStruct((PK_HEADS, N_KEYS, t_all), dt)
    tile = pl.BlockSpec((PK_HEADS, N_KEYS, tt), lambda t: (0, 0, t))
    return pl.pallas_call(
        _route_kernel,
        grid=(tiles,),
        in_specs=[pl.BlockSpec((None, d, tt), lambda t: (t, 0, 0)),
                  _resident(wq_t_bf.shape, lambda t: (0, 0)),
                  _resident(keys_bf.shape, lambda t: (0, 0, 0))],
        out_specs=[tile] * 4,
        out_shape=[out(_F32), out(_F32), out(_BF), out(_BF)],
        scratch_shapes=[pltpu.VMEM((nq, tt), _BF), pltpu.VMEM((2 * PK_HEADS, N_KEYS, tt), _F32)],
        compiler_params=_params("arbitrary"),
        name="peer_route",
    )(ht, wq_t_bf, keys_bf)


def _twice_gelu_tanh(x):
    return x * (1.0 + jnp.tanh(x * (0.7978845608028654 + (0.7978845608028654 * 0.044715) * (x * x))))


def _row_to_tile(row):
    packed = jnp.broadcast_to(row, (16, LANES)).astype(_BF)
    return jnp.tile(packed, (N_KEYS // 16, 1))


def _peer_kernel(ht_ref, u_ref, vt_ref, n1_ref, e1_ref, r2_ref, e2_ref, x_ref, gt_ref, gf_ref,
                 o_ref, at_scr, wt_scr, acc, r2_scr, e2_scr, *, final_norm):
    e = pl.program_id(1)
    tt = ht_ref.shape[1]
    groups = u_ref.shape[0] // N_KEYS
    span = at_scr.shape[2]
    n_chunks = tt // span

    @pl.when(e == 0)
    def _():
        acc[...] = jnp.zeros(acc.shape, _F32)
        r2_scr[...] = r2_ref[...]
        e2_scr[...] = e2_ref[...]

    for c in range(n_chunks):
        at_scr[c] = _dot(u_ref[...], ht_ref[:, c * span:(c + 1) * span])
    for c in range(n_chunks):
        for sub in range(span // LANES):
            cols = slice(c * span + sub * LANES, c * span + (sub + 1) * LANES)
            sub_cols = slice(sub * LANES, (sub + 1) * LANES)
            zero = jnp.zeros((N_KEYS, LANES), _BF)
            for il in range(groups):
                rows = slice(il * N_KEYS, (il + 1) * N_KEYS)
                g = None
                for h in range(PK_HEADS):
                    count = _row_to_tile(n1_ref[h, il:il + 1, cols])
                    sel = jnp.where(r2_scr[h, :, cols] < count, e2_scr[h, :, cols], zero)
                    term = sel * _row_to_tile(e1_ref[h, il:il + 1, cols])
                    g = term if g is None else g + term
                act = _twice_gelu_tanh(at_scr[c, rows, sub_cols]).astype(_BF)
                wt_scr[c, rows, sub_cols] = g * act
        acc[c] += _dot(vt_ref[...], wt_scr[c])

    @pl.when(e == pl.num_programs(1) - 1)
    def _():
        f = jnp.concatenate([acc[c].T for c in range(n_chunks)], axis=0)
        y = x_ref[...] + gt_ref[...] * f
        if final_norm:
            y = _rms(y) * gf_ref[...]
        o_ref[...] = y


def _peer_experts(ht, u_bf, vt_bf, route, x1, gate, g_final, *, per_row, seq, final_norm):
    tiles, d, tt = ht.shape
    steps, _, eb = vt_bf.shape
    t_all = tiles * tt
    span = min(PEER_CHUNK, tt)
    groups = eb // N_KEYS
    assert groups == 8
    n1, e1, r2, e2 = route
    by_step = lambda a: a.reshape(PK_HEADS, N_KEYS // groups, groups, t_all)
    gate_spec = (pl.BlockSpec((tt, d), lambda t, e: (t, 0)) if per_row else
                 pl.BlockSpec((None, 1, d), lambda t, e: (t // max(seq // tt, 1), 0, 0)))
    step_rows = pl.BlockSpec((PK_HEADS, None, groups, tt), lambda t, e: (0, e, 0, t))
    tile = pl.BlockSpec((PK_HEADS, N_KEYS, tt), lambda t, e: (0, 0, t))
    return pl.pallas_call(
        functools.partial(_peer_kernel, final_norm=final_norm),
        grid=(tiles, steps),
        in_specs=[pl.BlockSpec((None, d, tt), lambda t, e: (t, 0, 0)),
                  pl.BlockSpec((eb, d), lambda t, e: (e, 0)),
                  pl.BlockSpec((None, d, eb), lambda t, e: (e, 0, 0)),
                  step_rows, step_rows, tile, tile,
                  pl.BlockSpec((tt, d), lambda t, e: (t, 0)),
                  gate_spec,
                  _resident((1, d), lambda t, e: (0, 0))],
        out_specs=pl.BlockSpec((tt, d), lambda t, e: (t, 0)),
        out_shape=jax.ShapeDtypeStruct((t_all, d), _F32),
        scratch_shapes=[pltpu.VMEM((tt // span, eb, span), _F32), pltpu.VMEM((tt // span, eb, span), _BF),
                        pltpu.VMEM((tt // span, d, span), _F32),
                        pltpu.VMEM((PK_HEADS, N_KEYS, tt), _BF), pltpu.VMEM((PK_HEADS, N_KEYS, tt), _BF)],
        compiler_params=_params("arbitrary", "arbitrary"),
        name="peer_experts",
    )(ht, u_bf, vt_bf, by_step(n1), by_step(e1), r2, e2, x1, gate, g_final)


def _rope_tables(pos):
    half = ROT_DIM // 2
    inv = ROPE_THETA ** (-jnp.arange(0, ROT_DIM, 2, dtype=_F32) / ROT_DIM)
    ang = pos.astype(_F32)[:, None] * inv[None, :]
    cos, sin = jnp.cos(ang), jnp.sin(ang)
    n = pos.shape[0]
    pad = jnp.zeros((n, A_DIM - ROT_DIM), _F32)
    zeros = jnp.zeros((n, half), _F32)
    c = jnp.concatenate([cos, cos, pad + 1.0], axis=1)
    sa = jnp.concatenate([-sin, zeros, pad], axis=1)
    sb = jnp.concatenate([zeros, sin, pad], axis=1)
    return tuple(jnp.tile(t, (1, LANES // A_DIM)) for t in (c, sa, sb))


def _stream_layer(x, mod, lw, rope_tabs, mixer, *, per_row, seq, final_norm, g_final):
    shift1, scale1, gate1, shift2, scale2, gate2 = mod
    proj = _in_projection(x, shift1, scale1, lw["g_attn"], lw["w_in"], rope_tabs, per_row=per_row, seq=seq)
    qa, kaf, vaf, kab, vab, qb, kbf, vbf, kbb, vbb = proj
    oa, ob, extra = mixer(qa, kaf, vaf, kab, vab, qb, kbf, vbf, kbb, vbb)
    x1, ht = _out_projection(x, oa, ob, lw["w_out"], gate1, shift2, scale2, lw["g_ffn"],
                             per_row=per_row, seq=seq)
    route = _peer_route(ht, lw["pk_wq_t"], lw["pk_keys"])
    x2 = _peer_experts(ht, lw["pk_u"], lw["pk_v_t"], route, x1, gate2, g_final,
                       per_row=per_row, seq=seq, final_norm=final_norm)
    return x2, (kaf, vaf, kbf, vbf) + extra


def kernel(x_prompt, x_sample, c_prompt, c_sample, cache_a_k, cache_a_v, cache_b_k, cache_b_v, w_ada, b_ada, g_attn, g_ffn, w_in, lam_q1, lam_k1, lam_q2, lam_k2, a_gain, rel_bias, b_gain, w_out, pk_wq, pk_keys, pk_u, pk_v, g_final):
    batch, seq, d = x_prompt.shape
    nb, t_new, _ = x_sample.shape
    depth = w_ada.shape[0]
    past = cache_a_k.shape[2]
    wb = cache_b_k.shape[2]
    a_width = A_HEADS * 2 * A_DIM
    b_width = B_HEADS * B_DIM

    c_all = jnp.concatenate([c_prompt, c_sample], axis=0)
    pad_rows = (-c_all.shape[0]) % 16
    c_all = jnp.pad(c_all, ((0, pad_rows), (0, 0)))
    mod = _modulation(c_all, w_ada, b_ada)

    tabs_p = _rope_tables(jnp.arange(seq, dtype=jnp.int32))
    pos_s = past + jnp.arange(t_new, dtype=jnp.int32)
    tabs_s = tuple(jnp.tile(t, (nb, 1)) for t in _rope_tables(pos_s))

    cak = cache_a_k.reshape(depth, nb, past * A_HEADS, 2 * A_DIM)
    cav = cache_a_v.reshape(depth, nb, past * A_HEADS, 2 * A_DIM)
    cbk = cache_b_k.reshape(depth, nb, wb, b_width)
    cbv = cache_b_v.reshape(depth, nb, wb, b_width)

    xp = x_prompt.reshape(batch * seq, d)
    xs = x_sample.reshape(nb * t_new, d)
    g_fin = g_final.reshape(1, d)
    st_p, st_s = [], []
    for l in range(depth):
        lam_init = 0.8 - 0.6 * math.exp(-0.3 * l)
        lamv = jnp.stack([lam_q1[l], lam_k1[l], lam_q2[l], lam_k2[l]])
        lw = {
            "g_attn": g_attn[l].reshape(1, d),
            "g_ffn": g_ffn[l].reshape(1, d),
            "w_in": w_in[l].astype(_BF),
            "w_out": w_out[l].astype(_BF),
            "pk_wq_t": pk_wq[l].T.astype(_BF),
            "pk_keys": pk_keys[l].reshape(2 * PK_HEADS, N_KEYS, PK_HALF).astype(_BF),
            "pk_u": pk_u[l].astype(_BF),
            "pk_v_t": pk_v[l].astype(_BF).reshape(-1, PEER_EXPERTS, d).transpose(0, 2, 1),
        }
        gain_a = a_gain[l].reshape(1, 2 * A_DIM)
        gain_b = b_gain[l].reshape(1, b_width)
        mod_p = tuple(m.reshape(batch, 1, d) for m in jnp.split(mod[l, :batch], 6, axis=-1))
        mod_s = tuple(jnp.repeat(m, t_new, axis=0) for m in jnp.split(mod[l, batch:batch + nb], 6, axis=-1))
        last = l == depth - 1

        def mix_p(qa, kaf, vaf, kab, vab, qb, kbf, vbf, kbb, vbb):
            oa = _attn_a_prompt(qa, kab, vab, lamv, gain_a, batch=batch, seq=seq, lam_init=lam_init)
            ob = _attn_b_prompt(qb, kbb, vbb, rel_bias[l], gain_b, batch=batch, seq=seq)
            return oa, ob, ()

        def mix_s(qa, kaf, vaf, kab, vab, qb, kbf, vbf, kbb, vbb):
            oa = _attn_a_sample(qa, cak, cav, kaf, vaf, lamv, gain_a, layer=l, lam_init=lam_init, t_new=t_new)
            ob, nbk, nbv = _attn_b_sample(qb, cbk, cbv, kbf, vbf, rel_bias[l], gain_b, layer=l, t_new=t_new)
            return oa, ob, (nbk, nbv)

        xp, sp = _stream_layer(xp, mod_p, lw, tabs_p, mix_p, per_row=False, seq=seq,
                               final_norm=last, g_final=g_fin)
        xs, ss = _stream_layer(xs, mod_s, lw, tabs_s, mix_s, per_row=True, seq=t_new,
                               final_norm=last, g_final=g_fin)
        st_p.append(sp)
        st_s.append(ss)

    keep = min(B_REACH, seq)
    a_shape_p = (batch, seq, A_HEADS, 2 * A_DIM)
    b_tail = lambda s: s.reshape(batch, seq, B_HEADS, B_DIM)[:, seq - keep:]
    a_shape_s = (nb, t_new, A_HEADS, 2 * A_DIM)
    b_shape_s = (nb, wb, B_HEADS, B_DIM)
    return (
        xp.reshape(batch, seq, d),
        xs.reshape(nb, t_new, d),
        jnp.stack([s[0].reshape(a_shape_p) for s in st_p]),
        jnp.stack([s[1].reshape(a_shape_p) for s in st_p]),
        jnp.stack([b_tail(s[2]) for s in st_p]),
        jnp.stack([b_tail(s[3]) for s in st_p]),
        jnp.stack([s[0].reshape(a_shape_s) for s in st_s]),
        jnp.stack([s[1].reshape(a_shape_s) for s in st_s]),
        jnp.stack([s[4].reshape(b_shape_s) for s in st_s]),
        jnp.stack([s[5].reshape(b_shape_s) for s in st_s]),
    )
```

```python
import functools
import math

import jax
import jax.numpy as jnp
import numpy as np
from jax import lax
from jax.experimental import pallas as pl
from jax.experimental.pallas import tpu as pltpu

_F32 = jnp.float32
_BF = jnp.bfloat16

CHUNK = 64
A_HEADS = 4
A_DIM = 64
ROT_DIM = A_DIM // 4
ROPE_THETA = 500000.0
B_HEADS = 8
B_DIM = 64
LEFT_CHUNKS = 8
B_REACH = LEFT_CHUNKS * CHUNK
REL_CLIP = 128
PK_HEADS = 8
N_KEYS = 128
PK_TOPK = 16
PK_HALF = 128
EPS = 1e-6

LANES = 128
MASKED = -1e30
VMEM_LIMIT = 56 * 1024 * 1024

A_TILE = 512
B_TILE = 256
PEER_TOKENS = 1024
ROUTE_TOKENS = 512
PEER_EXPERTS = 1024
PEER_CHUNK = 256
ROW_TILE = 512


def _params(*sem):
    return pltpu.CompilerParams(dimension_semantics=sem, vmem_limit_bytes=VMEM_LIMIT)


def _resident(shape, index_map):
    return pl.BlockSpec(shape, index_map, pipeline_mode=pl.Buffered(1))


def _rms(x):
    return x * lax.rsqrt(jnp.mean(x * x, axis=-1, keepdims=True) + EPS)


def _dot(a, b):
    return jnp.dot(a, b, preferred_element_type=_F32)


def _dot_nt(a, b):
    return lax.dot_general(a, b, (((1,), (1,)), ((), ())), preferred_element_type=_F32)


def _mod_kernel(c_ref, w_ref, b_ref, o_ref):
    c = c_ref[...]
    s = c * (1.0 / (1.0 + jnp.exp(-c)))
    o_ref[...] = _dot(s.astype(_BF), w_ref[...].astype(_BF)) + b_ref[...]


def _modulation(c_all, w_ada, b_ada):
    depth, d, n = w_ada.shape
    rows = c_all.shape[0]
    tn = n // 4
    return pl.pallas_call(
        _mod_kernel,
        grid=(depth, n // tn),
        in_specs=[
            pl.BlockSpec((rows, d), lambda l, j: (0, 0)),
            pl.BlockSpec((None, d, tn), lambda l, j: (l, 0, j)),
            pl.BlockSpec((None, 1, tn), lambda l, j: (l, 0, j)),
        ],
        out_specs=pl.BlockSpec((None, rows, tn), lambda l, j: (l, 0, j)),
        out_shape=jax.ShapeDtypeStruct((depth, rows, n), _F32),
        compiler_params=_params("arbitrary", "arbitrary"),
        name="adaln_mod",
    )(c_all, w_ada, b_ada.reshape(depth, 1, n))


def _inproj_kernel(x_ref, sh_ref, sc_ref, g_ref, w_ref, cos_ref, sa_ref, sb_ref,
                   qa_ref, kaf_ref, vaf_ref, kab_ref, vab_ref,
                   qb_ref, kbf_ref, vbf_ref, kbb_ref, vbb_ref, *, width):
    h = _rms(x_ref[...]) * g_ref[...] * (1.0 + sc_ref[...]) + sh_ref[...]
    hb = h.astype(_BF)
    reps = width // LANES
    cos = jnp.tile(cos_ref[...], (1, reps))
    sa = jnp.tile(sa_ref[...], (1, reps))
    sb = jnp.tile(sb_ref[...], (1, reps))

    def rope(z):
        return z * cos + pltpu.roll(z, width - ROT_DIM // 2, 1) * sa + pltpu.roll(z, ROT_DIM // 2, 1) * sb

    def proj(i):
        return _dot(hb, w_ref[:, i * width:(i + 1) * width])

    qa_ref[...] = (rope(proj(0)) * (A_DIM ** -0.5)).astype(_BF)
    ka = rope(proj(1))
    kaf_ref[...] = ka
    kab_ref[...] = ka.astype(_BF)
    va = proj(2)
    vaf_ref[...] = va
    vab_ref[...] = va.astype(_BF)
    qb_ref[...] = (proj(3) * (B_DIM ** -0.5)).astype(_BF)
    kb = proj(4)
    kbf_ref[...] = kb
    kbb_ref[...] = kb.astype(_BF)
    vb = proj(5)
    vbf_ref[...] = vb
    vbb_ref[...] = vb.astype(_BF)


def _row_mod_specs(per_row, tm, d, tiles_per_batch):
    if per_row:
        return pl.BlockSpec((tm, d), lambda t: (t, 0))
    return pl.BlockSpec((None, 1, d), lambda t: (t // tiles_per_batch, 0, 0))


def _in_projection(x, shift, scale, gain, w_in_bf, rope_tabs, *, per_row, seq):
    t_all, d = x.shape
    width = w_in_bf.shape[1] // 6
    tm = min(ROW_TILE, t_all)
    tiles_per_batch = max(seq // tm, 1)
    tab_tiles = rope_tabs[0].shape[0] // tm
    mod_spec = _row_mod_specs(per_row, tm, d, tiles_per_batch)
    tab_spec = pl.BlockSpec((tm, LANES), lambda t: (t % tab_tiles, 0))
    row = lambda dt: jax.ShapeDtypeStruct((t_all, width), dt)
    out_spec = pl.BlockSpec((tm, width), lambda t: (t, 0))
    dts = [_BF, _F32, _F32, _BF, _BF, _BF, _F32, _F32, _BF, _BF]
    return pl.pallas_call(
        functools.partial(_inproj_kernel, width=width),
        grid=(t_all // tm,),
        in_specs=[
            pl.BlockSpec((tm, d), lambda t: (t, 0)),
            mod_spec, mod_spec,
            _resident((1, d), lambda t: (0, 0)),
            _resident(w_in_bf.shape, lambda t: (0, 0)),
            tab_spec, tab_spec, tab_spec,
        ],
        out_specs=[out_spec] * 10,
        out_shape=[row(dt) for dt in dts],
        compiler_params=_params("arbitrary"),
        name="in_projection",
    )(x, shift, scale, gain, w_in_bf, *rope_tabs)


def _diff_lambda(lamv_ref, lam_init):
    v = lamv_ref[...]
    d1 = jnp.sum(v[0:1] * v[1:2], axis=1, keepdims=True)
    d2 = jnp.sum(v[2:3] * v[3:4], axis=1, keepdims=True)
    return jnp.exp(d1) - jnp.exp(d2) + lam_init


def _split_maps(q):
    lane = lax.broadcasted_iota(jnp.int32, q.shape, 1)
    zero = jnp.zeros_like(q)
    return jnp.where(lane < A_DIM, q, zero), jnp.where(lane >= A_DIM, q, zero)


def _attn_a_kernel(lamv_ref, q_ref, k_ref, v_ref, gain_ref, o_ref,
                   sa, sb, m1, l1, acc1, m2, l2, acc2, *, lam_init):
    n_full = pl.program_id(2)
    tq = q_ref.shape[0]
    q1, q2 = _split_maps(q_ref[...])
    for m_scr, l_scr, acc in ((m1, l1, acc1), (m2, l2, acc2)):
        m_scr[...] = jnp.full(m_scr.shape, MASKED, _F32)
        l_scr[...] = jnp.zeros(l_scr.shape, _F32)
        acc[...] = jnp.zeros(acc.shape, _F32)

    def rows(j):
        return pl.ds(pl.multiple_of(j * tq, tq), tq)

    def scores(j, s_scr):
        kb = k_ref[rows(j), :]
        s_scr[0] = _dot_nt(q1, kb)
        s_scr[1] = _dot_nt(q2, kb)

    def update(s, m_scr, l_scr, acc, vb):
        m_prev = m_scr[...]
        m_next = jnp.maximum(m_prev, jnp.max(s, axis=1, keepdims=True))
        p = jnp.exp(s - jnp.tile(m_next, (1, tq // LANES)))
        alpha = jnp.exp(m_prev - m_next)
        part = p[:, 0:LANES]
        for c in range(1, tq // LANES):
            part = part + p[:, c * LANES:(c + 1) * LANES]
        l_scr[...] = alpha * l_scr[...] + part
        acc[...] = acc[...] * alpha + _dot(p.astype(_BF), vb)
        m_scr[...] = m_next

    def process(j, s_scr, diagonal):
        vb = v_ref[rows(j), :]
        for mp, (m_scr, l_scr, acc) in enumerate(((m1, l1, acc1), (m2, l2, acc2))):
            s = s_scr[mp]
            if diagonal:
                qc = lax.broadcasted_iota(jnp.int32, s.shape, 0) // CHUNK
                kc = lax.broadcasted_iota(jnp.int32, s.shape, 1) // CHUNK
                s = jnp.where(kc <= qc, s, MASKED)
            update(s, m_scr, l_scr, acc, vb)

    scores(0, sa)

    def pair(i, carry):
        scores(2 * i + 1, sb)
        process(2 * i, sa, False)
        scores(2 * i + 2, sa)
        process(2 * i + 1, sb, False)
        return carry

    lax.fori_loop(0, n_full // 2, pair, 0)

    @pl.when(n_full % 2 == 1)
    def _():
        scores(n_full, sb)
        process(n_full - 1, sa, False)
        process(n_full, sb, True)

    @pl.when(n_full % 2 == 0)
    def _():
        process(n_full, sa, True)

    lam = _diff_lambda(lamv_ref, lam_init)
    d1 = jnp.sum(l1[...], axis=1, keepdims=True)
    d2 = jnp.sum(l2[...], axis=1, keepdims=True)
    o = acc1[...] / d1 - lam * (acc2[...] / d2)
    o_ref[...] = (_rms(o) * gain_ref[...] * (1.0 - lam_init)).astype(_BF)


def _attn_a_prompt(qa, ka, va, lamv, a_gain, *, batch, seq, lam_init):
    t_all, width = qa.shape
    hd = 2 * A_DIM
    tq = A_TILE
    nq = seq // tq
    return pl.pallas_call(
        functools.partial(_attn_a_kernel, lam_init=lam_init),
        grid=(batch, A_HEADS, nq),
        in_specs=[
            _resident(lamv.shape, lambda b, h, i: (0, 0)),
            pl.BlockSpec((tq, hd), lambda b, h, i: (b * nq + i, h)),
            pl.BlockSpec((seq, hd), lambda b, h, i: (b, h)),
            pl.BlockSpec((seq, hd), lambda b, h, i: (b, h)),
            _resident((1, hd), lambda b, h, i: (0, 0)),
        ],
        out_specs=pl.BlockSpec((tq, hd), lambda b, h, i: (b * nq + i, h)),
        out_shape=jax.ShapeDtypeStruct((t_all, width), _BF),
        scratch_shapes=[pltpu.VMEM((2, tq, tq), _F32)] * 2 + [pltpu.VMEM((tq, hd), _F32)] * 6,
        compiler_params=_params("arbitrary", "arbitrary", "arbitrary"),
        name="diff_attn_prompt",
    )(lamv, qa, ka, va, a_gain)


def _head_lanes(shape, head, dim):
    lane = lax.broadcasted_iota(jnp.int32, shape, 1)
    return (lane >= head * dim) & (lane < (head + 1) * dim)


def _attn_b_kernel(q_ref, k0_ref, k1_ref, k2_ref, v0_ref, v1_ref, v2_ref, bias_ref, gain_ref, o_ref):
    qi = pl.program_id(1)
    tq = q_ref.shape[0]
    k_refs = (k0_ref, k1_ref, k2_ref)
    v_refs = (v0_ref, v1_ref, v2_ref)
    pen = (jnp.where(qi >= 2, 0.0, MASKED), jnp.where(qi >= 1, 0.0, MASKED), 0.0)
    outs = []
    for pair in range(B_HEADS // 2):
        cols = slice(pair * LANES, (pair + 1) * LANES)
        qp = q_ref[:, cols]
        zero = jnp.zeros_like(qp)
        halves = []
        for sub in range(2):
            head = 2 * pair + sub
            qh = jnp.where(_head_lanes(qp.shape, sub, B_DIM), qp, zero)
            s = [_dot_nt(qh, k_refs[b][:, cols]) + bias_ref[head, :, b * tq:(b + 1) * tq] + pen[b]
                 for b in range(3)]
            m = jnp.maximum(jnp.maximum(jnp.max(s[0], axis=1, keepdims=True),
                                        jnp.max(s[1], axis=1, keepdims=True)),
                            jnp.max(s[2], axis=1, keepdims=True))
            p = [jnp.exp(sb - m) for sb in s]
            l = (jnp.sum(p[0], axis=1, keepdims=True) + jnp.sum(p[1], axis=1, keepdims=True)
                 + jnp.sum(p[2], axis=1, keepdims=True))
            o = (_dot(p[0].astype(_BF), v_refs[0][:, cols]) + _dot(p[1].astype(_BF), v_refs[1][:, cols])
                 + _dot(p[2].astype(_BF), v_refs[2][:, cols]))
            halves.append(o / l)
        outs.append(jnp.where(_head_lanes(halves[0].shape, 0, B_DIM), halves[0], halves[1]))
    o = jnp.concatenate(outs, axis=1)
    o_ref[...] = (_rms(o) * gain_ref[...]).astype(_BF)


def _band_bias(table, tq):
    nk = B_REACH + tq
    qpos = np.arange(tq)[:, None]
    kpos = np.arange(nk)[None, :] - B_REACH
    qc = qpos // CHUNK
    kc = np.floor_divide(kpos, CHUNK)
    valid = (kc <= qc) & (kc >= qc - LEFT_CHUNKS)
    span = nk + tq - 1
    dist = (span - 1 - np.arange(span)) - (tq - 1)
    line = table[:, np.clip(dist, -REL_CLIP, REL_CLIP) + REL_CLIP].astype(_F32)
    line = jnp.pad(line, ((0, 0), (0, 1)))
    skew = jnp.tile(line, (1, tq))[:, :tq * span].reshape(-1, tq, span)
    bias = skew[:, :, tq - 1:]
    return jnp.where(valid[None], bias, MASKED)


def _attn_b_prompt(qb, kb, vb, rel_table, b_gain, *, batch, seq):
    t_all, width = qb.shape
    tq = B_TILE
    assert B_REACH == 2 * tq
    nq = seq // tq
    bias = _band_bias(rel_table, tq)
    qmap = lambda b, i: (b * nq + i, 0)
    kmap = lambda back: (lambda b, i: (b * nq + jnp.maximum(i - back, 0), 0))
    blk = lambda f: pl.BlockSpec((tq, width), f)
    return pl.pallas_call(
        _attn_b_kernel,
        grid=(batch, nq),
        in_specs=[blk(qmap), blk(kmap(2)), blk(kmap(1)), blk(kmap(0)),
                  blk(kmap(2)), blk(kmap(1)), blk(kmap(0)),
                  _resident(bias.shape, lambda b, i: (0, 0, 0)),
                  _resident((1, width), lambda b, i: (0, 0))],
        out_specs=blk(qmap),
        out_shape=jax.ShapeDtypeStruct((t_all, width), _BF),
        compiler_params=_params("arbitrary", "arbitrary"),
        name="band_attn_prompt",
    )(qb, kb, kb, kb, vb, vb, vb, bias, b_gain)


def _attn_a_sample_kernel(lamv_ref, q_ref, kc_ref, vc_ref, kn_ref, vn_ref, gain_ref, o_ref, *, lam_init):
    q = q_ref[...]
    t = q.shape[0]
    hd = 2 * A_DIM
    past = kc_ref.shape[0] // A_HEADS
    lam = _diff_lambda(lamv_ref, lam_init)
    outs = []
    for h in range(A_HEADS):
        cols = slice(h * hd, (h + 1) * hd)
        head_rows = pl.ds(h, past, stride=A_HEADS)
        q12 = jnp.concatenate(_split_maps(q[:, cols]), axis=0)
        kn = kn_ref[:, cols].astype(_BF)
        vn = vn_ref[:, cols].astype(_BF)
        s_c = _dot_nt(q12, kc_ref[head_rows, :].astype(_BF))
        s_n = _dot_nt(q12, kn)
        m = jnp.maximum(jnp.max(s_c, axis=1, keepdims=True), jnp.max(s_n, axis=1, keepdims=True))
        p_c = jnp.exp(s_c - m)
        p_n = jnp.exp(s_n - m)
        l = jnp.sum(p_c, axis=1, keepdims=True) + jnp.sum(p_n, axis=1, keepdims=True)
        p_c = p_c / l
        p_n = p_n / l
        a_c = p_c[0:t] - lam * p_c[t:2 * t]
        a_n = p_n[0:t] - lam * p_n[t:2 * t]
        o = _dot(a_c.astype(_BF), vc_ref[head_rows, :].astype(_BF)) + _dot(a_n.astype(_BF), vn)
        outs.append(_rms(o) * gain_ref[...] * (1.0 - lam_init))
    o_ref[...] = jnp.concatenate(outs, axis=1).astype(_BF)


def _attn_a_sample(qa, cache_k, cache_v, kn, vn, lamv, a_gain, *, layer, lam_init, t_new):
    t_all, width = qa.shape
    _, nb, cache_rows, hd = cache_k.shape
    tok = pl.BlockSpec((t_new, width), lambda b: (b, 0))
    cache = pl.BlockSpec((None, None, cache_rows, hd), lambda b: (layer, b, 0, 0))
    return pl.pallas_call(
        functools.partial(_attn_a_sample_kernel, lam_init=lam_init),
        grid=(nb,),
        in_specs=[_resident(lamv.shape, lambda b: (0, 0)), tok, cache, cache, tok, tok,
                  _resident((1, 2 * A_DIM), lambda b: (0, 0))],
        out_specs=tok,
        out_shape=jax.ShapeDtypeStruct((t_all, width), _BF),
        compiler_params=_params("arbitrary"),
        name="diff_attn_sample",
    )(lamv, qa, cache_k, cache_v, kn, vn, a_gain)


def _attn_b_sample_kernel(q_ref, kc_ref, vc_ref, kn_ref, vn_ref, bc_ref, bn_ref, gain_ref,
                          o_ref, ko_ref, vo_ref):
    q = q_ref[...]
    t = q.shape[0]
    keep = kc_ref.shape[0] - t
    zero = jnp.zeros_like(q)
    qall = jnp.concatenate(
        [jnp.where(_head_lanes(q.shape, h, B_DIM), q, zero) for h in range(B_HEADS)], axis=0)
    kc = kc_ref[...]
    vc = vc_ref[...]
    kn = kn_ref[...]
    vn = vn_ref[...]
    s_c = _dot_nt(qall, kc.astype(_BF)) + bc_ref[...]
    s_n = _dot_nt(qall, kn.astype(_BF)) + bn_ref[...]
    m = jnp.maximum(jnp.max(s_c, axis=1, keepdims=True), jnp.max(s_n, axis=1, keepdims=True))
    p_c = jnp.exp(s_c - m)
    p_n = jnp.exp(s_n - m)
    l = jnp.sum(p_c, axis=1, keepdims=True) + jnp.sum(p_n, axis=1, keepdims=True)
    o_all = (_dot(p_c.astype(_BF), vc.astype(_BF)) + _dot(p_n.astype(_BF), vn.astype(_BF))) / l
    o = jnp.zeros((t, q.shape[1]), _F32)
    for h in range(B_HEADS):
        o = jnp.where(_head_lanes(o.shape, h, B_DIM), o_all[h * t:(h + 1) * t], o)
    o_ref[...] = (_rms(o) * gain_ref[...]).astype(_BF)
    ko_ref[0:keep, :] = kc[t:, :]
    ko_ref[keep:, :] = kn
    vo_ref[0:keep, :] = vc[t:, :]
    vo_ref[keep:, :] = vn


def _sample_bias(table, t_new, wb):
    qpos = jnp.arange(t_new)[:, None] + wb
    kpos = jnp.arange(wb + t_new)[None, :]
    d = jnp.clip(qpos - kpos, -REL_CLIP, REL_CLIP) + REL_CLIP
    bias = table[:, d].astype(_F32).reshape(B_HEADS * t_new, wb + t_new)
    return bias[:, :wb], bias[:, wb:]


def _attn_b_sample(qb, cache_k, cache_v, kn, vn, rel_table, b_gain, *, layer, t_new):
    t_all, width = qb.shape
    _, nb, wb, _ = cache_k.shape
    bias_c, bias_n = _sample_bias(rel_table, t_new, wb)
    tok = pl.BlockSpec((t_new, width), lambda b: (b, 0))
    cache = pl.BlockSpec((None, None, wb, width), lambda b: (layer, b, 0, 0))
    state = pl.BlockSpec((None, wb, width), lambda b: (b, 0, 0))
    return pl.pallas_call(
        _attn_b_sample_kernel,
        grid=(nb,),
        in_specs=[tok, cache, cache, tok, tok,
                  _resident(bias_c.shape, lambda b: (0, 0)), _resident(bias_n.shape, lambda b: (0, 0)),
                  _resident((1, width), lambda b: (0, 0))],
        out_specs=[tok, state, state],
        out_shape=[jax.ShapeDtypeStruct((t_all, width), _BF),
                   jax.ShapeDtypeStruct((nb, wb, width), _F32),
                   jax.ShapeDtypeStruct((nb, wb, width), _F32)],
        compiler_params=_params("arbitrary"),
        name="band_attn_sample",
    )(qb, cache_k, cache_v, kn, vn, bias_c, bias_n, b_gain)


def _outproj_kernel(x_ref, oa_ref, ob_ref, w_ref, gt_ref, sh_ref, sc_ref, g_ref, x1_ref, ht_ref):
    half = oa_ref.shape[1]
    y = _dot(oa_ref[...], w_ref[0:half, :]) + _dot(ob_ref[...], w_ref[half:, :])
    x1 = x_ref[...] + gt_ref[...] * y
    x1_ref[...] = x1
    h = _rms(x1) * g_ref[...] * (1.0 + sc_ref[...]) + sh_ref[...]
    ht_ref[...] = h.T.astype(_BF)


def _out_projection(x, oa, ob, w_out_bf, gate, shift, scale, gain, *, per_row, seq):
    t_all, d = x.shape
    half = oa.shape[1]
    tm = min(PEER_TOKENS, t_all)
    mod_spec = _row_mod_specs(per_row, tm, d, max(seq // tm, 1))
    rows = lambda w: pl.BlockSpec((tm, w), lambda t: (t, 0))
    return pl.pallas_call(
        _outproj_kernel,
        grid=(t_all // tm,),
        in_specs=[rows(d), rows(half), rows(half), _resident(w_out_bf.shape, lambda t: (0, 0)),
                  mod_spec, mod_spec, mod_spec, _resident((1, d), lambda t: (0, 0))],
        out_specs=[rows(d), pl.BlockSpec((None, d, tm), lambda t: (t, 0, 0))],
        out_shape=[jax.ShapeDtypeStruct((t_all, d), _F32),
                   jax.ShapeDtypeStruct((t_all // tm, d, tm), _BF)],
        compiler_params=_params("arbitrary"),
        name="out_projection",
    )(x, oa, ob, w_out_bf, gate, shift, scale, gain)


def _top_values(s, k, want_rank=False):
    vals = []
    rank = jnp.full(s.shape, float(k), _F32)
    for i in range(k):
        m = jnp.max(s, axis=0, keepdims=True)
        vals.append(m)
        hit = s == m
        if want_rank:
            rank = jnp.where(hit, float(i), rank)
        s = jnp.where(hit, -jnp.inf, s)
    return (vals, rank) if want_rank else vals


def _stack_rows(vals):
    rows = lax.broadcasted_iota(jnp.int32, (len(vals), LANES), 0)
    out = jnp.zeros((len(vals), LANES), _F32)
    for i, v in enumerate(vals):
        out = jnp.where(rows == i, v, out)
    return out


def _route_kernel(ht_ref, wq_ref, keys_ref, n1_ref, e1_ref, r2_ref, e2_ref, q_scr, s_scr):
    tt = ht_ref.shape[1]
    q_scr[...] = _dot(wq_ref[...], ht_ref[...]).astype(_BF)
    for hm in range(2 * PK_HEADS):
        s_scr[hm] = _dot(keys_ref[hm], q_scr[hm * PK_HALF:(hm + 1) * PK_HALF, :])

    def head(h, carry):
        for c in range(tt // LANES):
            cols = slice(c * LANES, (c + 1) * LANES)
            s1 = s_scr[2 * h, :, cols]
            s2 = s_scr[2 * h + 1, :, cols]
            a = _top_values(s1, PK_TOPK)
            b, rank2 = _top_values(s2, PK_TOPK, want_rank=True)
            amat = _stack_rows(a)
            bmat = _stack_rows(b)
            cand = [a[0] + bmat]
            cand += [a[p] + bmat[0:8] for p in range(1, 8)]
            cand += [amat[8:16] + b[0]]
            top = _top_values(jnp.concatenate(cand, axis=0), PK_TOPK)
            tau = top[PK_TOPK - 1]
            z = jnp.ones_like(tau)
            for k in range(1, PK_TOPK):
                z = z + jnp.exp(top[k] - top[0])
            count1 = jnp.zeros(s1.shape, _F32)
            for p in range(PK_TOPK):
                reach = jnp.sum(jnp.where(a[p] + bmat >= tau, 1.0, 0.0), axis=0, keepdims=True)
                count1 = jnp.where(s1 == a[p], reach, count1)
            n1_ref[h, :, cols] = count1
            e1_ref[h, :, cols] = jnp.exp(s1 - a[0]) * (0.5 / z)
            r2_ref[h, :, cols] = rank2.astype(_BF)
            e2_ref[h, :, cols] = jnp.exp(s2 - b[0]).astype(_BF)
        return carry

    lax.fori_loop(0, PK_HEADS, head, 0)


def _peer_route(ht, wq_t_bf, keys_bf):
    slabs, d, slab = ht.shape
    t_all = slabs * slab
    tt = min(ROUTE_TOKENS, slab)
    per_slab = slab // tt
    nq = wq_t_bf.shape[0]
    out = lambda dt: jax.ShapeDtypeStruct((PK_HEADS, N_KEYS, t_all), dt)
    tile = pl.BlockSpec((PK_HEADS, N_KEYS, tt), lambda t: (0, 0, t))
    return pl.pallas_call(
        _route_kernel,
        grid=(t_all // tt,),
        in_specs=[pl.BlockSpec((None, d, tt), lambda t: (t // per_slab, 0, t % per_slab)),
                  _resident(wq_t_bf.shape, lambda t: (0, 0)),
                  _resident(keys_bf.shape, lambda t: (0, 0, 0))],
        out_specs=[tile] * 4,
        out_shape=[out(_F32), out(_F32), out(_BF), out(_BF)],
        scratch_shapes=[pltpu.VMEM((nq, tt), _BF), pltpu.VMEM((2 * PK_HEADS, N_KEYS, tt), _F32)],
        compiler_params=_params("arbitrary"),
        name="peer_route",
    )(ht, wq_t_bf, keys_bf)


def _twice_gelu_tanh(x):
    return x * (1.0 + jnp.tanh(x * (0.7978845608028654 + (0.7978845608028654 * 0.044715) * (x * x))))


def _row_to_tile(row):
    packed = jnp.broadcast_to(row, (16, LANES)).astype(_BF)
    return jnp.tile(packed, (N_KEYS // 16, 1))


def _peer_kernel(ht_ref, u_ref, vt_ref, n1_ref, e1_ref, r2_ref, e2_ref, x_ref, gt_ref, gf_ref,
                 o_ref, at_scr, wt_scr, acc, r2_scr, e2_scr, *, final_norm):
    e = pl.program_id(1)
    tt = ht_ref.shape[1]
    groups = u_ref.shape[0] // N_KEYS
    span = at_scr.shape[2]
    n_chunks = tt // span

    @pl.when(e == 0)
    def _():
        acc[...] = jnp.zeros(acc.shape, _F32)
        r2_scr[...] = r2_ref[...]
        e2_scr[...] = e2_ref[...]

    for c in range(n_chunks):
        at_scr[c] = _dot(u_ref[...], ht_ref[:, c * span:(c + 1) * span])
    for c in range(n_chunks):
        for sub in range(span // LANES):
            cols = slice(c * span + sub * LANES, c * span + (sub + 1) * LANES)
            sub_cols = slice(sub * LANES, (sub + 1) * LANES)
            zero = jnp.zeros((N_KEYS, LANES), _BF)
            for il in range(groups):
                rows = slice(il * N_KEYS, (il + 1) * N_KEYS)
                g = None
                for h in range(PK_HEADS):
                    count = _row_to_tile(n1_ref[h, il:il + 1, cols])
                    sel = jnp.where(r2_scr[h, :, cols] < count, e2_scr[h, :, cols], zero)
                    term = sel * _row_to_tile(e1_ref[h, il:il + 1, cols])
                    g = term if g is None else g + term
                act = _twice_gelu_tanh(at_scr[c, rows, sub_cols]).astype(_BF)
                wt_scr[c, rows, sub_cols] = g * act
        acc[c] += _dot(vt_ref[...], wt_scr[c])

    @pl.when(e == pl.num_programs(1) - 1)
    def _():
        f = jnp.concatenate([acc[c].T for c in range(n_chunks)], axis=0)
        y = x_ref[...] + gt_ref[...] * f
        if final_norm:
            y = _rms(y) * gf_ref[...]
        o_ref[...] = y


def _peer_experts(ht, u_bf, vt_bf, route, x1, gate, g_final, *, per_row, seq, final_norm):
    tiles, d, tt = ht.shape
    steps, _, eb = vt_bf.shape
    t_all = tiles * tt
    span = min(PEER_CHUNK, tt)
    groups = eb // N_KEYS
    assert groups == 8
    n1, e1, r2, e2 = route
    by_step = lambda a: a.reshape(PK_HEADS, N_KEYS // groups, groups, t_all)
    gate_spec = (pl.BlockSpec((tt, d), lambda t, e: (t, 0)) if per_row else
                 pl.BlockSpec((None, 1, d), lambda t, e: (t // max(seq // tt, 1), 0, 0)))
    step_rows = pl.BlockSpec((PK_HEADS, None, groups, tt), lambda t, e: (0, e, 0, t))
    tile = _resident((PK_HEADS, N_KEYS, tt), lambda t, e: (0, 0, t))
    return pl.pallas_call(
        functools.partial(_peer_kernel, final_norm=final_norm),
        grid=(tiles, steps),
        in_specs=[pl.BlockSpec((None, d, tt), lambda t, e: (t, 0, 0)),
                  pl.BlockSpec((eb, d), lambda t, e: (e, 0)),
                  pl.BlockSpec((None, d, eb), lambda t, e: (e, 0, 0)),
                  step_rows, step_rows, tile, tile,
                  _resident((tt, d), lambda t, e: (t, 0)),
                  gate_spec,
                  _resident((1, d), lambda t, e: (0, 0))],
        out_specs=pl.BlockSpec((tt, d), lambda t, e: (t, 0)),
        out_shape=jax.ShapeDtypeStruct((t_all, d), _F32),
        scratch_shapes=[pltpu.VMEM((tt // span, eb, span), _F32), pltpu.VMEM((tt // span, eb, span), _BF),
                        pltpu.VMEM((tt // span, d, span), _F32),
                        pltpu.VMEM((PK_HEADS, N_KEYS, tt), _BF), pltpu.VMEM((PK_HEADS, N_KEYS, tt), _BF)],
        compiler_params=_params("arbitrary", "arbitrary"),
        name="peer_experts",
    )(ht, u_bf, vt_bf, by_step(n1), by_step(e1), r2, e2, x1, gate, g_final)


def _rope_tables(pos):
    half = ROT_DIM // 2
    inv = ROPE_THETA ** (-jnp.arange(0, ROT_DIM, 2, dtype=_F32) / ROT_DIM)
    ang = pos.astype(_F32)[:, None] * inv[None, :]
    cos, sin = jnp.cos(ang), jnp.sin(ang)
    n = pos.shape[0]
    pad = jnp.zeros((n, A_DIM - ROT_DIM), _F32)
    zeros = jnp.zeros((n, half), _F32)
    c = jnp.concatenate([cos, cos, pad + 1.0], axis=1)
    sa = jnp.concatenate([-sin, zeros, pad], axis=1)
    sb = jnp.concatenate([zeros, sin, pad], axis=1)
    return tuple(jnp.tile(t, (1, LANES // A_DIM)) for t in (c, sa, sb))


def _stream_layer(x, mod, lw, rope_tabs, mixer, *, per_row, seq, final_norm, g_final):
    shift1, scale1, gate1, shift2, scale2, gate2 = mod
    proj = _in_projection(x, shift1, scale1, lw["g_attn"], lw["w_in"], rope_tabs, per_row=per_row, seq=seq)
    qa, kaf, vaf, kab, vab, qb, kbf, vbf, kbb, vbb = proj
    oa, ob, extra = mixer(qa, kaf, vaf, kab, vab, qb, kbf, vbf, kbb, vbb)
    x1, ht = _out_projection(x, oa, ob, lw["w_out"], gate1, shift2, scale2, lw["g_ffn"],
                             per_row=per_row, seq=seq)
    route = _peer_route(ht, lw["pk_wq_t"], lw["pk_keys"])
    x2 = _peer_experts(ht, lw["pk_u"], lw["pk_v_t"], route, x1, gate2, g_final,
                       per_row=per_row, seq=seq, final_norm=final_norm)
    return x2, (kaf, vaf, kbf, vbf) + extra


def kernel(x_prompt, x_sample, c_prompt, c_sample, cache_a_k, cache_a_v, cache_b_k, cache_b_v, w_ada, b_ada, g_attn, g_ffn, w_in, lam_q1, lam_k1, lam_q2, lam_k2, a_gain, rel_bias, b_gain, w_out, pk_wq, pk_keys, pk_u, pk_v, g_final):
    batch, seq, d = x_prompt.shape
    nb, t_new, _ = x_sample.shape
    depth = w_ada.shape[0]
    past = cache_a_k.shape[2]
    wb = cache_b_k.shape[2]
    a_width = A_HEADS * 2 * A_DIM
    b_width = B_HEADS * B_DIM

    c_all = jnp.concatenate([c_prompt, c_sample], axis=0)
    pad_rows = (-c_all.shape[0]) % 16
    c_all = jnp.pad(c_all, ((0, pad_rows), (0, 0)))
    mod = _modulation(c_all, w_ada, b_ada)

    tabs_p = _rope_tables(jnp.arange(seq, dtype=jnp.int32))
    pos_s = past + jnp.arange(t_new, dtype=jnp.int32)
    tabs_s = tuple(jnp.tile(t, (nb, 1)) for t in _rope_tables(pos_s))

    cak = cache_a_k.reshape(depth, nb, past * A_HEADS, 2 * A_DIM)
    cav = cache_a_v.reshape(depth, nb, past * A_HEADS, 2 * A_DIM)
    cbk = cache_b_k.reshape(depth, nb, wb, b_width)
    cbv = cache_b_v.reshape(depth, nb, wb, b_width)

    xp = x_prompt.reshape(batch * seq, d)
    xs = x_sample.reshape(nb * t_new, d)
    g_fin = g_final.reshape(1, d)
    st_p, st_s = [], []
    for l in range(depth):
        lam_init = 0.8 - 0.6 * math.exp(-0.3 * l)
        lamv = jnp.stack([lam_q1[l], lam_k1[l], lam_q2[l], lam_k2[l]])
        lw = {
            "g_attn": g_attn[l].reshape(1, d),
            "g_ffn": g_ffn[l].reshape(1, d),
            "w_in": w_in[l].astype(_BF),
            "w_out": w_out[l].astype(_BF),
            "pk_wq_t": pk_wq[l].T.astype(_BF),
            "pk_keys": pk_keys[l].reshape(2 * PK_HEADS, N_KEYS, PK_HALF).astype(_BF),
            "pk_u": pk_u[l].astype(_BF),
            "pk_v_t": pk_v[l].astype(_BF).reshape(-1, PEER_EXPERTS, d).transpose(0, 2, 1),
        }
        gain_a = a_gain[l].reshape(1, 2 * A_DIM)
        gain_b = b_gain[l].reshape(1, b_width)
        mod_p = tuple(m.reshape(batch, 1, d) for m in jnp.split(mod[l, :batch], 6, axis=-1))
        mod_s = tuple(jnp.repeat(m, t_new, axis=0) for m in jnp.split(mod[l, batch:batch + nb], 6, axis=-1))
        last = l == depth - 1

        def mix_p(qa, kaf, vaf, kab, vab, qb, kbf, vbf, kbb, vbb):
            oa = _attn_a_prompt(qa, kab, vab, lamv, gain_a, batch=batch, seq=seq, lam_init=lam_init)
            ob = _attn_b_prompt(qb, kbb, vbb, rel_bias[l], gain_b, batch=batch, seq=seq)
            return oa, ob, ()

        def mix_s(qa, kaf, vaf, kab, vab, qb, kbf, vbf, kbb, vbb):
            oa = _attn_a_sample(qa, cak, cav, kaf, vaf, lamv, gain_a, layer=l, lam_init=lam_init, t_new=t_new)
            ob, nbk, nbv = _attn_b_sample(qb, cbk, cbv, kbf, vbf, rel_bias[l], gain_b, layer=l, t_new=t_new)
            return oa, ob, (nbk, nbv)

        xp, sp = _stream_layer(xp, mod_p, lw, tabs_p, mix_p, per_row=False, seq=seq,
                               final_norm=last, g_final=g_fin)
        xs, ss = _stream_layer(xs, mod_s, lw, tabs_s, mix_s, per_row=True, seq=t_new,
                               final_norm=last, g_final=g_fin)
        st_p.append(sp)
        st_s.append(ss)

    keep = min(B_REACH, seq)
    a_shape_p = (batch, seq, A_HEADS, 2 * A_DIM)
    b_tail = lambda s: s.reshape(batch, seq, B_HEADS, B_DIM)[:, seq - keep:]
    a_shape_s = (nb, t_new, A_HEADS, 2 * A_DIM)
    b_shape_s = (nb, wb, B_HEADS, B_DIM)
    return (
        xp.reshape(batch, seq, d),
        xs.reshape(nb, t_new, d),
        jnp.stack([s[0].reshape(a_shape_p) for s in st_p]),
        jnp.stack([s[1].reshape(a_shape_p) for s in st_p]),
        jnp.stack([b_tail(s[2]) for s in st_p]),
        jnp.stack([b_tail(s[3]) for s in st_p]),
        jnp.stack([s[0].reshape(a_shape_s) for s in st_s]),
        jnp.stack([s[1].reshape(a_shape_s) for s in st_s]),
        jnp.stack([s[4].reshape(b_shape_s) for s in st_s]),
        jnp.stack([s[5].reshape(b_shape_s) for s in st_s]),
    )
```

```python
import functools
import math

import jax
import jax.numpy as jnp
import numpy as np
from jax import lax
from jax.experimental import pallas as pl
from jax.experimental.pallas import tpu as pltpu

_F32 = jnp.float32
_BF = jnp.bfloat16

CHUNK = 64
A_HEADS = 4
A_DIM = 64
ROT_DIM = A_DIM // 4
ROPE_THETA = 500000.0
B_HEADS = 8
B_DIM = 64
LEFT_CHUNKS = 8
B_REACH = LEFT_CHUNKS * CHUNK
REL_CLIP = 128
PK_HEADS = 8
N_KEYS = 128
PK_TOPK = 16
PK_HALF = 128
EPS = 1e-6

LANES = 128
MASKED = -1e30
VMEM_LIMIT = 56 * 1024 * 1024

A_TILE = 512
B_TILE = 256
PEER_TOKENS = 1024
ROUTE_TOKENS = 512
PEER_EXPERTS = 1024
PEER_CHUNK = 256
ROW_TILE = 512


def _params(*sem):
    return pltpu.CompilerParams(dimension_semantics=sem, vmem_limit_bytes=VMEM_LIMIT)


def _resident(shape, index_map):
    return pl.BlockSpec(shape, index_map, pipeline_mode=pl.Buffered(1))


def _rms(x):
    return x * lax.rsqrt(jnp.mean(x * x, axis=-1, keepdims=True) + EPS)


def _dot(a, b):
    return jnp.dot(a, b, preferred_element_type=_F32)


def _dot_nt(a, b):
    return lax.dot_general(a, b, (((1,), (1,)), ((), ())), preferred_element_type=_F32)


def _mod_kernel(c_ref, w_ref, b_ref, o_ref):
    c = c_ref[...]
    s = c * (1.0 / (1.0 + jnp.exp(-c)))
    o_ref[...] = _dot(s.astype(_BF), w_ref[...].astype(_BF)) + b_ref[...]


def _modulation(c_all, w_ada, b_ada):
    depth, d, n = w_ada.shape
    rows = c_all.shape[0]
    tn = n // 4
    return pl.pallas_call(
        _mod_kernel,
        grid=(depth, n // tn),
        in_specs=[
            pl.BlockSpec((rows, d), lambda l, j: (0, 0)),
            pl.BlockSpec((None, d, tn), lambda l, j: (l, 0, j)),
            pl.BlockSpec((None, 1, tn), lambda l, j: (l, 0, j)),
        ],
        out_specs=pl.BlockSpec((None, rows, tn), lambda l, j: (l, 0, j)),
        out_shape=jax.ShapeDtypeStruct((depth, rows, n), _F32),
        compiler_params=_params("arbitrary", "arbitrary"),
        name="adaln_mod",
    )(c_all, w_ada, b_ada.reshape(depth, 1, n))


def _inproj_kernel(x_ref, sh_ref, sc_ref, g_ref, w_ref, cos_ref, sa_ref, sb_ref,
                   qa_ref, kaf_ref, vaf_ref, kab_ref, vab_ref,
                   qb_ref, kbf_ref, vbf_ref, kbb_ref, vbb_ref, *, width):
    h = _rms(x_ref[...]) * g_ref[...] * (1.0 + sc_ref[...]) + sh_ref[...]
    hb = h.astype(_BF)
    reps = width // LANES
    cos = jnp.tile(cos_ref[...], (1, reps))
    sa = jnp.tile(sa_ref[...], (1, reps))
    sb = jnp.tile(sb_ref[...], (1, reps))

    def rope(z):
        return z * cos + pltpu.roll(z, width - ROT_DIM // 2, 1) * sa + pltpu.roll(z, ROT_DIM // 2, 1) * sb

    def proj(i):
        return _dot(hb, w_ref[:, i * width:(i + 1) * width])

    qa_ref[...] = (rope(proj(0)) * (A_DIM ** -0.5)).astype(_BF)
    ka = rope(proj(1))
    kaf_ref[...] = ka
    kab_ref[...] = ka.astype(_BF)
    va = proj(2)
    vaf_ref[...] = va
    vab_ref[...] = va.astype(_BF)
    qb_ref[...] = (proj(3) * (B_DIM ** -0.5)).astype(_BF)
    kb = proj(4)
    kbf_ref[...] = kb
    kbb_ref[...] = kb.astype(_BF)
    vb = proj(5)
    vbf_ref[...] = vb
    vbb_ref[...] = vb.astype(_BF)


def _row_mod_specs(per_row, tm, d, tiles_per_batch):
    if per_row:
        return pl.BlockSpec((tm, d), lambda t: (t, 0))
    return pl.BlockSpec((None, 1, d), lambda t: (t // tiles_per_batch, 0, 0))


def _in_projection(x, shift, scale, gain, w_in_bf, rope_tabs, *, per_row, seq):
    t_all, d = x.shape
    width = w_in_bf.shape[1] // 6
    tm = min(ROW_TILE, t_all)
    tiles_per_batch = max(seq // tm, 1)
    tab_tiles = rope_tabs[0].shape[0] // tm
    mod_spec = _row_mod_specs(per_row, tm, d, tiles_per_batch)
    tab_spec = pl.BlockSpec((tm, LANES), lambda t: (t % tab_tiles, 0))
    row = lambda dt: jax.ShapeDtypeStruct((t_all, width), dt)
    out_spec = pl.BlockSpec((tm, width), lambda t: (t, 0))
    dts = [_BF, _F32, _F32, _BF, _BF, _BF, _F32, _F32, _BF, _BF]
    return pl.pallas_call(
        functools.partial(_inproj_kernel, width=width),
        grid=(t_all // tm,),
        in_specs=[
            pl.BlockSpec((tm, d), lambda t: (t, 0)),
            mod_spec, mod_spec,
            _resident((1, d), lambda t: (0, 0)),
            _resident(w_in_bf.shape, lambda t: (0, 0)),
            tab_spec, tab_spec, tab_spec,
        ],
        out_specs=[out_spec] * 10,
        out_shape=[row(dt) for dt in dts],
        compiler_params=_params("arbitrary"),
        name="in_projection",
    )(x, shift, scale, gain, w_in_bf, *rope_tabs)


def _diff_lambda(lamv_ref, lam_init):
    v = lamv_ref[...]
    d1 = jnp.sum(v[0:1] * v[1:2], axis=1, keepdims=True)
    d2 = jnp.sum(v[2:3] * v[3:4], axis=1, keepdims=True)
    return jnp.exp(d1) - jnp.exp(d2) + lam_init


def _split_maps(q):
    lane = lax.broadcasted_iota(jnp.int32, q.shape, 1)
    zero = jnp.zeros_like(q)
    return jnp.where(lane < A_DIM, q, zero), jnp.where(lane >= A_DIM, q, zero)


def _attn_a_kernel(lamv_ref, q_ref, k_ref, v_ref, gain_ref, o_ref,
                   sa, sb, m1, l1, acc1, m2, l2, acc2, *, lam_init):
    n_full = pl.program_id(2)
    tq = q_ref.shape[0]
    q1, q2 = _split_maps(q_ref[...])
    for m_scr, l_scr, acc in ((m1, l1, acc1), (m2, l2, acc2)):
        m_scr[...] = jnp.full(m_scr.shape, MASKED, _F32)
        l_scr[...] = jnp.zeros(l_scr.shape, _F32)
        acc[...] = jnp.zeros(acc.shape, _F32)

    def rows(j):
        return pl.ds(pl.multiple_of(j * tq, tq), tq)

    def scores(j, s_scr):
        kb = k_ref[rows(j), :]
        s_scr[0] = _dot_nt(q1, kb)
        s_scr[1] = _dot_nt(q2, kb)

    def update(s, m_scr, l_scr, acc, vb):
        m_prev = m_scr[...]
        m_next = jnp.maximum(m_prev, jnp.max(s, axis=1, keepdims=True))
        p = jnp.exp(s - jnp.tile(m_next, (1, tq // LANES)))
        alpha = jnp.exp(m_prev - m_next)
        part = p[:, 0:LANES]
        for c in range(1, tq // LANES):
            part = part + p[:, c * LANES:(c + 1) * LANES]
        l_scr[...] = alpha * l_scr[...] + part
        acc[...] = acc[...] * alpha + _dot(p.astype(_BF), vb)
        m_scr[...] = m_next

    def process(j, s_scr, diagonal):
        vb = v_ref[rows(j), :]
        for mp, (m_scr, l_scr, acc) in enumerate(((m1, l1, acc1), (m2, l2, acc2))):
            s = s_scr[mp]
            if diagonal:
                qc = lax.broadcasted_iota(jnp.int32, s.shape, 0) // CHUNK
                kc = lax.broadcasted_iota(jnp.int32, s.shape, 1) // CHUNK
                s = jnp.where(kc <= qc, s, MASKED)
            update(s, m_scr, l_scr, acc, vb)

    scores(0, sa)

    def pair(i, carry):
        scores(2 * i + 1, sb)
        process(2 * i, sa, False)
        scores(2 * i + 2, sa)
        process(2 * i + 1, sb, False)
        return carry

    lax.fori_loop(0, n_full // 2, pair, 0)

    @pl.when(n_full % 2 == 1)
    def _():
        scores(n_full, sb)
        process(n_full - 1, sa, False)
        process(n_full, sb, True)

    @pl.when(n_full % 2 == 0)
    def _():
        process(n_full, sa, True)

    lam = _diff_lambda(lamv_ref, lam_init)
    d1 = jnp.sum(l1[...], axis=1, keepdims=True)
    d2 = jnp.sum(l2[...], axis=1, keepdims=True)
    o = acc1[...] / d1 - lam * (acc2[...] / d2)
    o_ref[...] = (_rms(o) * gain_ref[...] * (1.0 - lam_init)).astype(_BF)


def _attn_a_prompt(qa, ka, va, lamv, a_gain, *, batch, seq, lam_init):
    t_all, width = qa.shape
    hd = 2 * A_DIM
    tq = A_TILE
    nq = seq // tq
    return pl.pallas_call(
        functools.partial(_attn_a_kernel, lam_init=lam_init),
        grid=(batch, A_HEADS, nq),
        in_specs=[
            _resident(lamv.shape, lambda b, h, i: (0, 0)),
            pl.BlockSpec((tq, hd), lambda b, h, i: (b * nq + i, h)),
            pl.BlockSpec((seq, hd), lambda b, h, i: (b, h)),
            pl.BlockSpec((seq, hd), lambda b, h, i: (b, h)),
            _resident((1, hd), lambda b, h, i: (0, 0)),
        ],
        out_specs=pl.BlockSpec((tq, hd), lambda b, h, i: (b * nq + i, h)),
        out_shape=jax.ShapeDtypeStruct((t_all, width), _BF),
        scratch_shapes=[pltpu.VMEM((2, tq, tq), _F32)] * 2 + [pltpu.VMEM((tq, hd), _F32)] * 6,
        compiler_params=_params("arbitrary", "arbitrary", "arbitrary"),
        name="diff_attn_prompt",
    )(lamv, qa, ka, va, a_gain)


def _head_lanes(shape, head, dim):
    lane = lax.broadcasted_iota(jnp.int32, shape, 1)
    return (lane >= head * dim) & (lane < (head + 1) * dim)


def _attn_b_kernel(q_ref, k0_ref, k1_ref, k2_ref, v0_ref, v1_ref, v2_ref, bias_ref, gain_ref, o_ref):
    qi = pl.program_id(1)
    tq = q_ref.shape[0]
    k_refs = (k0_ref, k1_ref, k2_ref)
    v_refs = (v0_ref, v1_ref, v2_ref)
    pen = (jnp.where(qi >= 2, 0.0, MASKED), jnp.where(qi >= 1, 0.0, MASKED), 0.0)
    outs = []
    for pair in range(B_HEADS // 2):
        cols = slice(pair * LANES, (pair + 1) * LANES)
        qp = q_ref[:, cols]
        zero = jnp.zeros_like(qp)
        halves = []
        for sub in range(2):
            head = 2 * pair + sub
            qh = jnp.where(_head_lanes(qp.shape, sub, B_DIM), qp, zero)
            s = [_dot_nt(qh, k_refs[b][:, cols]) + bias_ref[head, :, b * tq:(b + 1) * tq] + pen[b]
                 for b in range(3)]
            m = jnp.maximum(jnp.maximum(jnp.max(s[0], axis=1, keepdims=True),
                                        jnp.max(s[1], axis=1, keepdims=True)),
                            jnp.max(s[2], axis=1, keepdims=True))
            p = [jnp.exp(sb - m) for sb in s]
            l = (jnp.sum(p[0], axis=1, keepdims=True) + jnp.sum(p[1], axis=1, keepdims=True)
                 + jnp.sum(p[2], axis=1, keepdims=True))
            o = (_dot(p[0].astype(_BF), v_refs[0][:, cols]) + _dot(p[1].astype(_BF), v_refs[1][:, cols])
                 + _dot(p[2].astype(_BF), v_refs[2][:, cols]))
            halves.append(o / l)
        outs.append(jnp.where(_head_lanes(halves[0].shape, 0, B_DIM), halves[0], halves[1]))
    o = jnp.concatenate(outs, axis=1)
    o_ref[...] = (_rms(o) * gain_ref[...]).astype(_BF)


def _band_bias(table, tq):
    nk = B_REACH + tq
    qpos = np.arange(tq)[:, None]
    kpos = np.arange(nk)[None, :] - B_REACH
    qc = qpos // CHUNK
    kc = np.floor_divide(kpos, CHUNK)
    valid = (kc <= qc) & (kc >= qc - LEFT_CHUNKS)
    span = nk + tq - 1
    dist = (span - 1 - np.arange(span)) - (tq - 1)
    line = table[:, np.clip(dist, -REL_CLIP, REL_CLIP) + REL_CLIP].astype(_F32)
    line = jnp.pad(line, ((0, 0), (0, 1)))
    skew = jnp.tile(line, (1, tq))[:, :tq * span].reshape(-1, tq, span)
    bias = skew[:, :, tq - 1:]
    return jnp.where(valid[None], bias, MASKED)


def _attn_b_prompt(qb, kb, vb, rel_table, b_gain, *, batch, seq):
    t_all, width = qb.shape
    tq = B_TILE
    assert B_REACH == 2 * tq
    nq = seq // tq
    bias = _band_bias(rel_table, tq)
    qmap = lambda b, i: (b * nq + i, 0)
    kmap = lambda back: (lambda b, i: (b * nq + jnp.maximum(i - back, 0), 0))
    blk = lambda f: pl.BlockSpec((tq, width), f)
    return pl.pallas_call(
        _attn_b_kernel,
        grid=(batch, nq),
        in_specs=[blk(qmap), blk(kmap(2)), blk(kmap(1)), blk(kmap(0)),
                  blk(kmap(2)), blk(kmap(1)), blk(kmap(0)),
                  _resident(bias.shape, lambda b, i: (0, 0, 0)),
                  _resident((1, width), lambda b, i: (0, 0))],
        out_specs=blk(qmap),
        out_shape=jax.ShapeDtypeStruct((t_all, width), _BF),
        compiler_params=_params("arbitrary", "arbitrary"),
        name="band_attn_prompt",
    )(qb, kb, kb, kb, vb, vb, vb, bias, b_gain)


def _attn_a_sample_kernel(lamv_ref, q_ref, kc_ref, vc_ref, kn_ref, vn_ref, gain_ref, o_ref, *, lam_init):
    q = q_ref[...]
    t = q.shape[0]
    hd = 2 * A_DIM
    past = kc_ref.shape[0] // A_HEADS
    lam = _diff_lambda(lamv_ref, lam_init)
    outs = []
    for h in range(A_HEADS):
        cols = slice(h * hd, (h + 1) * hd)
        head_rows = pl.ds(h, past, stride=A_HEADS)
        q12 = jnp.concatenate(_split_maps(q[:, cols]), axis=0)
        kn = kn_ref[:, cols].astype(_BF)
        vn = vn_ref[:, cols].astype(_BF)
        s_c = _dot_nt(q12, kc_ref[head_rows, :].astype(_BF))
        s_n = _dot_nt(q12, kn)
        m = jnp.maximum(jnp.max(s_c, axis=1, keepdims=True), jnp.max(s_n, axis=1, keepdims=True))
        p_c = jnp.exp(s_c - m)
        p_n = jnp.exp(s_n - m)
        l = jnp.sum(p_c, axis=1, keepdims=True) + jnp.sum(p_n, axis=1, keepdims=True)
        p_c = p_c / l
        p_n = p_n / l
        a_c = p_c[0:t] - lam * p_c[t:2 * t]
        a_n = p_n[0:t] - lam * p_n[t:2 * t]
        o = _dot(a_c.astype(_BF), vc_ref[head_rows, :].astype(_BF)) + _dot(a_n.astype(_BF), vn)
        outs.append(_rms(o) * gain_ref[...] * (1.0 - lam_init))
    o_ref[...] = jnp.concatenate(outs, axis=1).astype(_BF)


def _attn_a_sample(qa, cache_k, cache_v, kn, vn, lamv, a_gain, *, layer, lam_init, t_new):
    t_all, width = qa.shape
    _, nb, cache_rows, hd = cache_k.shape
    tok = pl.BlockSpec((t_new, width), lambda b: (b, 0))
    cache = pl.BlockSpec((None, None, cache_rows, hd), lambda b: (layer, b, 0, 0))
    return pl.pallas_call(
        functools.partial(_attn_a_sample_kernel, lam_init=lam_init),
        grid=(nb,),
        in_specs=[_resident(lamv.shape, lambda b: (0, 0)), tok, cache, cache, tok, tok,
                  _resident((1, 2 * A_DIM), lambda b: (0, 0))],
        out_specs=tok,
        out_shape=jax.ShapeDtypeStruct((t_all, width), _BF),
        compiler_params=_params("arbitrary"),
        name="diff_attn_sample",
    )(lamv, qa, cache_k, cache_v, kn, vn, a_gain)


def _attn_b_sample_kernel(q_ref, kc_ref, vc_ref, kn_ref, vn_ref, bc_ref, bn_ref, gain_ref,
                          o_ref, ko_ref, vo_ref):
    q = q_ref[...]
    t = q.shape[0]
    keep = kc_ref.shape[0] - t
    zero = jnp.zeros_like(q)
    qall = jnp.concatenate(
        [jnp.where(_head_lanes(q.shape, h, B_DIM), q, zero) for h in range(B_HEADS)], axis=0)
    kc = kc_ref[...]
    vc = vc_ref[...]
    kn = kn_ref[...]
    vn = vn_ref[...]
    s_c = _dot_nt(qall, kc.astype(_BF)) + bc_ref[...]
    s_n = _dot_nt(qall, kn.astype(_BF)) + bn_ref[...]
    m = jnp.maximum(jnp.max(s_c, axis=1, keepdims=True), jnp.max(s_n, axis=1, keepdims=True))
    p_c = jnp.exp(s_c - m)
    p_n = jnp.exp(s_n - m)
    l = jnp.sum(p_c, axis=1, keepdims=True) + jnp.sum(p_n, axis=1, keepdims=True)
    o_all = (_dot(p_c.astype(_BF), vc.astype(_BF)) + _dot(p_n.astype(_BF), vn.astype(_BF))) / l
    o = jnp.zeros((t, q.shape[1]), _F32)
    for h in range(B_HEADS):
        o = jnp.where(_head_lanes(o.shape, h, B_DIM), o_all[h * t:(h + 1) * t], o)
    o_ref[...] = (_rms(o) * gain_ref[...]).astype(_BF)
    ko_ref[0:keep, :] = kc[t:, :]
    ko_ref[keep:, :] = kn
    vo_ref[0:keep, :] = vc[t:, :]
    vo_ref[keep:, :] = vn


def _sample_bias(table, t_new, wb):
    qpos = jnp.arange(t_new)[:, None] + wb
    kpos = jnp.arange(wb + t_new)[None, :]
    d = jnp.clip(qpos - kpos, -REL_CLIP, REL_CLIP) + REL_CLIP
    bias = table[:, d].astype(_F32).reshape(B_HEADS * t_new, wb + t_new)
    return bias[:, :wb], bias[:, wb:]


def _attn_b_sample(qb, cache_k, cache_v, kn, vn, rel_table, b_gain, *, layer, t_new):
    t_all, width = qb.shape
    _, nb, wb, _ = cache_k.shape
    bias_c, bias_n = _sample_bias(rel_table, t_new, wb)
    tok = pl.BlockSpec((t_new, width), lambda b: (b, 0))
    cache = pl.BlockSpec((None, None, wb, width), lambda b: (layer, b, 0, 0))
    state = pl.BlockSpec((None, wb, width), lambda b: (b, 0, 0))
    return pl.pallas_call(
        _attn_b_sample_kernel,
        grid=(nb,),
        in_specs=[tok, cache, cache, tok, tok,
                  _resident(bias_c.shape, lambda b: (0, 0)), _resident(bias_n.shape, lambda b: (0, 0)),
                  _resident((1, width), lambda b: (0, 0))],
        out_specs=[tok, state, state],
        out_shape=[jax.ShapeDtypeStruct((t_all, width), _BF),
                   jax.ShapeDtypeStruct((nb, wb, width), _F32),
                   jax.ShapeDtypeStruct((nb, wb, width), _F32)],
        compiler_params=_params("arbitrary"),
        name="band_attn_sample",
    )(qb, cache_k, cache_v, kn, vn, bias_c, bias_n, b_gain)


def _outproj_kernel(x_ref, oa_ref, ob_ref, w_ref, gt_ref, sh_ref, sc_ref, g_ref, x1_ref, ht_ref):
    half = oa_ref.shape[1]
    y = _dot(oa_ref[...], w_ref[0:half, :]) + _dot(ob_ref[...], w_ref[half:, :])
    x1 = x_ref[...] + gt_ref[...] * y
    x1_ref[...] = x1
    h = _rms(x1) * g_ref[...] * (1.0 + sc_ref[...]) + sh_ref[...]
    ht_ref[...] = h.T.astype(_BF)


def _out_projection(x, oa, ob, w_out_bf, gate, shift, scale, gain, *, per_row, seq):
    t_all, d = x.shape
    half = oa.shape[1]
    tm = min(PEER_TOKENS, t_all)
    mod_spec = _row_mod_specs(per_row, tm, d, max(seq // tm, 1))
    rows = lambda w: pl.BlockSpec((tm, w), lambda t: (t, 0))
    return pl.pallas_call(
        _outproj_kernel,
        grid=(t_all // tm,),
        in_specs=[rows(d), rows(half), rows(half), _resident(w_out_bf.shape, lambda t: (0, 0)),
                  mod_spec, mod_spec, mod_spec, _resident((1, d), lambda t: (0, 0))],
        out_specs=[rows(d), pl.BlockSpec((None, d, tm), lambda t: (t, 0, 0))],
        out_shape=[jax.ShapeDtypeStruct((t_all, d), _F32),
                   jax.ShapeDtypeStruct((t_all // tm, d, tm), _BF)],
        compiler_params=_params("arbitrary"),
        name="out_projection",
    )(x, oa, ob, w_out_bf, gate, shift, scale, gain)


def _top_values(s, k, want_rank=False):
    vals = []
    rank = jnp.full(s.shape, float(k), _F32)
    for i in range(k):
        m = jnp.max(s, axis=0, keepdims=True)
        vals.append(m)
        hit = s == m
        if want_rank:
            rank = jnp.where(hit, float(i), rank)
        s = jnp.where(hit, -jnp.inf, s)
    return (vals, rank) if want_rank else vals


def _stack_rows(vals):
    rows = lax.broadcasted_iota(jnp.int32, (len(vals), LANES), 0)
    out = jnp.zeros((len(vals), LANES), _F32)
    for i, v in enumerate(vals):
        out = jnp.where(rows == i, v, out)
    return out


def _route_kernel(ht_ref, wq_ref, keys_ref, n1_ref, e1_ref, r2_ref, e2_ref, q_scr, s_scr):
    tt = ht_ref.shape[1]
    q_scr[...] = _dot(wq_ref[...], ht_ref[...]).astype(_BF)
    for hm in range(2 * PK_HEADS):
        s = _dot(keys_ref[hm], q_scr[hm * PK_HALF:(hm + 1) * PK_HALF, :])
        for c in range(tt // LANES):
            s_scr[hm, c] = s[:, c * LANES:(c + 1) * LANES]

    def head(h, carry):
        for c in range(tt // LANES):
            cols = slice(c * LANES, (c + 1) * LANES)
            s1 = s_scr[2 * h, c]
            s2 = s_scr[2 * h + 1, c]
            a = _top_values(s1, PK_TOPK)
            b, rank2 = _top_values(s2, PK_TOPK, want_rank=True)
            amat = _stack_rows(a)
            bmat = _stack_rows(b)
            cand = [a[0] + bmat]
            cand += [a[p] + bmat[0:8] for p in range(1, 8)]
            cand += [amat[8:16] + b[0]]
            top = _top_values(jnp.concatenate(cand, axis=0), PK_TOPK)
            tau = top[PK_TOPK - 1]
            z = jnp.ones_like(tau)
            for k in range(1, PK_TOPK):
                z = z + jnp.exp(top[k] - top[0])
            count1 = jnp.zeros(s1.shape, _F32)
            for p in range(PK_TOPK):
                reach = jnp.sum(jnp.where(a[p] + bmat >= tau, 1.0, 0.0), axis=0, keepdims=True)
                count1 = jnp.where(s1 == a[p], reach, count1)
            n1_ref[h, :, cols] = count1
            e1_ref[h, :, cols] = jnp.exp(s1 - a[0]) * (0.5 / z)
            r2_ref[h, :, cols] = rank2.astype(_BF)
            e2_ref[h, :, cols] = jnp.exp(s2 - b[0]).astype(_BF)
        return carry

    lax.fori_loop(0, PK_HEADS, head, 0)


def _peer_route(ht, wq_t_bf, keys_bf):
    slabs, d, slab = ht.shape
    t_all = slabs * slab
    tt = min(ROUTE_TOKENS, slab)
    per_slab = slab // tt
    nq = wq_t_bf.shape[0]
    out = lambda dt: jax.ShapeDtypeStruct((PK_HEADS, N_KEYS, t_all), dt)
    tile = pl.BlockSpec((PK_HEADS, N_KEYS, tt), lambda t: (0, 0, t))
    return pl.pallas_call(
        _route_kernel,
        grid=(t_all // tt,),
        in_specs=[pl.BlockSpec((None, d, tt), lambda t: (t // per_slab, 0, t % per_slab)),
                  _resident(wq_t_bf.shape, lambda t: (0, 0)),
                  _resident(keys_bf.shape, lambda t: (0, 0, 0))],
        out_specs=[tile] * 4,
        out_shape=[out(_F32), out(_F32), out(_BF), out(_BF)],
        scratch_shapes=[pltpu.VMEM((nq, tt), _BF),
                        pltpu.VMEM((2 * PK_HEADS, tt // LANES, N_KEYS, LANES), _F32)],
        compiler_params=_params("arbitrary"),
        name="peer_route",
    )(ht, wq_t_bf, keys_bf)


def _twice_gelu_tanh(x):
    return x * (1.0 + jnp.tanh(x * (0.7978845608028654 + (0.7978845608028654 * 0.044715) * (x * x))))


def _row_to_tile(row):
    packed = jnp.broadcast_to(row, (16, LANES)).astype(_BF)
    return jnp.tile(packed, (N_KEYS // 16, 1))


def _peer_kernel(ht_ref, u_ref, vt_ref, n1_ref, e1_ref, r2_ref, e2_ref, x_ref, gt_ref, gf_ref,
                 o_ref, at_scr, wt_scr, acc, r2_scr, e2_scr, *, final_norm):
    e = pl.program_id(1)
    tt = ht_ref.shape[1]
    groups = u_ref.shape[0] // N_KEYS
    span = acc.shape[2]
    n_chunks = tt // span
    subs = span // LANES

    @pl.when(e == 0)
    def _():
        acc[...] = jnp.zeros(acc.shape, _F32)
        for cb in range(tt // LANES):
            cols = slice(cb * LANES, (cb + 1) * LANES)
            r2_scr[:, cb] = r2_ref[:, :, cols]
            e2_scr[:, cb] = e2_ref[:, :, cols]

    for c in range(n_chunks):
        a_t = _dot(u_ref[...], ht_ref[:, c * span:(c + 1) * span])
        for sub in range(subs):
            at_scr[c * subs + sub] = a_t[:, sub * LANES:(sub + 1) * LANES]
    for c in range(n_chunks):
        for sub in range(subs):
            cb = c * subs + sub
            cols = slice(cb * LANES, (cb + 1) * LANES)
            zero = jnp.zeros((N_KEYS, LANES), _BF)
            for il in range(groups):
                rows = slice(il * N_KEYS, (il + 1) * N_KEYS)
                g = None
                for h in range(PK_HEADS):
                    count = _row_to_tile(n1_ref[h, il:il + 1, cols])
                    sel = jnp.where(r2_scr[h, cb] < count, e2_scr[h, cb], zero)
                    term = sel * _row_to_tile(e1_ref[h, il:il + 1, cols])
                    g = term if g is None else g + term
                act = _twice_gelu_tanh(at_scr[cb, rows, :]).astype(_BF)
                wt_scr[cb, rows, :] = g * act
        w_t = jnp.concatenate([wt_scr[c * subs + sub] for sub in range(subs)], axis=1)
        acc[c] += _dot(vt_ref[...], w_t)

    @pl.when(e == pl.num_programs(1) - 1)
    def _():
        f = jnp.concatenate([acc[c].T for c in range(n_chunks)], axis=0)
        y = x_ref[...] + gt_ref[...] * f
        if final_norm:
            y = _rms(y) * gf_ref[...]
        o_ref[...] = y


def _peer_experts(ht, u_bf, vt_bf, route, x1, gate, g_final, *, per_row, seq, final_norm):
    tiles, d, tt = ht.shape
    steps, _, eb = vt_bf.shape
    t_all = tiles * tt
    span = min(PEER_CHUNK, tt)
    groups = eb // N_KEYS
    assert groups == 8
    n1, e1, r2, e2 = route
    by_step = lambda a: a.reshape(PK_HEADS, N_KEYS // groups, groups, t_all)
    gate_spec = (pl.BlockSpec((tt, d), lambda t, e: (t, 0)) if per_row else
                 pl.BlockSpec((None, 1, d), lambda t, e: (t // max(seq // tt, 1), 0, 0)))
    step_rows = pl.BlockSpec((PK_HEADS, None, groups, tt), lambda t, e: (0, e, 0, t))
    tile = _resident((PK_HEADS, N_KEYS, tt), lambda t, e: (0, 0, t))
    return pl.pallas_call(
        functools.partial(_peer_kernel, final_norm=final_norm),
        grid=(tiles, steps),
        in_specs=[pl.BlockSpec((None, d, tt), lambda t, e: (t, 0, 0)),
                  pl.BlockSpec((eb, d), lambda t, e: (e, 0)),
                  pl.BlockSpec((None, d, eb), lambda t, e: (e, 0, 0)),
                  step_rows, step_rows, tile, tile,
                  _resident((tt, d), lambda t, e: (t, 0)),
                  gate_spec,
                  _resident((1, d), lambda t, e: (0, 0))],
        out_specs=pl.BlockSpec((tt, d), lambda t, e: (t, 0)),
        out_shape=jax.ShapeDtypeStruct((t_all, d), _F32),
        scratch_shapes=[pltpu.VMEM((tt // LANES, eb, LANES), _F32), pltpu.VMEM((tt // LANES, eb, LANES), _BF),
                        pltpu.VMEM((tt // span, d, span), _F32),
                        pltpu.VMEM((PK_HEADS, tt // LANES, N_KEYS, LANES), _BF),
                        pltpu.VMEM((PK_HEADS, tt // LANES, N_KEYS, LANES), _BF)],
        compiler_params=_params("arbitrary", "arbitrary"),
        name="peer_experts",
    )(ht, u_bf, vt_bf, by_step(n1), by_step(e1), r2, e2, x1, gate, g_final)


def _rope_tables(pos):
    half = ROT_DIM // 2
    inv = ROPE_THETA ** (-jnp.arange(0, ROT_DIM, 2, dtype=_F32) / ROT_DIM)
    ang = pos.astype(_F32)[:, None] * inv[None, :]
    cos, sin = jnp.cos(ang), jnp.sin(ang)
    n = pos.shape[0]
    pad = jnp.zeros((n, A_DIM - ROT_DIM), _F32)
    zeros = jnp.zeros((n, half), _F32)
    c = jnp.concatenate([cos, cos, pad + 1.0], axis=1)
    sa = jnp.concatenate([-sin, zeros, pad], axis=1)
    sb = jnp.concatenate([zeros, sin, pad], axis=1)
    return tuple(jnp.tile(t, (1, LANES // A_DIM)) for t in (c, sa, sb))


def _stream_layer(x, mod, lw, rope_tabs, mixer, *, per_row, seq, final_norm, g_final):
    shift1, scale1, gate1, shift2, scale2, gate2 = mod
    proj = _in_projection(x, shift1, scale1, lw["g_attn"], lw["w_in"], rope_tabs, per_row=per_row, seq=seq)
    qa, kaf, vaf, kab, vab, qb, kbf, vbf, kbb, vbb = proj
    oa, ob, extra = mixer(qa, kaf, vaf, kab, vab, qb, kbf, vbf, kbb, vbb)
    x1, ht = _out_projection(x, oa, ob, lw["w_out"], gate1, shift2, scale2, lw["g_ffn"],
                             per_row=per_row, seq=seq)
    route = _peer_route(ht, lw["pk_wq_t"], lw["pk_keys"])
    x2 = _peer_experts(ht, lw["pk_u"], lw["pk_v_t"], route, x1, gate2, g_final,
                       per_row=per_row, seq=seq, final_norm=final_norm)
    return x2, (kaf, vaf, kbf, vbf) + extra


def kernel(x_prompt, x_sample, c_prompt, c_sample, cache_a_k, cache_a_v, cache_b_k, cache_b_v, w_ada, b_ada, g_attn, g_ffn, w_in, lam_q1, lam_k1, lam_q2, lam_k2, a_gain, rel_bias, b_gain, w_out, pk_wq, pk_keys, pk_u, pk_v, g_final):
    batch, seq, d = x_prompt.shape
    nb, t_new, _ = x_sample.shape
    depth = w_ada.shape[0]
    past = cache_a_k.shape[2]
    wb = cache_b_k.shape[2]
    a_width = A_HEADS * 2 * A_DIM
    b_width = B_HEADS * B_DIM

    c_all = jnp.concatenate([c_prompt, c_sample], axis=0)
    pad_rows = (-c_all.shape[0]) % 16
    c_all = jnp.pad(c_all, ((0, pad_rows), (0, 0)))
    mod = _modulation(c_all, w_ada, b_ada)

    tabs_p = _rope_tables(jnp.arange(seq, dtype=jnp.int32))
    pos_s = past + jnp.arange(t_new, dtype=jnp.int32)
    tabs_s = tuple(jnp.tile(t, (nb, 1)) for t in _rope_tables(pos_s))

    cak = cache_a_k.reshape(depth, nb, past * A_HEADS, 2 * A_DIM)
    cav = cache_a_v.reshape(depth, nb, past * A_HEADS, 2 * A_DIM)
    cbk = cache_b_k.reshape(depth, nb, wb, b_width)
    cbv = cache_b_v.reshape(depth, nb, wb, b_width)

    xp = x_prompt.reshape(batch * seq, d)
    xs = x_sample.reshape(nb * t_new, d)
    g_fin = g_final.reshape(1, d)
    st_p, st_s = [], []
    for l in range(depth):
        lam_init = 0.8 - 0.6 * math.exp(-0.3 * l)
        lamv = jnp.stack([lam_q1[l], lam_k1[l], lam_q2[l], lam_k2[l]])
        lw = {
            "g_attn": g_attn[l].reshape(1, d),
            "g_ffn": g_ffn[l].reshape(1, d),
            "w_in": w_in[l].astype(_BF),
            "w_out": w_out[l].astype(_BF),
            "pk_wq_t": pk_wq[l].T.astype(_BF),
            "pk_keys": pk_keys[l].reshape(2 * PK_HEADS, N_KEYS, PK_HALF).astype(_BF),
            "pk_u": pk_u[l].astype(_BF),
            "pk_v_t": pk_v[l].astype(_BF).reshape(-1, PEER_EXPERTS, d).transpose(0, 2, 1),
        }
        gain_a = a_gain[l].reshape(1, 2 * A_DIM)
        gain_b = b_gain[l].reshape(1, b_width)
        mod_p = tuple(m.reshape(batch, 1, d) for m in jnp.split(mod[l, :batch], 6, axis=-1))
        mod_s = tuple(jnp.repeat(m, t_new, axis=0) for m in jnp.split(mod[l, batch:batch + nb], 6, axis=-1))
        last = l == depth - 1

        def mix_p(qa, kaf, vaf, kab, vab, qb, kbf, vbf, kbb, vbb):
            oa = _attn_a_prompt(qa, kab, vab, lamv, gain_a, batch=batch, seq=seq, lam_init=lam_init)
            ob = _attn_b_prompt(qb, kbb, vbb, rel_bias[l], gain_b, batch=batch, seq=seq)
            return oa, ob, ()

        def mix_s(qa, kaf, vaf, kab, vab, qb, kbf, vbf, kbb, vbb):
            oa = _attn_a_sample(qa, cak, cav, kaf, vaf, lamv, gain_a, layer=l, lam_init=lam_init, t_new=t_new)
            ob, nbk, nbv = _attn_b_sample(qb, cbk, cbv, kbf, vbf, rel_bias[l], gain_b, layer=l, t_new=t_new)
            return oa, ob, (nbk, nbv)

        xp, sp = _stream_layer(xp, mod_p, lw, tabs_p, mix_p, per_row=False, seq=seq,
                               final_norm=last, g_final=g_fin)
        xs, ss = _stream_layer(xs, mod_s, lw, tabs_s, mix_s, per_row=True, seq=t_new,
                               final_norm=last, g_final=g_fin)
        st_p.append(sp)
        st_s.append(ss)

    keep = min(B_REACH, seq)
    a_shape_p = (batch, seq, A_HEADS, 2 * A_DIM)
    b_tail = lambda s: s.reshape(batch, seq, B_HEADS, B_DIM)[:, seq - keep:]
    a_shape_s = (nb, t_new, A_HEADS, 2 * A_DIM)
    b_shape_s = (nb, wb, B_HEADS, B_DIM)
    return (
        xp.reshape(batch, seq, d),
        xs.reshape(nb, t_new, d),
        jnp.stack([s[0].reshape(a_shape_p) for s in st_p]),
        jnp.stack([s[1].reshape(a_shape_p) for s in st_p]),
        jnp.stack([b_tail(s[2]) for s in st_p]),
        jnp.stack([b_tail(s[3]) for s in st_p]),
        jnp.stack([s[0].reshape(a_shape_s) for s in st_s]),
        jnp.stack([s[1].reshape(a_shape_s) for s in st_s]),
        jnp.stack([s[4].reshape(b_shape_s) for s in st_s]),
        jnp.stack([s[5].reshape(b_shape_s) for s in st_s]),
    )
```

```python
import functools
import math

import jax
import jax.numpy as jnp
import numpy as np
from jax import lax
from jax.experimental import pallas as pl
from jax.experimental.pallas import tpu as pltpu

_F32 = jnp.float32
_BF = jnp.bfloat16

CHUNK = 64
A_HEADS = 4
A_DIM = 64
ROT_DIM = A_DIM // 4
ROPE_THETA = 500000.0
B_HEADS = 8
B_DIM = 64
LEFT_CHUNKS = 8
B_REACH = LEFT_CHUNKS * CHUNK
REL_CLIP = 128
PK_HEADS = 8
N_KEYS = 128
PK_TOPK = 16
PK_HALF = 128
EPS = 1e-6

LANES = 128
MASKED = -1e30
VMEM_LIMIT = 56 * 1024 * 1024

A_TILE = 512
B_TILE = 256
PEER_TOKENS = 1024
ROUTE_TOKENS = 512
PEER_EXPERTS = 1024
PEER_CHUNK = 256
GATE_ROWS = 64
ROW_TILE = 512


def _params(*sem):
    return pltpu.CompilerParams(dimension_semantics=sem, vmem_limit_bytes=VMEM_LIMIT)


def _resident(shape, index_map):
    return pl.BlockSpec(shape, index_map, pipeline_mode=pl.Buffered(1))


def _rms(x):
    return x * lax.rsqrt(jnp.mean(x * x, axis=-1, keepdims=True) + EPS)


def _dot(a, b):
    return jnp.dot(a, b, preferred_element_type=_F32)


def _dot_nt(a, b):
    return lax.dot_general(a, b, (((1,), (1,)), ((), ())), preferred_element_type=_F32)


def _mod_kernel(c_ref, w_ref, b_ref, o_ref):
    c = c_ref[...]
    s = c * (1.0 / (1.0 + jnp.exp(-c)))
    o_ref[...] = _dot(s.astype(_BF), w_ref[...].astype(_BF)) + b_ref[...]


def _modulation(c_all, w_ada, b_ada):
    depth, d, n = w_ada.shape
    rows = c_all.shape[0]
    tn = n // 4
    return pl.pallas_call(
        _mod_kernel,
        grid=(depth, n // tn),
        in_specs=[
            pl.BlockSpec((rows, d), lambda l, j: (0, 0)),
            pl.BlockSpec((None, d, tn), lambda l, j: (l, 0, j)),
            pl.BlockSpec((None, 1, tn), lambda l, j: (l, 0, j)),
        ],
        out_specs=pl.BlockSpec((None, rows, tn), lambda l, j: (l, 0, j)),
        out_shape=jax.ShapeDtypeStruct((depth, rows, n), _F32),
        compiler_params=_params("arbitrary", "arbitrary"),
        name="adaln_mod",
    )(c_all, w_ada, b_ada.reshape(depth, 1, n))


def _inproj_kernel(x_ref, sh_ref, sc_ref, g_ref, w_ref, cos_ref, sa_ref, sb_ref,
                   qa_ref, kaf_ref, vaf_ref, kab_ref, vab_ref,
                   qb_ref, kbf_ref, vbf_ref, kbb_ref, vbb_ref, *, width):
    h = _rms(x_ref[...]) * g_ref[...] * (1.0 + sc_ref[...]) + sh_ref[...]
    hb = h.astype(_BF)
    reps = width // LANES
    cos = jnp.tile(cos_ref[...], (1, reps))
    sa = jnp.tile(sa_ref[...], (1, reps))
    sb = jnp.tile(sb_ref[...], (1, reps))

    def rope(z):
        return z * cos + pltpu.roll(z, width - ROT_DIM // 2, 1) * sa + pltpu.roll(z, ROT_DIM // 2, 1) * sb

    def proj(i):
        return _dot(hb, w_ref[:, i * width:(i + 1) * width])

    qa_ref[...] = (rope(proj(0)) * (A_DIM ** -0.5)).astype(_BF)
    ka = rope(proj(1))
    kaf_ref[...] = ka
    kab_ref[...] = ka.astype(_BF)
    va = proj(2)
    vaf_ref[...] = va
    vab_ref[...] = va.astype(_BF)
    qb_ref[...] = (proj(3) * (B_DIM ** -0.5)).astype(_BF)
    kb = proj(4)
    kbf_ref[...] = kb
    kbb_ref[...] = kb.astype(_BF)
    vb = proj(5)
    vbf_ref[...] = vb
    vbb_ref[...] = vb.astype(_BF)


def _row_mod_specs(per_row, tm, d, tiles_per_batch):
    if per_row:
        return pl.BlockSpec((tm, d), lambda t: (t, 0))
    return pl.BlockSpec((None, 1, d), lambda t: (t // tiles_per_batch, 0, 0))


def _in_projection(x, shift, scale, gain, w_in_bf, rope_tabs, *, per_row, seq):
    t_all, d = x.shape
    width = w_in_bf.shape[1] // 6
    tm = min(ROW_TILE, t_all)
    tiles_per_batch = max(seq // tm, 1)
    tab_tiles = rope_tabs[0].shape[0] // tm
    mod_spec = _row_mod_specs(per_row, tm, d, tiles_per_batch)
    tab_spec = pl.BlockSpec((tm, LANES), lambda t: (t % tab_tiles, 0))
    row = lambda dt: jax.ShapeDtypeStruct((t_all, width), dt)
    out_spec = pl.BlockSpec((tm, width), lambda t: (t, 0))
    dts = [_BF, _F32, _F32, _BF, _BF, _BF, _F32, _F32, _BF, _BF]
    return pl.pallas_call(
        functools.partial(_inproj_kernel, width=width),
        grid=(t_all // tm,),
        in_specs=[
            pl.BlockSpec((tm, d), lambda t: (t, 0)),
            mod_spec, mod_spec,
            _resident((1, d), lambda t: (0, 0)),
            _resident(w_in_bf.shape, lambda t: (0, 0)),
            tab_spec, tab_spec, tab_spec,
        ],
        out_specs=[out_spec] * 10,
        out_shape=[row(dt) for dt in dts],
        compiler_params=_params("arbitrary"),
        name="in_projection",
    )(x, shift, scale, gain, w_in_bf, *rope_tabs)


def _diff_lambda(lamv_ref, lam_init):
    v = lamv_ref[...]
    d1 = jnp.sum(v[0:1] * v[1:2], axis=1, keepdims=True)
    d2 = jnp.sum(v[2:3] * v[3:4], axis=1, keepdims=True)
    return jnp.exp(d1) - jnp.exp(d2) + lam_init


def _split_maps(q):
    lane = lax.broadcasted_iota(jnp.int32, q.shape, 1)
    zero = jnp.zeros_like(q)
    return jnp.where(lane < A_DIM, q, zero), jnp.where(lane >= A_DIM, q, zero)


def _attn_a_kernel(lamv_ref, q_ref, k_ref, v_ref, gain_ref, o_ref,
                   sa, sb, m1, l1, acc1, m2, l2, acc2, *, lam_init):
    n_full = pl.program_id(2)
    tq = q_ref.shape[0]
    q1, q2 = _split_maps(q_ref[...])
    for m_scr, l_scr, acc in ((m1, l1, acc1), (m2, l2, acc2)):
        m_scr[...] = jnp.full(m_scr.shape, MASKED, _F32)
        l_scr[...] = jnp.zeros(l_scr.shape, _F32)
        acc[...] = jnp.zeros(acc.shape, _F32)

    def rows(j):
        return pl.ds(pl.multiple_of(j * tq, tq), tq)

    def scores(j, s_scr):
        kb = k_ref[rows(j), :]
        s_scr[0] = _dot_nt(q1, kb)
        s_scr[1] = _dot_nt(q2, kb)

    def update(s, m_scr, l_scr, acc, vb):
        m_prev = m_scr[...]
        m_next = jnp.maximum(m_prev, jnp.max(s, axis=1, keepdims=True))
        p = jnp.exp(s - jnp.tile(m_next, (1, tq // LANES)))
        alpha = jnp.exp(m_prev - m_next)
        part = p[:, 0:LANES]
        for c in range(1, tq // LANES):
            part = part + p[:, c * LANES:(c + 1) * LANES]
        l_scr[...] = alpha * l_scr[...] + part
        acc[...] = acc[...] * alpha + _dot(p.astype(_BF), vb)
        m_scr[...] = m_next

    def process(j, s_scr, diagonal):
        vb = v_ref[rows(j), :]
        for mp, (m_scr, l_scr, acc) in enumerate(((m1, l1, acc1), (m2, l2, acc2))):
            s = s_scr[mp]
            if diagonal:
                qc = lax.broadcasted_iota(jnp.int32, s.shape, 0) // CHUNK
                kc = lax.broadcasted_iota(jnp.int32, s.shape, 1) // CHUNK
                s = jnp.where(kc <= qc, s, MASKED)
            update(s, m_scr, l_scr, acc, vb)

    scores(0, sa)

    def pair(i, carry):
        scores(2 * i + 1, sb)
        process(2 * i, sa, False)
        scores(2 * i + 2, sa)
        process(2 * i + 1, sb, False)
        return carry

    lax.fori_loop(0, n_full // 2, pair, 0)

    @pl.when(n_full % 2 == 1)
    def _():
        scores(n_full, sb)
        process(n_full - 1, sa, False)
        process(n_full, sb, True)

    @pl.when(n_full % 2 == 0)
    def _():
        process(n_full, sa, True)

    lam = _diff_lambda(lamv_ref, lam_init)
    d1 = jnp.sum(l1[...], axis=1, keepdims=True)
    d2 = jnp.sum(l2[...], axis=1, keepdims=True)
    o = acc1[...] / d1 - lam * (acc2[...] / d2)
    o_ref[...] = (_rms(o) * gain_ref[...] * (1.0 - lam_init)).astype(_BF)


def _attn_a_prompt(qa, ka, va, lamv, a_gain, *, batch, seq, lam_init):
    t_all, width = qa.shape
    hd = 2 * A_DIM
    tq = A_TILE
    nq = seq // tq
    return pl.pallas_call(
        functools.partial(_attn_a_kernel, lam_init=lam_init),
        grid=(batch, A_HEADS, nq),
        in_specs=[
            _resident(lamv.shape, lambda b, h, i: (0, 0)),
            pl.BlockSpec((tq, hd), lambda b, h, i: (b * nq + i, h)),
            pl.BlockSpec((seq, hd), lambda b, h, i: (b, h)),
            pl.BlockSpec((seq, hd), lambda b, h, i: (b, h)),
            _resident((1, hd), lambda b, h, i: (0, 0)),
        ],
        out_specs=pl.BlockSpec((tq, hd), lambda b, h, i: (b * nq + i, h)),
        out_shape=jax.ShapeDtypeStruct((t_all, width), _BF),
        scratch_shapes=[pltpu.VMEM((2, tq, tq), _F32)] * 2 + [pltpu.VMEM((tq, hd), _F32)] * 6,
        compiler_params=_params("arbitrary", "arbitrary", "arbitrary"),
        name="diff_attn_prompt",
    )(lamv, qa, ka, va, a_gain)


def _head_lanes(shape, head, dim):
    lane = lax.broadcasted_iota(jnp.int32, shape, 1)
    return (lane >= head * dim) & (lane < (head + 1) * dim)


def _attn_b_kernel(q_ref, k0_ref, k1_ref, k2_ref, v0_ref, v1_ref, v2_ref, bias_ref, gain_ref, o_ref):
    qi = pl.program_id(1)
    tq = q_ref.shape[0]
    k_refs = (k0_ref, k1_ref, k2_ref)
    v_refs = (v0_ref, v1_ref, v2_ref)
    pen = (jnp.where(qi >= 2, 0.0, MASKED), jnp.where(qi >= 1, 0.0, MASKED), 0.0)
    outs = []
    for pair in range(B_HEADS // 2):
        cols = slice(pair * LANES, (pair + 1) * LANES)
        qp = q_ref[:, cols]
        zero = jnp.zeros_like(qp)
        halves = []
        for sub in range(2):
            head = 2 * pair + sub
            qh = jnp.where(_head_lanes(qp.shape, sub, B_DIM), qp, zero)
            s = [_dot_nt(qh, k_refs[b][:, cols]) + bias_ref[head, :, b * tq:(b + 1) * tq] + pen[b]
                 for b in range(3)]
            m = jnp.maximum(jnp.maximum(jnp.max(s[0], axis=1, keepdims=True),
                                        jnp.max(s[1], axis=1, keepdims=True)),
                            jnp.max(s[2], axis=1, keepdims=True))
            p = [jnp.exp(sb - m) for sb in s]
            l = (jnp.sum(p[0], axis=1, keepdims=True) + jnp.sum(p[1], axis=1, keepdims=True)
                 + jnp.sum(p[2], axis=1, keepdims=True))
            o = (_dot(p[0].astype(_BF), v_refs[0][:, cols]) + _dot(p[1].astype(_BF), v_refs[1][:, cols])
                 + _dot(p[2].astype(_BF), v_refs[2][:, cols]))
            halves.append(o / l)
        outs.append(jnp.where(_head_lanes(halves[0].shape, 0, B_DIM), halves[0], halves[1]))
    o = jnp.concatenate(outs, axis=1)
    o_ref[...] = (_rms(o) * gain_ref[...]).astype(_BF)


def _band_bias(table, tq):
    nk = B_REACH + tq
    qpos = np.arange(tq)[:, None]
    kpos = np.arange(nk)[None, :] - B_REACH
    qc = qpos // CHUNK
    kc = np.floor_divide(kpos, CHUNK)
    valid = (kc <= qc) & (kc >= qc - LEFT_CHUNKS)
    span = nk + tq - 1
    dist = (span - 1 - np.arange(span)) - (tq - 1)
    line = table[:, np.clip(dist, -REL_CLIP, REL_CLIP) + REL_CLIP].astype(_F32)
    line = jnp.pad(line, ((0, 0), (0, 1)))
    skew = jnp.tile(line, (1, tq))[:, :tq * span].reshape(-1, tq, span)
    bias = skew[:, :, tq - 1:]
    return jnp.where(valid[None], bias, MASKED)


def _attn_b_prompt(qb, kb, vb, rel_table, b_gain, *, batch, seq):
    t_all, width = qb.shape
    tq = B_TILE
    assert B_REACH == 2 * tq
    nq = seq // tq
    bias = _band_bias(rel_table, tq)
    qmap = lambda b, i: (b * nq + i, 0)
    kmap = lambda back: (lambda b, i: (b * nq + jnp.maximum(i - back, 0), 0))
    blk = lambda f: pl.BlockSpec((tq, width), f)
    return pl.pallas_call(
        _attn_b_kernel,
        grid=(batch, nq),
        in_specs=[blk(qmap), blk(kmap(2)), blk(kmap(1)), blk(kmap(0)),
                  blk(kmap(2)), blk(kmap(1)), blk(kmap(0)),
                  _resident(bias.shape, lambda b, i: (0, 0, 0)),
                  _resident((1, width), lambda b, i: (0, 0))],
        out_specs=blk(qmap),
        out_shape=jax.ShapeDtypeStruct((t_all, width), _BF),
        compiler_params=_params("arbitrary", "arbitrary"),
        name="band_attn_prompt",
    )(qb, kb, kb, kb, vb, vb, vb, bias, b_gain)


def _attn_a_sample_kernel(lamv_ref, q_ref, kc_ref, vc_ref, kn_ref, vn_ref, gain_ref, o_ref, *, lam_init):
    q = q_ref[...]
    t = q.shape[0]
    hd = 2 * A_DIM
    past = kc_ref.shape[0] // A_HEADS
    lam = _diff_lambda(lamv_ref, lam_init)
    outs = []
    for h in range(A_HEADS):
        cols = slice(h * hd, (h + 1) * hd)
        head_rows = pl.ds(h, past, stride=A_HEADS)
        q12 = jnp.concatenate(_split_maps(q[:, cols]), axis=0)
        kn = kn_ref[:, cols].astype(_BF)
        vn = vn_ref[:, cols].astype(_BF)
        s_c = _dot_nt(q12, kc_ref[head_rows, :].astype(_BF))
        s_n = _dot_nt(q12, kn)
        m = jnp.maximum(jnp.max(s_c, axis=1, keepdims=True), jnp.max(s_n, axis=1, keepdims=True))
        p_c = jnp.exp(s_c - m)
        p_n = jnp.exp(s_n - m)
        l = jnp.sum(p_c, axis=1, keepdims=True) + jnp.sum(p_n, axis=1, keepdims=True)
        p_c = p_c / l
        p_n = p_n / l
        a_c = p_c[0:t] - lam * p_c[t:2 * t]
        a_n = p_n[0:t] - lam * p_n[t:2 * t]
        o = _dot(a_c.astype(_BF), vc_ref[head_rows, :].astype(_BF)) + _dot(a_n.astype(_BF), vn)
        outs.append(_rms(o) * gain_ref[...] * (1.0 - lam_init))
    o_ref[...] = jnp.concatenate(outs, axis=1).astype(_BF)


def _attn_a_sample(qa, cache_k, cache_v, kn, vn, lamv, a_gain, *, layer, lam_init, t_new):
    t_all, width = qa.shape
    _, nb, cache_rows, hd = cache_k.shape
    tok = pl.BlockSpec((t_new, width), lambda b: (b, 0))
    cache = pl.BlockSpec((None, None, cache_rows, hd), lambda b: (layer, b, 0, 0))
    return pl.pallas_call(
        functools.partial(_attn_a_sample_kernel, lam_init=lam_init),
        grid=(nb,),
        in_specs=[_resident(lamv.shape, lambda b: (0, 0)), tok, cache, cache, tok, tok,
                  _resident((1, 2 * A_DIM), lambda b: (0, 0))],
        out_specs=tok,
        out_shape=jax.ShapeDtypeStruct((t_all, width), _BF),
        compiler_params=_params("arbitrary"),
        name="diff_attn_sample",
    )(lamv, qa, cache_k, cache_v, kn, vn, a_gain)


def _attn_b_sample_kernel(q_ref, kc_ref, vc_ref, kn_ref, vn_ref, bc_ref, bn_ref, gain_ref,
                          o_ref, ko_ref, vo_ref):
    q = q_ref[...]
    t = q.shape[0]
    keep = kc_ref.shape[0] - t
    zero = jnp.zeros_like(q)
    qall = jnp.concatenate(
        [jnp.where(_head_lanes(q.shape, h, B_DIM), q, zero) for h in range(B_HEADS)], axis=0)
    kc = kc_ref[...]
    vc = vc_ref[...]
    kn = kn_ref[...]
    vn = vn_ref[...]
    s_c = _dot_nt(qall, kc.astype(_BF)) + bc_ref[...]
    s_n = _dot_nt(qall, kn.astype(_BF)) + bn_ref[...]
    m = jnp.maximum(jnp.max(s_c, axis=1, keepdims=True), jnp.max(s_n, axis=1, keepdims=True))
    p_c = jnp.exp(s_c - m)
    p_n = jnp.exp(s_n - m)
    l = jnp.sum(p_c, axis=1, keepdims=True) + jnp.sum(p_n, axis=1, keepdims=True)
    o_all = (_dot(p_c.astype(_BF), vc.astype(_BF)) + _dot(p_n.astype(_BF), vn.astype(_BF))) / l
    o = jnp.zeros((t, q.shape[1]), _F32)
    for h in range(B_HEADS):
        o = jnp.where(_head_lanes(o.shape, h, B_DIM), o_all[h * t:(h + 1) * t], o)
    o_ref[...] = (_rms(o) * gain_ref[...]).astype(_BF)
    ko_ref[0:keep, :] = kc[t:, :]
    ko_ref[keep:, :] = kn
    vo_ref[0:keep, :] = vc[t:, :]
    vo_ref[keep:, :] = vn


def _sample_bias(table, t_new, wb):
    qpos = jnp.arange(t_new)[:, None] + wb
    kpos = jnp.arange(wb + t_new)[None, :]
    d = jnp.clip(qpos - kpos, -REL_CLIP, REL_CLIP) + REL_CLIP
    bias = table[:, d].astype(_F32).reshape(B_HEADS * t_new, wb + t_new)
    return bias[:, :wb], bias[:, wb:]


def _attn_b_sample(qb, cache_k, cache_v, kn, vn, rel_table, b_gain, *, layer, t_new):
    t_all, width = qb.shape
    _, nb, wb, _ = cache_k.shape
    bias_c, bias_n = _sample_bias(rel_table, t_new, wb)
    tok = pl.BlockSpec((t_new, width), lambda b: (b, 0))
    cache = pl.BlockSpec((None, None, wb, width), lambda b: (layer, b, 0, 0))
    state = pl.BlockSpec((None, wb, width), lambda b: (b, 0, 0))
    return pl.pallas_call(
        _attn_b_sample_kernel,
        grid=(nb,),
        in_specs=[tok, cache, cache, tok, tok,
                  _resident(bias_c.shape, lambda b: (0, 0)), _resident(bias_n.shape, lambda b: (0, 0)),
                  _resident((1, width), lambda b: (0, 0))],
        out_specs=[tok, state, state],
        out_shape=[jax.ShapeDtypeStruct((t_all, width), _BF),
                   jax.ShapeDtypeStruct((nb, wb, width), _F32),
                   jax.ShapeDtypeStruct((nb, wb, width), _F32)],
        compiler_params=_params("arbitrary"),
        name="band_attn_sample",
    )(qb, cache_k, cache_v, kn, vn, bias_c, bias_n, b_gain)


def _outproj_kernel(x_ref, oa_ref, ob_ref, w_ref, gt_ref, sh_ref, sc_ref, g_ref, x1_ref, ht_ref):
    half = oa_ref.shape[1]
    y = _dot(oa_ref[...], w_ref[0:half, :]) + _dot(ob_ref[...], w_ref[half:, :])
    x1 = x_ref[...] + gt_ref[...] * y
    x1_ref[...] = x1
    h = _rms(x1) * g_ref[...] * (1.0 + sc_ref[...]) + sh_ref[...]
    ht_ref[...] = h.T.astype(_BF)


def _out_projection(x, oa, ob, w_out_bf, gate, shift, scale, gain, *, per_row, seq):
    t_all, d = x.shape
    half = oa.shape[1]
    tm = min(PEER_TOKENS, t_all)
    mod_spec = _row_mod_specs(per_row, tm, d, max(seq // tm, 1))
    rows = lambda w: pl.BlockSpec((tm, w), lambda t: (t, 0))
    return pl.pallas_call(
        _outproj_kernel,
        grid=(t_all // tm,),
        in_specs=[rows(d), rows(half), rows(half), _resident(w_out_bf.shape, lambda t: (0, 0)),
                  mod_spec, mod_spec, mod_spec, _resident((1, d), lambda t: (0, 0))],
        out_specs=[rows(d), pl.BlockSpec((None, d, tm), lambda t: (t, 0, 0))],
        out_shape=[jax.ShapeDtypeStruct((t_all, d), _F32),
                   jax.ShapeDtypeStruct((t_all // tm, d, tm), _BF)],
        compiler_params=_params("arbitrary"),
        name="out_projection",
    )(x, oa, ob, w_out_bf, gate, shift, scale, gain)


def _top_values(s, k, want_rank=False):
    vals = []
    rank = jnp.full(s.shape, float(k), _F32)
    for i in range(k):
        m = jnp.max(s, axis=0, keepdims=True)
        vals.append(m)
        hit = s == m
        if want_rank:
            rank = jnp.where(hit, float(i), rank)
        s = jnp.where(hit, -jnp.inf, s)
    return (vals, rank) if want_rank else vals


def _stack_rows(vals):
    rows = lax.broadcasted_iota(jnp.int32, (len(vals), LANES), 0)
    out = jnp.zeros((len(vals), LANES), _F32)
    for i, v in enumerate(vals):
        out = jnp.where(rows == i, v, out)
    return out


def _route_kernel(ht_ref, wq_ref, keys_ref, n1_ref, e1_ref, r2_ref, e2_ref, q_scr, s_scr):
    tt = ht_ref.shape[1]
    q_scr[...] = _dot(wq_ref[...], ht_ref[...]).astype(_BF)
    for hm in range(2 * PK_HEADS):
        s_scr[hm] = _dot(keys_ref[hm], q_scr[hm * PK_HALF:(hm + 1) * PK_HALF, :])

    def head(h, carry):
        for c in range(tt // LANES):
            cols = slice(c * LANES, (c + 1) * LANES)
            s1 = s_scr[2 * h, :, cols]
            s2 = s_scr[2 * h + 1, :, cols]
            a = _top_values(s1, PK_TOPK)
            b, rank2 = _top_values(s2, PK_TOPK, want_rank=True)
            amat = _stack_rows(a)
            bmat = _stack_rows(b)
            cand = [a[0] + bmat]
            cand += [a[p] + bmat[0:8] for p in range(1, 8)]
            cand += [amat[8:16] + b[0]]
            top = _top_values(jnp.concatenate(cand, axis=0), PK_TOPK)
            tau = top[PK_TOPK - 1]
            z = jnp.ones_like(tau)
            for k in range(1, PK_TOPK):
                z = z + jnp.exp(top[k] - top[0])
            count1 = jnp.zeros(s1.shape, _F32)
            for p in range(PK_TOPK):
                reach = jnp.sum(jnp.where(a[p] + bmat >= tau, 1.0, 0.0), axis=0, keepdims=True)
                count1 = jnp.where(s1 == a[p], reach, count1)
            n1_ref[h, :, cols] = count1
            e1_ref[h, :, cols] = jnp.exp(s1 - a[0]) * (0.5 / z)
            r2_ref[h, :, cols] = rank2.astype(_BF)
            e2_ref[h, :, cols] = jnp.exp(s2 - b[0]).astype(_BF)
        return carry

    lax.fori_loop(0, PK_HEADS, head, 0)


def _peer_route(ht, wq_t_bf, keys_bf):
    slabs, d, slab = ht.shape
    t_all = slabs * slab
    tt = min(ROUTE_TOKENS, slab)
    per_slab = slab // tt
    nq = wq_t_bf.shape[0]
    out = lambda dt: jax.ShapeDtypeStruct((PK_HEADS, N_KEYS, t_all), dt)
    tile = pl.BlockSpec((PK_HEADS, N_KEYS, tt), lambda t: (0, 0, t))
    return pl.pallas_call(
        _route_kernel,
        grid=(t_all // tt,),
        in_specs=[pl.BlockSpec((None, d, tt), lambda t: (t // per_slab, 0, t % per_slab)),
                  _resident(wq_t_bf.shape, lambda t: (0, 0)),
                  _resident(keys_bf.shape, lambda t: (0, 0, 0))],
        out_specs=[tile] * 4,
        out_shape=[out(_F32), out(_F32), out(_BF), out(_BF)],
        scratch_shapes=[pltpu.VMEM((nq, tt), _BF), pltpu.VMEM((2 * PK_HEADS, N_KEYS, tt), _F32)],
        compiler_params=_params("arbitrary"),
        name="peer_route",
    )(ht, wq_t_bf, keys_bf)


def _twice_gelu_tanh(x):
    return x * (1.0 + jnp.tanh(x * (0.7978845608028654 + (0.7978845608028654 * 0.044715) * (x * x))))


def _row_to_tile(row, rows):
    packed = jnp.broadcast_to(row, (16, LANES)).astype(_BF)
    return jnp.tile(packed, (rows // 16, 1))


def _peer_kernel(ht_ref, u_ref, vt_ref, n1_ref, e1_ref, r2_ref, e2_ref, x_ref, gt_ref, gf_ref,
                 o_ref, at_scr, wt_scr, acc, r2_scr, e2_scr, *, final_norm, probe=None):
    e = pl.program_id(1)
    tt = ht_ref.shape[1]
    groups = u_ref.shape[0] // N_KEYS
    span = at_scr.shape[2]
    n_chunks = tt // span

    @pl.when(e == 0)
    def _():
        acc[...] = jnp.zeros(acc.shape, _F32)
        r2_scr[...] = r2_ref[...]
        e2_scr[...] = e2_ref[...]

    for c in range(n_chunks):
        at_scr[c] = _dot(u_ref[...], ht_ref[:, c * span:(c + 1) * span])
    zero = jnp.zeros((N_KEYS, LANES), _BF)

    def gate_block(c, cols, sub_cols):
        for il in range(groups):
            rows = slice(il * N_KEYS, (il + 1) * N_KEYS)
            g = None
            for h in range(PK_HEADS):
                if probe == "f32":
                    sel = jnp.where(r2_scr[h, :, cols].astype(_F32) < n1_ref[h, il:il + 1, cols],
                                    e2_scr[h, :, cols].astype(_F32), 0.0)
                    term = sel * e1_ref[h, il:il + 1, cols]
                else:
                    count = _row_to_tile(n1_ref[h, il:il + 1, cols], N_KEYS)
                    sel = jnp.where(r2_scr[h, :, cols] < count, e2_scr[h, :, cols], zero)
                    term = sel * _row_to_tile(e1_ref[h, il:il + 1, cols], N_KEYS)
                g = term if g is None else g + term
            act = _twice_gelu_tanh(at_scr[c, rows, sub_cols])
            if probe == "f32":
                wt_scr[c, rows, sub_cols] = (g * act).astype(_BF)
            else:
                wt_scr[c, rows, sub_cols] = g * act.astype(_BF)

    if probe == "serial":
        for c in range(n_chunks):
            def body(sub, carry, c=c):
                off = pl.multiple_of(sub * LANES, LANES)
                gate_block(c, pl.ds(c * span + off, LANES), pl.ds(off, LANES))
                return carry
            lax.fori_loop(0, span // LANES, body, 0)
        for c in range(n_chunks):
            acc[c] += _dot(vt_ref[...], wt_scr[c])
    else:
        for c in range(n_chunks):
            for sub in range(span // LANES):
                gate_block(c, slice(c * span + sub * LANES, c * span + (sub + 1) * LANES),
                           slice(sub * LANES, (sub + 1) * LANES))
            acc[c] += _dot(vt_ref[...], wt_scr[c])

    @pl.when(e == pl.num_programs(1) - 1)
    def _():
        f = jnp.concatenate([acc[c].T for c in range(n_chunks)], axis=0)
        y = x_ref[...] + gt_ref[...] * f
        if final_norm:
            y = _rms(y) * gf_ref[...]
        o_ref[...] = y


def _peer_experts(ht, u_bf, vt_bf, route, x1, gate, g_final, *, per_row, seq, final_norm):
    tiles, d, tt = ht.shape
    steps, _, eb = vt_bf.shape
    t_all = tiles * tt
    span = min(PEER_CHUNK, tt)
    groups = eb // N_KEYS
    assert groups == 8
    n1, e1, r2, e2 = route
    by_step = lambda a: a.reshape(PK_HEADS, N_KEYS // groups, groups, t_all)
    gate_spec = (pl.BlockSpec((tt, d), lambda t, e: (t, 0)) if per_row else
                 pl.BlockSpec((None, 1, d), lambda t, e: (t // max(seq // tt, 1), 0, 0)))
    step_rows = pl.BlockSpec((PK_HEADS, None, groups, tt), lambda t, e: (0, e, 0, t))
    tile = _resident((PK_HEADS, N_KEYS, tt), lambda t, e: (0, 0, t))
    return pl.pallas_call(
        functools.partial(_peer_kernel, final_norm=final_norm,
                          probe=(("serial" if final_norm else "f32") if per_row else None)),
        grid=(tiles, steps),
        in_specs=[pl.BlockSpec((None, d, tt), lambda t, e: (t, 0, 0)),
                  pl.BlockSpec((eb, d), lambda t, e: (e, 0)),
                  pl.BlockSpec((None, d, eb), lambda t, e: (e, 0, 0)),
                  step_rows, step_rows, tile, tile,
                  _resident((tt, d), lambda t, e: (t, 0)),
                  gate_spec,
                  _resident((1, d), lambda t, e: (0, 0))],
        out_specs=pl.BlockSpec((tt, d), lambda t, e: (t, 0)),
        out_shape=jax.ShapeDtypeStruct((t_all, d), _F32),
        scratch_shapes=[pltpu.VMEM((tt // span, eb, span), _F32), pltpu.VMEM((tt // span, eb, span), _BF),
                        pltpu.VMEM((tt // span, d, span), _F32),
                        pltpu.VMEM((PK_HEADS, N_KEYS, tt), _BF), pltpu.VMEM((PK_HEADS, N_KEYS, tt), _BF)],
        compiler_params=_params("arbitrary", "arbitrary"),
        name="peer_experts",
    )(ht, u_bf, vt_bf, by_step(n1), by_step(e1), r2, e2, x1, gate, g_final)


def _rope_tables(pos):
    half = ROT_DIM // 2
    inv = ROPE_THETA ** (-jnp.arange(0, ROT_DIM, 2, dtype=_F32) / ROT_DIM)
    ang = pos.astype(_F32)[:, None] * inv[None, :]
    cos, sin = jnp.cos(ang), jnp.sin(ang)
    n = pos.shape[0]
    pad = jnp.zeros((n, A_DIM - ROT_DIM), _F32)
    zeros = jnp.zeros((n, half), _F32)
    c = jnp.concatenate([cos, cos, pad + 1.0], axis=1)
    sa = jnp.concatenate([-sin, zeros, pad], axis=1)
    sb = jnp.concatenate([zeros, sin, pad], axis=1)
    return tuple(jnp.tile(t, (1, LANES // A_DIM)) for t in (c, sa, sb))


def _stream_layer(x, mod, lw, rope_tabs, mixer, *, per_row, seq, final_norm, g_final):
    shift1, scale1, gate1, shift2, scale2, gate2 = mod
    proj = _in_projection(x, shift1, scale1, lw["g_attn"], lw["w_in"], rope_tabs, per_row=per_row, seq=seq)
    qa, kaf, vaf, kab, vab, qb, kbf, vbf, kbb, vbb = proj
    oa, ob, extra = mixer(qa, kaf, vaf, kab, vab, qb, kbf, vbf, kbb, vbb)
    x1, ht = _out_projection(x, oa, ob, lw["w_out"], gate1, shift2, scale2, lw["g_ffn"],
                             per_row=per_row, seq=seq)
    route = _peer_route(ht, lw["pk_wq_t"], lw["pk_keys"])
    x2 = _peer_experts(ht, lw["pk_u"], lw["pk_v_t"], route, x1, gate2, g_final,
                       per_row=per_row, seq=seq, final_norm=final_norm)
    return x2, (kaf, vaf, kbf, vbf) + extra


def kernel(x_prompt, x_sample, c_prompt, c_sample, cache_a_k, cache_a_v, cache_b_k, cache_b_v, w_ada, b_ada, g_attn, g_ffn, w_in, lam_q1, lam_k1, lam_q2, lam_k2, a_gain, rel_bias, b_gain, w_out, pk_wq, pk_keys, pk_u, pk_v, g_final):
    batch, seq, d = x_prompt.shape
    nb, t_new, _ = x_sample.shape
    depth = w_ada.shape[0]
    past = cache_a_k.shape[2]
    wb = cache_b_k.shape[2]
    a_width = A_HEADS * 2 * A_DIM
    b_width = B_HEADS * B_DIM

    c_all = jnp.concatenate([c_prompt, c_sample], axis=0)
    pad_rows = (-c_all.shape[0]) % 16
    c_all = jnp.pad(c_all, ((0, pad_rows), (0, 0)))
    mod = _modulation(c_all, w_ada, b_ada)

    tabs_p = _rope_tables(jnp.arange(seq, dtype=jnp.int32))
    pos_s = past + jnp.arange(t_new, dtype=jnp.int32)
    tabs_s = tuple(jnp.tile(t, (nb, 1)) for t in _rope_tables(pos_s))

    cak = cache_a_k.reshape(depth, nb, past * A_HEADS, 2 * A_DIM)
    cav = cache_a_v.reshape(depth, nb, past * A_HEADS, 2 * A_DIM)
    cbk = cache_b_k.reshape(depth, nb, wb, b_width)
    cbv = cache_b_v.reshape(depth, nb, wb, b_width)

    xp = x_prompt.reshape(batch * seq, d)
    xs = x_sample.reshape(nb * t_new, d)
    g_fin = g_final.reshape(1, d)
    st_p, st_s = [], []
    for l in range(depth):
        lam_init = 0.8 - 0.6 * math.exp(-0.3 * l)
        lamv = jnp.stack([lam_q1[l], lam_k1[l], lam_q2[l], lam_k2[l]])
        lw = {
            "g_attn": g_attn[l].reshape(1, d),
            "g_ffn": g_ffn[l].reshape(1, d),
            "w_in": w_in[l].astype(_BF),
            "w_out": w_out[l].astype(_BF),
            "pk_wq_t": pk_wq[l].T.astype(_BF),
            "pk_keys": pk_keys[l].reshape(2 * PK_HEADS, N_KEYS, PK_HALF).astype(_BF),
            "pk_u": pk_u[l].astype(_BF),
            "pk_v_t": pk_v[l].astype(_BF).reshape(-1, PEER_EXPERTS, d).transpose(0, 2, 1),
        }
        gain_a = a_gain[l].reshape(1, 2 * A_DIM)
        gain_b = b_gain[l].reshape(1, b_width)
        mod_p = tuple(m.reshape(batch, 1, d) for m in jnp.split(mod[l, :batch], 6, axis=-1))
        mod_s = tuple(jnp.repeat(m, t_new, axis=0) for m in jnp.split(mod[l, batch:batch + nb], 6, axis=-1))
        last = l == depth - 1

        def mix_p(qa, kaf, vaf, kab, vab, qb, kbf, vbf, kbb, vbb):
            oa = _attn_a_prompt(qa, kab, vab, lamv, gain_a, batch=batch, seq=seq, lam_init=lam_init)
            ob = _attn_b_prompt(qb, kbb, vbb, rel_bias[l], gain_b, batch=batch, seq=seq)
            return oa, ob, ()

        def mix_s(qa, kaf, vaf, kab, vab, qb, kbf, vbf, kbb, vbb):
            oa = _attn_a_sample(qa, cak, cav, kaf, vaf, lamv, gain_a, layer=l, lam_init=lam_init, t_new=t_new)
            ob, nbk, nbv = _attn_b_sample(qb, cbk, cbv, kbf, vbf, rel_bias[l], gain_b, layer=l, t_new=t_new)
            return oa, ob, (nbk, nbv)

        xp, sp = _stream_layer(xp, mod_p, lw, tabs_p, mix_p, per_row=False, seq=seq,
                               final_norm=last, g_final=g_fin)
        xs, ss = _stream_layer(xs, mod_s, lw, tabs_s, mix_s, per_row=True, seq=t_new,
                               final_norm=last, g_final=g_fin)
        st_p.append(sp)
        st_s.append(ss)

    keep = min(B_REACH, seq)
    a_shape_p = (batch, seq, A_HEADS, 2 * A_DIM)
    b_tail = lambda s: s.reshape(batch, seq, B_HEADS, B_DIM)[:, seq - keep:]
    a_shape_s = (nb, t_new, A_HEADS, 2 * A_DIM)
    b_shape_s = (nb, wb, B_HEADS, B_DIM)
    return (
        xp.reshape(batch, seq, d),
        xs.reshape(nb, t_new, d),
        jnp.stack([s[0].reshape(a_shape_p) for s in st_p]),
        jnp.stack([s[1].reshape(a_shape_p) for s in st_p]),
        jnp.stack([b_tail(s[2]) for s in st_p]),
        jnp.stack([b_tail(s[3]) for s in st_p]),
        jnp.stack([s[0].reshape(a_shape_s) for s in st_s]),
        jnp.stack([s[1].reshape(a_shape_s) for s in st_s]),
        jnp.stack([s[4].reshape(b_shape_s) for s in st_s]),
        jnp.stack([s[5].reshape(b_shape_s) for s in st_s]),
    )
```

```python
import functools
import math

import jax
import jax.numpy as jnp
import numpy as np
from jax import lax
from jax.experimental import pallas as pl
from jax.experimental.pallas import tpu as pltpu

_F32 = jnp.float32
_BF = jnp.bfloat16

CHUNK = 64
A_HEADS = 4
A_DIM = 64
ROT_DIM = A_DIM // 4
ROPE_THETA = 500000.0
B_HEADS = 8
B_DIM = 64
LEFT_CHUNKS = 8
B_REACH = LEFT_CHUNKS * CHUNK
REL_CLIP = 128
PK_HEADS = 8
N_KEYS = 128
PK_TOPK = 16
PK_HALF = 128
EPS = 1e-6

LANES = 128
MASKED = -1e30
VMEM_LIMIT = 56 * 1024 * 1024

A_TILE = 512
B_TILE = 256
PEER_TOKENS = 1024
ROUTE_TOKENS = 512
PEER_EXPERTS = 1024
PEER_CHUNK = 256
ROW_TILE = 512


def _params(*sem):
    return pltpu.CompilerParams(dimension_semantics=sem, vmem_limit_bytes=VMEM_LIMIT)


def _resident(shape, index_map):
    return pl.BlockSpec(shape, index_map, pipeline_mode=pl.Buffered(1))


def _rms(x):
    return x * lax.rsqrt(jnp.mean(x * x, axis=-1, keepdims=True) + EPS)


def _dot(a, b):
    return jnp.dot(a, b, preferred_element_type=_F32)


def _dot_nt(a, b):
    return lax.dot_general(a, b, (((1,), (1,)), ((), ())), preferred_element_type=_F32)


def _mod_kernel(c_ref, w_ref, b_ref, o_ref):
    c = c_ref[...]
    s = c * (1.0 / (1.0 + jnp.exp(-c)))
    o_ref[...] = _dot(s.astype(_BF), w_ref[...].astype(_BF)) + b_ref[...]


def _modulation(c_all, w_ada, b_ada):
    depth, d, n = w_ada.shape
    rows = c_all.shape[0]
    tn = n // 4
    return pl.pallas_call(
        _mod_kernel,
        grid=(depth, n // tn),
        in_specs=[
            pl.BlockSpec((rows, d), lambda l, j: (0, 0)),
            pl.BlockSpec((None, d, tn), lambda l, j: (l, 0, j)),
            pl.BlockSpec((None, 1, tn), lambda l, j: (l, 0, j)),
        ],
        out_specs=pl.BlockSpec((None, rows, tn), lambda l, j: (l, 0, j)),
        out_shape=jax.ShapeDtypeStruct((depth, rows, n), _F32),
        compiler_params=_params("arbitrary", "arbitrary"),
        name="adaln_mod",
    )(c_all, w_ada, b_ada.reshape(depth, 1, n))


def _inproj_kernel(x_ref, sh_ref, sc_ref, g_ref, w_ref, cos_ref, sa_ref, sb_ref,
                   qa_ref, kaf_ref, vaf_ref, kab_ref, vab_ref,
                   qb_ref, kbf_ref, vbf_ref, kbb_ref, vbb_ref, *, width):
    h = _rms(x_ref[...]) * g_ref[...] * (1.0 + sc_ref[...]) + sh_ref[...]
    hb = h.astype(_BF)
    reps = width // LANES
    cos = jnp.tile(cos_ref[...], (1, reps))
    sa = jnp.tile(sa_ref[...], (1, reps))
    sb = jnp.tile(sb_ref[...], (1, reps))

    def rope(z):
        return z * cos + pltpu.roll(z, width - ROT_DIM // 2, 1) * sa + pltpu.roll(z, ROT_DIM // 2, 1) * sb

    def proj(i):
        return _dot(hb, w_ref[:, i * width:(i + 1) * width])

    qa_ref[...] = (rope(proj(0)) * (A_DIM ** -0.5)).astype(_BF)
    ka = rope(proj(1))
    kaf_ref[...] = ka
    kab_ref[...] = ka.astype(_BF)
    va = proj(2)
    vaf_ref[...] = va
    vab_ref[...] = va.astype(_BF)
    qb_ref[...] = (proj(3) * (B_DIM ** -0.5)).astype(_BF)
    kb = proj(4)
    kbf_ref[...] = kb
    kbb_ref[...] = kb.astype(_BF)
    vb = proj(5)
    vbf_ref[...] = vb
    vbb_ref[...] = vb.astype(_BF)


def _row_mod_specs(per_row, tm, d, tiles_per_batch):
    if per_row:
        return pl.BlockSpec((tm, d), lambda t: (t, 0))
    return pl.BlockSpec((None, 1, d), lambda t: (t // tiles_per_batch, 0, 0))


def _in_projection(x, shift, scale, gain, w_in_bf, rope_tabs, *, per_row, seq):
    t_all, d = x.shape
    width = w_in_bf.shape[1] // 6
    tm = min(ROW_TILE, t_all)
    tiles_per_batch = max(seq // tm, 1)
    tab_tiles = rope_tabs[0].shape[0] // tm
    mod_spec = _row_mod_specs(per_row, tm, d, tiles_per_batch)
    tab_spec = pl.BlockSpec((tm, LANES), lambda t: (t % tab_tiles, 0))
    row = lambda dt: jax.ShapeDtypeStruct((t_all, width), dt)
    out_spec = pl.BlockSpec((tm, width), lambda t: (t, 0))
    dts = [_BF, _F32, _F32, _BF, _BF, _BF, _F32, _F32, _BF, _BF]
    return pl.pallas_call(
        functools.partial(_inproj_kernel, width=width),
        grid=(t_all // tm,),
        in_specs=[
            pl.BlockSpec((tm, d), lambda t: (t, 0)),
            mod_spec, mod_spec,
            _resident((1, d), lambda t: (0, 0)),
            _resident(w_in_bf.shape, lambda t: (0, 0)),
            tab_spec, tab_spec, tab_spec,
        ],
        out_specs=[out_spec] * 10,
        out_shape=[row(dt) for dt in dts],
        compiler_params=_params("arbitrary"),
        name="in_projection",
    )(x, shift, scale, gain, w_in_bf, *rope_tabs)


def _diff_lambda(lamv_ref, lam_init):
    v = lamv_ref[...]
    d1 = jnp.sum(v[0:1] * v[1:2], axis=1, keepdims=True)
    d2 = jnp.sum(v[2:3] * v[3:4], axis=1, keepdims=True)
    return jnp.exp(d1) - jnp.exp(d2) + lam_init


def _split_maps(q):
    lane = lax.broadcasted_iota(jnp.int32, q.shape, 1)
    zero = jnp.zeros_like(q)
    return jnp.where(lane < A_DIM, q, zero), jnp.where(lane >= A_DIM, q, zero)


def _attn_a_kernel(lamv_ref, q_ref, k_ref, v_ref, gain_ref, o_ref,
                   sa, sb, m1, l1, acc1, m2, l2, acc2, *, lam_init):
    n_full = pl.program_id(2)
    tq = q_ref.shape[0]
    q1, q2 = _split_maps(q_ref[...])
    for m_scr, l_scr, acc in ((m1, l1, acc1), (m2, l2, acc2)):
        m_scr[...] = jnp.full(m_scr.shape, MASKED, _F32)
        l_scr[...] = jnp.zeros(l_scr.shape, _F32)
        acc[...] = jnp.zeros(acc.shape, _F32)

    def rows(j):
        return pl.ds(pl.multiple_of(j * tq, tq), tq)

    def scores(j, s_scr):
        kb = k_ref[rows(j), :]
        s_scr[0] = _dot_nt(q1, kb)
        s_scr[1] = _dot_nt(q2, kb)

    def update(s, m_scr, l_scr, acc, vb):
        m_prev = m_scr[...]
        m_next = jnp.maximum(m_prev, jnp.max(s, axis=1, keepdims=True))
        p = jnp.exp(s - jnp.tile(m_next, (1, tq // LANES)))
        alpha = jnp.exp(m_prev - m_next)
        part = p[:, 0:LANES]
        for c in range(1, tq // LANES):
            part = part + p[:, c * LANES:(c + 1) * LANES]
        l_scr[...] = alpha * l_scr[...] + part
        acc[...] = acc[...] * alpha + _dot(p.astype(_BF), vb)
        m_scr[...] = m_next

    def process(j, s_scr, diagonal):
        vb = v_ref[rows(j), :]
        for mp, (m_scr, l_scr, acc) in enumerate(((m1, l1, acc1), (m2, l2, acc2))):
            s = s_scr[mp]
            if diagonal:
                qc = lax.broadcasted_iota(jnp.int32, s.shape, 0) // CHUNK
                kc = lax.broadcasted_iota(jnp.int32, s.shape, 1) // CHUNK
                s = jnp.where(kc <= qc, s, MASKED)
            update(s, m_scr, l_scr, acc, vb)

    scores(0, sa)

    def pair(i, carry):
        scores(2 * i + 1, sb)
        process(2 * i, sa, False)
        scores(2 * i + 2, sa)
        process(2 * i + 1, sb, False)
        return carry

    lax.fori_loop(0, n_full // 2, pair, 0)

    @pl.when(n_full % 2 == 1)
    def _():
        scores(n_full, sb)
        process(n_full - 1, sa, False)
        process(n_full, sb, True)

    @pl.when(n_full % 2 == 0)
    def _():
        process(n_full, sa, True)

    lam = _diff_lambda(lamv_ref, lam_init)
    d1 = jnp.sum(l1[...], axis=1, keepdims=True)
    d2 = jnp.sum(l2[...], axis=1, keepdims=True)
    o = acc1[...] / d1 - lam * (acc2[...] / d2)
    o_ref[...] = (_rms(o) * gain_ref[...] * (1.0 - lam_init)).astype(_BF)


def _attn_a_prompt(qa, ka, va, lamv, a_gain, *, batch, seq, lam_init):
    t_all, width = qa.shape
    hd = 2 * A_DIM
    tq = A_TILE
    nq = seq // tq
    return pl.pallas_call(
        functools.partial(_attn_a_kernel, lam_init=lam_init),
        grid=(batch, A_HEADS, nq),
        in_specs=[
            _resident(lamv.shape, lambda b, h, i: (0, 0)),
            pl.BlockSpec((tq, hd), lambda b, h, i: (b * nq + i, h)),
            pl.BlockSpec((seq, hd), lambda b, h, i: (b, h)),
            pl.BlockSpec((seq, hd), lambda b, h, i: (b, h)),
            _resident((1, hd), lambda b, h, i: (0, 0)),
        ],
        out_specs=pl.BlockSpec((tq, hd), lambda b, h, i: (b * nq + i, h)),
        out_shape=jax.ShapeDtypeStruct((t_all, width), _BF),
        scratch_shapes=[pltpu.VMEM((2, tq, tq), _F32)] * 2 + [pltpu.VMEM((tq, hd), _F32)] * 6,
        compiler_params=_params("arbitrary", "arbitrary", "arbitrary"),
        name="diff_attn_prompt",
    )(lamv, qa, ka, va, a_gain)


def _head_lanes(shape, head, dim):
    lane = lax.broadcasted_iota(jnp.int32, shape, 1)
    return (lane >= head * dim) & (lane < (head + 1) * dim)


def _attn_b_kernel(q_ref, k0_ref, k1_ref, k2_ref, v0_ref, v1_ref, v2_ref, bias_ref, gain_ref, o_ref):
    qi = pl.program_id(1)
    tq = q_ref.shape[0]
    k_refs = (k0_ref, k1_ref, k2_ref)
    v_refs = (v0_ref, v1_ref, v2_ref)
    pen = (jnp.where(qi >= 2, 0.0, MASKED), jnp.where(qi >= 1, 0.0, MASKED), 0.0)
    outs = []
    for pair in range(B_HEADS // 2):
        cols = slice(pair * LANES, (pair + 1) * LANES)
        qp = q_ref[:, cols]
        zero = jnp.zeros_like(qp)
        halves = []
        for sub in range(2):
            head = 2 * pair + sub
            qh = jnp.where(_head_lanes(qp.shape, sub, B_DIM), qp, zero)
            s = [_dot_nt(qh, k_refs[b][:, cols]) + bias_ref[head, :, b * tq:(b + 1) * tq] + pen[b]
                 for b in range(3)]
            m = jnp.maximum(jnp.maximum(jnp.max(s[0], axis=1, keepdims=True),
                                        jnp.max(s[1], axis=1, keepdims=True)),
                            jnp.max(s[2], axis=1, keepdims=True))
            p = [jnp.exp(sb - m) for sb in s]
            l = (jnp.sum(p[0], axis=1, keepdims=True) + jnp.sum(p[1], axis=1, keepdims=True)
                 + jnp.sum(p[2], axis=1, keepdims=True))
            o = (_dot(p[0].astype(_BF), v_refs[0][:, cols]) + _dot(p[1].astype(_BF), v_refs[1][:, cols])
                 + _dot(p[2].astype(_BF), v_refs[2][:, cols]))
            halves.append(o / l)
        outs.append(jnp.where(_head_lanes(halves[0].shape, 0, B_DIM), halves[0], halves[1]))
    o = jnp.concatenate(outs, axis=1)
    o_ref[...] = (_rms(o) * gain_ref[...]).astype(_BF)


def _band_bias(table, tq):
    nk = B_REACH + tq
    qpos = np.arange(tq)[:, None]
    kpos = np.arange(nk)[None, :] - B_REACH
    qc = qpos // CHUNK
    kc = np.floor_divide(kpos, CHUNK)
    valid = (kc <= qc) & (kc >= qc - LEFT_CHUNKS)
    span = nk + tq - 1
    dist = (span - 1 - np.arange(span)) - (tq - 1)
    line = table[:, np.clip(dist, -REL_CLIP, REL_CLIP) + REL_CLIP].astype(_F32)
    line = jnp.pad(line, ((0, 0), (0, 1)))
    skew = jnp.tile(line, (1, tq))[:, :tq * span].reshape(-1, tq, span)
    bias = skew[:, :, tq - 1:]
    return jnp.where(valid[None], bias, MASKED)


def _attn_b_prompt(qb, kb, vb, rel_table, b_gain, *, batch, seq):
    t_all, width = qb.shape
    tq = B_TILE
    assert B_REACH == 2 * tq
    nq = seq // tq
    bias = _band_bias(rel_table, tq)
    qmap = lambda b, i: (b * nq + i, 0)
    kmap = lambda back: (lambda b, i: (b * nq + jnp.maximum(i - back, 0), 0))
    blk = lambda f: pl.BlockSpec((tq, width), f)
    return pl.pallas_call(
        _attn_b_kernel,
        grid=(batch, nq),
        in_specs=[blk(qmap), blk(kmap(2)), blk(kmap(1)), blk(kmap(0)),
                  blk(kmap(2)), blk(kmap(1)), blk(kmap(0)),
                  _resident(bias.shape, lambda b, i: (0, 0, 0)),
                  _resident((1, width), lambda b, i: (0, 0))],
        out_specs=blk(qmap),
        out_shape=jax.ShapeDtypeStruct((t_all, width), _BF),
        compiler_params=_params("arbitrary", "arbitrary"),
        name="band_attn_prompt",
    )(qb, kb, kb, kb, vb, vb, vb, bias, b_gain)


def _attn_a_sample_kernel(lamv_ref, q_ref, kc_ref, vc_ref, kn_ref, vn_ref, gain_ref, o_ref, *, lam_init):
    q = q_ref[...]
    t = q.shape[0]
    hd = 2 * A_DIM
    past = kc_ref.shape[0] // A_HEADS
    lam = _diff_lambda(lamv_ref, lam_init)
    outs = []
    for h in range(A_HEADS):
        cols = slice(h * hd, (h + 1) * hd)
        head_rows = pl.ds(h, past, stride=A_HEADS)
        q12 = jnp.concatenate(_split_maps(q[:, cols]), axis=0)
        kn = kn_ref[:, cols].astype(_BF)
        vn = vn_ref[:, cols].astype(_BF)
        s_c = _dot_nt(q12, kc_ref[head_rows, :].astype(_BF))
        s_n = _dot_nt(q12, kn)
        m = jnp.maximum(jnp.max(s_c, axis=1, keepdims=True), jnp.max(s_n, axis=1, keepdims=True))
        p_c = jnp.exp(s_c - m)
        p_n = jnp.exp(s_n - m)
        l = jnp.sum(p_c, axis=1, keepdims=True) + jnp.sum(p_n, axis=1, keepdims=True)
        p_c = p_c / l
        p_n = p_n / l
        a_c = p_c[0:t] - lam * p_c[t:2 * t]
        a_n = p_n[0:t] - lam * p_n[t:2 * t]
        o = _dot(a_c.astype(_BF), vc_ref[head_rows, :].astype(_BF)) + _dot(a_n.astype(_BF), vn)
        outs.append(_rms(o) * gain_ref[...] * (1.0 - lam_init))
    o_ref[...] = jnp.concatenate(outs, axis=1).astype(_BF)


def _attn_a_sample(qa, cache_k, cache_v, kn, vn, lamv, a_gain, *, layer, lam_init, t_new):
    t_all, width = qa.shape
    _, nb, cache_rows, hd = cache_k.shape
    tok = pl.BlockSpec((t_new, width), lambda b: (b, 0))
    cache = pl.BlockSpec((None, None, cache_rows, hd), lambda b: (layer, b, 0, 0))
    return pl.pallas_call(
        functools.partial(_attn_a_sample_kernel, lam_init=lam_init),
        grid=(nb,),
        in_specs=[_resident(lamv.shape, lambda b: (0, 0)), tok, cache, cache, tok, tok,
                  _resident((1, 2 * A_DIM), lambda b: (0, 0))],
        out_specs=tok,
        out_shape=jax.ShapeDtypeStruct((t_all, width), _BF),
        compiler_params=_params("arbitrary"),
        name="diff_attn_sample",
    )(lamv, qa, cache_k, cache_v, kn, vn, a_gain)


def _attn_b_sample_kernel(q_ref, kc_ref, vc_ref, kn_ref, vn_ref, bc_ref, bn_ref, gain_ref,
                          o_ref, ko_ref, vo_ref):
    q = q_ref[...]
    t = q.shape[0]
    keep = kc_ref.shape[0] - t
    zero = jnp.zeros_like(q)
    qall = jnp.concatenate(
        [jnp.where(_head_lanes(q.shape, h, B_DIM), q, zero) for h in range(B_HEADS)], axis=0)
    kc = kc_ref[...]
    vc = vc_ref[...]
    kn = kn_ref[...]
    vn = vn_ref[...]
    s_c = _dot_nt(qall, kc.astype(_BF)) + bc_ref[...]
    s_n = _dot_nt(qall, kn.astype(_BF)) + bn_ref[...]
    m = jnp.maximum(jnp.max(s_c, axis=1, keepdims=True), jnp.max(s_n, axis=1, keepdims=True))
    p_c = jnp.exp(s_c - m)
    p_n = jnp.exp(s_n - m)
    l = jnp.sum(p_c, axis=1, keepdims=True) + jnp.sum(p_n, axis=1, keepdims=True)
    o_all = (_dot(p_c.astype(_BF), vc.astype(_BF)) + _dot(p_n.astype(_BF), vn.astype(_BF))) / l
    o = jnp.zeros((t, q.shape[1]), _F32)
    for h in range(B_HEADS):
        o = jnp.where(_head_lanes(o.shape, h, B_DIM), o_all[h * t:(h + 1) * t], o)
    o_ref[...] = (_rms(o) * gain_ref[...]).astype(_BF)
    ko_ref[0:keep, :] = kc[t:, :]
    ko_ref[keep:, :] = kn
    vo_ref[0:keep, :] = vc[t:, :]
    vo_ref[keep:, :] = vn


def _sample_bias(table, t_new, wb):
    qpos = jnp.arange(t_new)[:, None] + wb
    kpos = jnp.arange(wb + t_new)[None, :]
    d = jnp.clip(qpos - kpos, -REL_CLIP, REL_CLIP) + REL_CLIP
    bias = table[:, d].astype(_F32).reshape(B_HEADS * t_new, wb + t_new)
    return bias[:, :wb], bias[:, wb:]


def _attn_b_sample(qb, cache_k, cache_v, kn, vn, rel_table, b_gain, *, layer, t_new):
    t_all, width = qb.shape
    _, nb, wb, _ = cache_k.shape
    bias_c, bias_n = _sample_bias(rel_table, t_new, wb)
    tok = pl.BlockSpec((t_new, width), lambda b: (b, 0))
    cache = pl.BlockSpec((None, None, wb, width), lambda b: (layer, b, 0, 0))
    state = pl.BlockSpec((None, wb, width), lambda b: (b, 0, 0))
    return pl.pallas_call(
        _attn_b_sample_kernel,
        grid=(nb,),
        in_specs=[tok, cache, cache, tok, tok,
                  _resident(bias_c.shape, lambda b: (0, 0)), _resident(bias_n.shape, lambda b: (0, 0)),
                  _resident((1, width), lambda b: (0, 0))],
        out_specs=[tok, state, state],
        out_shape=[jax.ShapeDtypeStruct((t_all, width), _BF),
                   jax.ShapeDtypeStruct((nb, wb, width), _F32),
                   jax.ShapeDtypeStruct((nb, wb, width), _F32)],
        compiler_params=_params("arbitrary"),
        name="band_attn_sample",
    )(qb, cache_k, cache_v, kn, vn, bias_c, bias_n, b_gain)


def _outproj_kernel(x_ref, oa_ref, ob_ref, w_ref, gt_ref, sh_ref, sc_ref, g_ref, x1_ref, ht_ref):
    half = oa_ref.shape[1]
    y = _dot(oa_ref[...], w_ref[0:half, :]) + _dot(ob_ref[...], w_ref[half:, :])
    x1 = x_ref[...] + gt_ref[...] * y
    x1_ref[...] = x1
    h = _rms(x1) * g_ref[...] * (1.0 + sc_ref[...]) + sh_ref[...]
    ht_ref[...] = h.T.astype(_BF)


def _out_projection(x, oa, ob, w_out_bf, gate, shift, scale, gain, *, per_row, seq):
    t_all, d = x.shape
    half = oa.shape[1]
    tm = min(PEER_TOKENS, t_all)
    mod_spec = _row_mod_specs(per_row, tm, d, max(seq // tm, 1))
    rows = lambda w: pl.BlockSpec((tm, w), lambda t: (t, 0))
    return pl.pallas_call(
        _outproj_kernel,
        grid=(t_all // tm,),
        in_specs=[rows(d), rows(half), rows(half), _resident(w_out_bf.shape, lambda t: (0, 0)),
                  mod_spec, mod_spec, mod_spec, _resident((1, d), lambda t: (0, 0))],
        out_specs=[rows(d), pl.BlockSpec((None, d, tm), lambda t: (t, 0, 0))],
        out_shape=[jax.ShapeDtypeStruct((t_all, d), _F32),
                   jax.ShapeDtypeStruct((t_all // tm, d, tm), _BF)],
        compiler_params=_params("arbitrary"),
        name="out_projection",
    )(x, oa, ob, w_out_bf, gate, shift, scale, gain)


def _top_values(s, k, want_rank=False):
    vals = []
    rank = jnp.full(s.shape, float(k), _F32)
    for i in range(k):
        m = jnp.max(s, axis=0, keepdims=True)
        vals.append(m)
        hit = s == m
        if want_rank:
            rank = jnp.where(hit, float(i), rank)
        s = jnp.where(hit, -jnp.inf, s)
    return (vals, rank) if want_rank else vals


def _stack_rows(vals):
    rows = lax.broadcasted_iota(jnp.int32, (len(vals), LANES), 0)
    out = jnp.zeros((len(vals), LANES), _F32)
    for i, v in enumerate(vals):
        out = jnp.where(rows == i, v, out)
    return out


SUBLANES = 8


def _merge_sort_network(lo, hi):
    def merge(lo, hi, r):
        step = 2 * r
        if step < hi - lo:
            yield from merge(lo, hi, step)
            yield from merge(lo + r, hi, step)
            yield from ((i, i + r) for i in range(lo + r, hi - r, step))
        else:
            yield (lo, lo + r)

    if hi - lo >= 1:
        mid = lo + (hi - lo) // 2
        yield from _merge_sort_network(lo, mid)
        yield from _merge_sort_network(mid + 1, hi)
        yield from merge(lo, hi, 1)


def _compare_exchange(v, i, j):
    a, b = v[i], v[j]
    if b is None:
        return
    if a is None:
        v[i], v[j] = b, None
        return
    v[i], v[j] = jnp.maximum(a, b), jnp.minimum(a, b)


def _top16_replicated(tiles):
    n = PK_TOPK
    v = list(tiles) + [None] * (n - len(tiles))
    for i, j in _merge_sort_network(0, n - 1):
        _compare_exchange(v, i, j)
    shift = SUBLANES // 2
    while shift:
        w = [None if x is None else pltpu.roll(x, shift, 0) for x in v]
        c = []
        for k in range(n):
            a, b = v[k], w[n - 1 - k]
            c.append(b if a is None else a if b is None else jnp.maximum(a, b))
        stride = n // 2
        while stride:
            for i in range(n):
                if not i & stride:
                    _compare_exchange(c, i, i + stride)
            stride //= 2
        v = c
        shift //= 2
    return [jnp.full((SUBLANES, LANES), -jnp.inf, _F32) if x is None else x for x in v]


def _route_kernel(ht_ref, wq_ref, keys_ref, n1_ref, e1_ref, r2_ref, e2_ref, q_scr, s_scr):
    tt = ht_ref.shape[1]
    q_scr[...] = _dot(wq_ref[...], ht_ref[...]).astype(_BF)
    for hm in range(2 * PK_HEADS):
        s_scr[hm] = _dot(keys_ref[hm], q_scr[hm * PK_HALF:(hm + 1) * PK_HALF, :])

    def head(h, carry):
        for c in range(tt // LANES):
            cols = slice(c * LANES, (c + 1) * LANES)
            s1 = s_scr[2 * h, :, cols]
            s2 = s_scr[2 * h + 1, :, cols]
            a = _top16_replicated([s1[r:r + SUBLANES] for r in range(0, N_KEYS, SUBLANES)])
            b, rank2 = _top_values(s2, PK_TOPK, want_rank=True)
            bmat = _stack_rows(b)
            b_lo, b_hi = bmat[0:SUBLANES], bmat[SUBLANES:]
            cand = [a[0] + b_lo, a[0] + b_hi]
            cand += [a[p] + b_lo for p in range(1, 8)]
            cand += [_stack_rows([a[p][0:1] for p in range(8, PK_TOPK)]) + b[0]]
            top = _top16_replicated(cand)
            tau = top[PK_TOPK - 1]
            z = jnp.ones_like(tau)
            for k in range(1, PK_TOPK):
                z = z + jnp.exp(top[k] - top[0])
            z_row = z[0:1]
            count1 = jnp.zeros(s1.shape, _F32)
            for p in range(PK_TOPK):
                hit = jnp.where(a[p] + b_lo >= tau, 1.0, 0.0) + jnp.where(a[p] + b_hi >= tau, 1.0, 0.0)
                reach = jnp.sum(hit, axis=0, keepdims=True)
                count1 = jnp.where(s1 == jnp.tile(a[p], (N_KEYS // SUBLANES, 1)), reach, count1)
            n1_ref[h, :, cols] = count1
            e1_ref[h, :, cols] = jnp.exp(s1 - a[0][0:1]) * (0.5 / z_row)
            r2_ref[h, :, cols] = rank2.astype(_BF)
            e2_ref[h, :, cols] = jnp.exp(s2 - b[0]).astype(_BF)
        return carry

    lax.fori_loop(0, PK_HEADS, head, 0)


def _peer_route(ht, wq_t_bf, keys_bf):
    slabs, d, slab = ht.shape
    t_all = slabs * slab
    tt = min(ROUTE_TOKENS, slab)
    per_slab = slab // tt
    nq = wq_t_bf.shape[0]
    out = lambda dt: jax.ShapeDtypeStruct((PK_HEADS, N_KEYS, t_all), dt)
    tile = pl.BlockSpec((PK_HEADS, N_KEYS, tt), lambda t: (0, 0, t))
    return pl.pallas_call(
        _route_kernel,
        grid=(t_all // tt,),
        in_specs=[pl.BlockSpec((None, d, tt), lambda t: (t // per_slab, 0, t % per_slab)),
                  _resident(wq_t_bf.shape, lambda t: (0, 0)),
                  _resident(keys_bf.shape, lambda t: (0, 0, 0))],
        out_specs=[tile] * 4,
        out_shape=[out(_F32), out(_F32), out(_BF), out(_BF)],
        scratch_shapes=[pltpu.VMEM((nq, tt), _BF), pltpu.VMEM((2 * PK_HEADS, N_KEYS, tt), _F32)],
        compiler_params=_params("arbitrary"),
        name="peer_route",
    )(ht, wq_t_bf, keys_bf)


def _twice_gelu_tanh(x):
    return x * (1.0 + jnp.tanh(x * (0.7978845608028654 + (0.7978845608028654 * 0.044715) * (x * x))))


def _row_to_tile(row, rows):
    packed = jnp.broadcast_to(row, (16, LANES)).astype(_BF)
    return jnp.tile(packed, (rows // 16, 1))


def _peer_kernel(ht_ref, u_ref, vt_ref, n1_ref, e1_ref, r2_ref, e2_ref, x_ref, gt_ref, gf_ref,
                 o_ref, at_scr, wt_scr, acc, r2_scr, e2_scr, *, final_norm):
    e = pl.program_id(1)
    tt = ht_ref.shape[1]
    groups = u_ref.shape[0] // N_KEYS
    span = at_scr.shape[2]
    n_chunks = tt // span

    @pl.when(e == 0)
    def _():
        acc[...] = jnp.zeros(acc.shape, _F32)
        r2_scr[...] = r2_ref[...]
        e2_scr[...] = e2_ref[...]

    for c in range(n_chunks):
        at_scr[c] = _dot(u_ref[...], ht_ref[:, c * span:(c + 1) * span])
    zero = jnp.zeros((N_KEYS, LANES), _BF)

    for c in range(n_chunks):
        for sub in range(span // LANES):
            cols = slice(c * span + sub * LANES, c * span + (sub + 1) * LANES)
            sub_cols = slice(sub * LANES, (sub + 1) * LANES)
            for il in range(groups):
                rows = slice(il * N_KEYS, (il + 1) * N_KEYS)
                g = None
                for h in range(PK_HEADS):
                    count = _row_to_tile(n1_ref[h, il:il + 1, cols], N_KEYS)
                    sel = jnp.where(r2_scr[h, :, cols] < count, e2_scr[h, :, cols], zero)
                    term = sel * _row_to_tile(e1_ref[h, il:il + 1, cols], N_KEYS)
                    g = term if g is None else g + term
                act = _twice_gelu_tanh(at_scr[c, rows, sub_cols]).astype(_BF)
                wt_scr[c, rows, sub_cols] = g * act
        acc[c] += _dot(vt_ref[...], wt_scr[c])

    @pl.when(e == pl.num_programs(1) - 1)
    def _():
        f = jnp.concatenate([acc[c].T for c in range(n_chunks)], axis=0)
        y = x_ref[...] + gt_ref[...] * f
        if final_norm:
            y = _rms(y) * gf_ref[...]
        o_ref[...] = y


def _peer_experts(ht, u_bf, vt_bf, route, x1, gate, g_final, *, per_row, seq, final_norm):
    tiles, d, tt = ht.shape
    steps, _, eb = vt_bf.shape
    t_all = tiles * tt
    span = min(PEER_CHUNK, tt)
    groups = eb // N_KEYS
    assert groups == 8
    n1, e1, r2, e2 = route
    by_step = lambda a: a.reshape(PK_HEADS, N_KEYS // groups, groups, t_all)
    gate_spec = (pl.BlockSpec((tt, d), lambda t, e: (t, 0)) if per_row else
                 pl.BlockSpec((None, 1, d), lambda t, e: (t // max(seq // tt, 1), 0, 0)))
    step_rows = pl.BlockSpec((PK_HEADS, None, groups, tt), lambda t, e: (0, e, 0, t))
    tile = _resident((PK_HEADS, N_KEYS, tt), lambda t, e: (0, 0, t))
    return pl.pallas_call(
        functools.partial(_peer_kernel, final_norm=final_norm),
        grid=(tiles, steps),
        in_specs=[pl.BlockSpec((None, d, tt), lambda t, e: (t, 0, 0)),
                  pl.BlockSpec((eb, d), lambda t, e: (e, 0)),
                  pl.BlockSpec((None, d, eb), lambda t, e: (e, 0, 0)),
                  step_rows, step_rows, tile, tile,
                  _resident((tt, d), lambda t, e: (t, 0)),
                  gate_spec,
                  _resident((1, d), lambda t, e: (0, 0))],
        out_specs=pl.BlockSpec((tt, d), lambda t, e: (t, 0)),
        out_shape=jax.ShapeDtypeStruct((t_all, d), _F32),
        scratch_shapes=[pltpu.VMEM((tt // span, eb, span), _F32), pltpu.VMEM((tt // span, eb, span), _BF),
                        pltpu.VMEM((tt // span, d, span), _F32),
                        pltpu.VMEM((PK_HEADS, N_KEYS, tt), _BF), pltpu.VMEM((PK_HEADS, N_KEYS, tt), _BF)],
        compiler_params=_params("arbitrary", "arbitrary"),
        name="peer_experts",
    )(ht, u_bf, vt_bf, by_step(n1), by_step(e1), r2, e2, x1, gate, g_final)


def _rope_tables(pos):
    half = ROT_DIM // 2
    inv = ROPE_THETA ** (-jnp.arange(0, ROT_DIM, 2, dtype=_F32) / ROT_DIM)
    ang = pos.astype(_F32)[:, None] * inv[None, :]
    cos, sin = jnp.cos(ang), jnp.sin(ang)
    n = pos.shape[0]
    pad = jnp.zeros((n, A_DIM - ROT_DIM), _F32)
    zeros = jnp.zeros((n, half), _F32)
    c = jnp.concatenate([cos, cos, pad + 1.0], axis=1)
    sa = jnp.concatenate([-sin, zeros, pad], axis=1)
    sb = jnp.concatenate([zeros, sin, pad], axis=1)
    return tuple(jnp.tile(t, (1, LANES // A_DIM)) for t in (c, sa, sb))


def _stream_layer(x, mod, lw, rope_tabs, mixer, *, per_row, seq, final_norm, g_final):
    shift1, scale1, gate1, shift2, scale2, gate2 = mod
    proj = _in_projection(x, shift1, scale1, lw["g_attn"], lw["w_in"], rope_tabs, per_row=per_row, seq=seq)
    qa, kaf, vaf, kab, vab, qb, kbf, vbf, kbb, vbb = proj
    oa, ob, extra = mixer(qa, kaf, vaf, kab, vab, qb, kbf, vbf, kbb, vbb)
    x1, ht = _out_projection(x, oa, ob, lw["w_out"], gate1, shift2, scale2, lw["g_ffn"],
                             per_row=per_row, seq=seq)
    route = _peer_route(ht, lw["pk_wq_t"], lw["pk_keys"])
    x2 = _peer_experts(ht, lw["pk_u"], lw["pk_v_t"], route, x1, gate2, g_final,
                       per_row=per_row, seq=seq, final_norm=final_norm)
    return x2, (kaf, vaf, kbf, vbf) + extra


def kernel(x_prompt, x_sample, c_prompt, c_sample, cache_a_k, cache_a_v, cache_b_k, cache_b_v, w_ada, b_ada, g_attn, g_ffn, w_in, lam_q1, lam_k1, lam_q2, lam_k2, a_gain, rel_bias, b_gain, w_out, pk_wq, pk_keys, pk_u, pk_v, g_final):
    batch, seq, d = x_prompt.shape
    nb, t_new, _ = x_sample.shape
    depth = w_ada.shape[0]
    past = cache_a_k.shape[2]
    wb = cache_b_k.shape[2]
    a_width = A_HEADS * 2 * A_DIM
    b_width = B_HEADS * B_DIM

    c_all = jnp.concatenate([c_prompt, c_sample], axis=0)
    pad_rows = (-c_all.shape[0]) % 16
    c_all = jnp.pad(c_all, ((0, pad_rows), (0, 0)))
    mod = _modulation(c_all, w_ada, b_ada)

    tabs_p = _rope_tables(jnp.arange(seq, dtype=jnp.int32))
    pos_s = past + jnp.arange(t_new, dtype=jnp.int32)
    tabs_s = tuple(jnp.tile(t, (nb, 1)) for t in _rope_tables(pos_s))

    cak = cache_a_k.reshape(depth, nb, past * A_HEADS, 2 * A_DIM)
    cav = cache_a_v.reshape(depth, nb, past * A_HEADS, 2 * A_DIM)
    cbk = cache_b_k.reshape(depth, nb, wb, b_width)
    cbv = cache_b_v.reshape(depth, nb, wb, b_width)

    xp = x_prompt.reshape(batch * seq, d)
    xs = x_sample.reshape(nb * t_new, d)
    g_fin = g_final.reshape(1, d)
    st_p, st_s = [], []
    for l in range(depth):
        lam_init = 0.8 - 0.6 * math.exp(-0.3 * l)
        lamv = jnp.stack([lam_q1[l], lam_k1[l], lam_q2[l], lam_k2[l]])
        lw = {
            "g_attn": g_attn[l].reshape(1, d),
            "g_ffn": g_ffn[l].reshape(1, d),
            "w_in": w_in[l].astype(_BF),
            "w_out": w_out[l].astype(_BF),
            "pk_wq_t": pk_wq[l].T.astype(_BF),
            "pk_keys": pk_keys[l].reshape(2 * PK_HEADS, N_KEYS, PK_HALF).astype(_BF),
            "pk_u": pk_u[l].astype(_BF),
            "pk_v_t": pk_v[l].astype(_BF).reshape(-1, PEER_EXPERTS, d).transpose(0, 2, 1),
        }
        gain_a = a_gain[l].reshape(1, 2 * A_DIM)
        gain_b = b_gain[l].reshape(1, b_width)
        mod_p = tuple(m.reshape(batch, 1, d) for m in jnp.split(mod[l, :batch], 6, axis=-1))
        mod_s = tuple(jnp.repeat(m, t_new, axis=0) for m in jnp.split(mod[l, batch:batch + nb], 6, axis=-1))
        last = l == depth - 1

        def mix_p(qa, kaf, vaf, kab, vab, qb, kbf, vbf, kbb, vbb):
            oa = _attn_a_prompt(qa, kab, vab, lamv, gain_a, batch=batch, seq=seq, lam_init=lam_init)
            ob = _attn_b_prompt(qb, kbb, vbb, rel_bias[l], gain_b, batch=batch, seq=seq)
            return oa, ob, ()

        def mix_s(qa, kaf, vaf, kab, vab, qb, kbf, vbf, kbb, vbb):
            oa = _attn_a_sample(qa, cak, cav, kaf, vaf, lamv, gain_a, layer=l, lam_init=lam_init, t_new=t_new)
            ob, nbk, nbv = _attn_b_sample(qb, cbk, cbv, kbf, vbf, rel_bias[l], gain_b, layer=l, t_new=t_new)
            return oa, ob, (nbk, nbv)

        xp, sp = _stream_layer(xp, mod_p, lw, tabs_p, mix_p, per_row=False, seq=seq,
                               final_norm=last, g_final=g_fin)
        xs, ss = _stream_layer(xs, mod_s, lw, tabs_s, mix_s, per_row=True, seq=t_new,
                               final_norm=last, g_final=g_fin)
        st_p.append(sp)
        st_s.append(ss)

    keep = min(B_REACH, seq)
    a_shape_p = (batch, seq, A_HEADS, 2 * A_DIM)
    b_tail = lambda s: s.reshape(batch, seq, B_HEADS, B_DIM)[:, seq - keep:]
    a_shape_s = (nb, t_new, A_HEADS, 2 * A_DIM)
    b_shape_s = (nb, wb, B_HEADS, B_DIM)
    return (
        xp.reshape(batch, seq, d),
        xs.reshape(nb, t_new, d),
        jnp.stack([s[0].reshape(a_shape_p) for s in st_p]),
        jnp.stack([s[1].reshape(a_shape_p) for s in st_p]),
        jnp.stack([b_tail(s[2]) for s in st_p]),
        jnp.stack([b_tail(s[3]) for s in st_p]),
        jnp.stack([s[0].reshape(a_shape_s) for s in st_s]),
        jnp.stack([s[1].reshape(a_shape_s) for s in st_s]),
        jnp.stack([s[4].reshape(b_shape_s) for s in st_s]),
        jnp.stack([s[5].reshape(b_shape_s) for s in st_s]),
    )
```

```python
import functools
import math

import jax
import jax.numpy as jnp
import numpy as np
from jax import lax
from jax.experimental import pallas as pl
from jax.experimental.pallas import tpu as pltpu

_F32 = jnp.float32
_BF = jnp.bfloat16

CHUNK = 64
A_HEADS = 4
A_DIM = 64
ROT_DIM = A_DIM // 4
ROPE_THETA = 500000.0
B_HEADS = 8
B_DIM = 64
LEFT_CHUNKS = 8
B_REACH = LEFT_CHUNKS * CHUNK
REL_CLIP = 128
PK_HEADS = 8
N_KEYS = 128
PK_TOPK = 16
PK_HALF = 128
EPS = 1e-6

LANES = 128
MASKED = -1e30
VMEM_LIMIT = 56 * 1024 * 1024

A_TILE = 512
B_TILE = 256
PEER_TOKENS = 1024
ROUTE_TOKENS = 512
PEER_EXPERTS = 1024
PEER_CHUNK = 256
ROW_TILE = 512


def _params(*sem):
    return pltpu.CompilerParams(dimension_semantics=sem, vmem_limit_bytes=VMEM_LIMIT)


def _resident(shape, index_map):
    return pl.BlockSpec(shape, index_map, pipeline_mode=pl.Buffered(1))


def _rms(x):
    return x * lax.rsqrt(jnp.mean(x * x, axis=-1, keepdims=True) + EPS)


def _dot(a, b):
    return jnp.dot(a, b, preferred_element_type=_F32)


def _dot_nt(a, b):
    return lax.dot_general(a, b, (((1,), (1,)), ((), ())), preferred_element_type=_F32)


def _mod_kernel(c_ref, w_ref, b_ref, o_ref):
    c = c_ref[...]
    s = c * (1.0 / (1.0 + jnp.exp(-c)))
    o_ref[...] = _dot(s.astype(_BF), w_ref[...].astype(_BF)) + b_ref[...]


def _modulation(c_all, w_ada, b_ada):
    depth, d, n = w_ada.shape
    rows = c_all.shape[0]
    tn = n // 4
    return pl.pallas_call(
        _mod_kernel,
        grid=(depth, n // tn),
        in_specs=[
            pl.BlockSpec((rows, d), lambda l, j: (0, 0)),
            pl.BlockSpec((None, d, tn), lambda l, j: (l, 0, j)),
            pl.BlockSpec((None, 1, tn), lambda l, j: (l, 0, j)),
        ],
        out_specs=pl.BlockSpec((None, rows, tn), lambda l, j: (l, 0, j)),
        out_shape=jax.ShapeDtypeStruct((depth, rows, n), _F32),
        compiler_params=_params("arbitrary", "arbitrary"),
        name="adaln_mod",
    )(c_all, w_ada, b_ada.reshape(depth, 1, n))


def _inproj_kernel(x_ref, sh_ref, sc_ref, g_ref, w_ref, cos_ref, sa_ref, sb_ref,
                   qa_ref, kaf_ref, vaf_ref, kab_ref, vab_ref,
                   qb_ref, kbf_ref, vbf_ref, kbb_ref, vbb_ref, *, width):
    h = _rms(x_ref[...]) * g_ref[...] * (1.0 + sc_ref[...]) + sh_ref[...]
    hb = h.astype(_BF)
    reps = width // LANES
    cos = jnp.tile(cos_ref[...], (1, reps))
    sa = jnp.tile(sa_ref[...], (1, reps))
    sb = jnp.tile(sb_ref[...], (1, reps))

    def rope(z):
        return z * cos + pltpu.roll(z, width - ROT_DIM // 2, 1) * sa + pltpu.roll(z, ROT_DIM // 2, 1) * sb

    def proj(i):
        return _dot(hb, w_ref[:, i * width:(i + 1) * width])

    def store_state(ref, z):
        hd = 2 * A_DIM
        for head in range(A_HEADS):
            ref[pl.ds(head, z.shape[0], stride=A_HEADS), :] = z[:, head * hd:(head + 1) * hd]

    qa_ref[...] = (rope(proj(0)) * (A_DIM ** -0.5)).astype(_BF)
    ka = rope(proj(1))
    store_state(kaf_ref, ka)
    kab_ref[...] = ka.astype(_BF)
    va = proj(2)
    store_state(vaf_ref, va)
    vab_ref[...] = va.astype(_BF)
    qb_ref[...] = (proj(3) * (B_DIM ** -0.5)).astype(_BF)
    kb = proj(4)
    kbf_ref[...] = kb
    kbb_ref[...] = kb.astype(_BF)
    vb = proj(5)
    vbf_ref[...] = vb
    vbb_ref[...] = vb.astype(_BF)


def _row_mod_specs(per_row, tm, d, tiles_per_batch):
    if per_row:
        return pl.BlockSpec((tm, d), lambda t: (t, 0))
    return pl.BlockSpec((None, 1, d), lambda t: (t // tiles_per_batch, 0, 0))


def _in_projection(x, shift, scale, gain, w_in_bf, rope_tabs, *, per_row, seq):
    t_all, d = x.shape
    width = w_in_bf.shape[1] // 6
    tm = min(ROW_TILE, t_all)
    tiles_per_batch = max(seq // tm, 1)
    tab_tiles = rope_tabs[0].shape[0] // tm
    mod_spec = _row_mod_specs(per_row, tm, d, tiles_per_batch)
    tab_spec = pl.BlockSpec((tm, LANES), lambda t: (t % tab_tiles, 0))
    row = lambda dt: jax.ShapeDtypeStruct((t_all, width), dt)
    out_spec = pl.BlockSpec((tm, width), lambda t: (t, 0))
    dts = [_BF, _F32, _F32, _BF, _BF, _BF, _F32, _F32, _BF, _BF]
    out_shapes = [row(dt) for dt in dts]
    out_specs = [out_spec] * 10
    for i in (1, 2):
        out_shapes[i] = jax.ShapeDtypeStruct((t_all * A_HEADS, 2 * A_DIM), _F32)
        out_specs[i] = pl.BlockSpec((tm * A_HEADS, 2 * A_DIM), lambda t: (t, 0))
    return pl.pallas_call(
        functools.partial(_inproj_kernel, width=width),
        grid=(t_all // tm,),
        in_specs=[
            pl.BlockSpec((tm, d), lambda t: (t, 0)),
            mod_spec, mod_spec,
            _resident((1, d), lambda t: (0, 0)),
            _resident(w_in_bf.shape, lambda t: (0, 0)),
            tab_spec, tab_spec, tab_spec,
        ],
        out_specs=out_specs,
        out_shape=out_shapes,
        compiler_params=_params("arbitrary"),
        name="in_projection",
    )(x, shift, scale, gain, w_in_bf, *rope_tabs)


def _diff_lambda(lamv_ref, lam_init):
    v = lamv_ref[...]
    d1 = jnp.sum(v[0:1] * v[1:2], axis=1, keepdims=True)
    d2 = jnp.sum(v[2:3] * v[3:4], axis=1, keepdims=True)
    return jnp.exp(d1) - jnp.exp(d2) + lam_init


def _split_maps(q):
    lane = lax.broadcasted_iota(jnp.int32, q.shape, 1)
    zero = jnp.zeros_like(q)
    return jnp.where(lane < A_DIM, q, zero), jnp.where(lane >= A_DIM, q, zero)


def _attn_a_kernel(lamv_ref, q_ref, k_ref, v_ref, gain_ref, o_ref,
                   sa, sb, m1, l1, acc1, m2, l2, acc2, *, lam_init):
    n_full = pl.program_id(2)
    tq = q_ref.shape[0]
    q1, q2 = _split_maps(q_ref[...])
    for m_scr, l_scr, acc in ((m1, l1, acc1), (m2, l2, acc2)):
        m_scr[...] = jnp.full(m_scr.shape, MASKED, _F32)
        l_scr[...] = jnp.zeros(l_scr.shape, _F32)
        acc[...] = jnp.zeros(acc.shape, _F32)

    def rows(j):
        return pl.ds(pl.multiple_of(j * tq, tq), tq)

    def scores(j, s_scr):
        kb = k_ref[rows(j), :]
        s_scr[0] = _dot_nt(q1, kb)
        s_scr[1] = _dot_nt(q2, kb)

    def update(s, m_scr, l_scr, acc, vb):
        m_prev = m_scr[...]
        m_next = jnp.maximum(m_prev, jnp.max(s, axis=1, keepdims=True))
        p = jnp.exp(s - jnp.tile(m_next, (1, tq // LANES)))
        alpha = jnp.exp(m_prev - m_next)
        part = p[:, 0:LANES]
        for c in range(1, tq // LANES):
            part = part + p[:, c * LANES:(c + 1) * LANES]
        l_scr[...] = alpha * l_scr[...] + part
        acc[...] = acc[...] * alpha + _dot(p.astype(_BF), vb)
        m_scr[...] = m_next

    def process(j, s_scr, diagonal):
        vb = v_ref[rows(j), :]
        for mp, (m_scr, l_scr, acc) in enumerate(((m1, l1, acc1), (m2, l2, acc2))):
            s = s_scr[mp]
            if diagonal:
                qc = lax.broadcasted_iota(jnp.int32, s.shape, 0) // CHUNK
                kc = lax.broadcasted_iota(jnp.int32, s.shape, 1) // CHUNK
                s = jnp.where(kc <= qc, s, MASKED)
            update(s, m_scr, l_scr, acc, vb)

    scores(0, sa)

    def pair(i, carry):
        scores(2 * i + 1, sb)
        process(2 * i, sa, False)
        scores(2 * i + 2, sa)
        process(2 * i + 1, sb, False)
        return carry

    lax.fori_loop(0, n_full // 2, pair, 0)

    @pl.when(n_full % 2 == 1)
    def _():
        scores(n_full, sb)
        process(n_full - 1, sa, False)
        process(n_full, sb, True)

    @pl.when(n_full % 2 == 0)
    def _():
        process(n_full, sa, True)

    lam = _diff_lambda(lamv_ref, lam_init)
    d1 = jnp.sum(l1[...], axis=1, keepdims=True)
    d2 = jnp.sum(l2[...], axis=1, keepdims=True)
    o = acc1[...] / d1 - lam * (acc2[...] / d2)
    o_ref[...] = (_rms(o) * gain_ref[...] * (1.0 - lam_init)).astype(_BF)


def _attn_a_prompt(qa, ka, va, lamv, a_gain, *, batch, seq, lam_init):
    t_all, width = qa.shape
    hd = 2 * A_DIM
    tq = A_TILE
    nq = seq // tq
    return pl.pallas_call(
        functools.partial(_attn_a_kernel, lam_init=lam_init),
        grid=(batch, A_HEADS, nq),
        in_specs=[
            _resident(lamv.shape, lambda b, h, i: (0, 0)),
            pl.BlockSpec((tq, hd), lambda b, h, i: (b * nq + i, h)),
            pl.BlockSpec((seq, hd), lambda b, h, i: (b, h)),
            pl.BlockSpec((seq, hd), lambda b, h, i: (b, h)),
            _resident((1, hd), lambda b, h, i: (0, 0)),
        ],
        out_specs=pl.BlockSpec((tq, hd), lambda b, h, i: (b * nq + i, h)),
        out_shape=jax.ShapeDtypeStruct((t_all, width), _BF),
        scratch_shapes=[pltpu.VMEM((2, tq, tq), _F32)] * 2 + [pltpu.VMEM((tq, hd), _F32)] * 6,
        compiler_params=_params("arbitrary", "arbitrary", "arbitrary"),
        name="diff_attn_prompt",
    )(lamv, qa, ka, va, a_gain)


def _head_lanes(shape, head, dim):
    lane = lax.broadcasted_iota(jnp.int32, shape, 1)
    return (lane >= head * dim) & (lane < (head + 1) * dim)


def _attn_b_kernel(q_ref, k0_ref, k1_ref, k2_ref, v0_ref, v1_ref, v2_ref, bias_ref, gain_ref, o_ref):
    qi = pl.program_id(1)
    tq = q_ref.shape[0]
    k_refs = (k0_ref, k1_ref, k2_ref)
    v_refs = (v0_ref, v1_ref, v2_ref)
    pen = (jnp.where(qi >= 2, 0.0, MASKED), jnp.where(qi >= 1, 0.0, MASKED), 0.0)
    outs = []
    for pair in range(B_HEADS // 2):
        cols = slice(pair * LANES, (pair + 1) * LANES)
        qp = q_ref[:, cols]
        zero = jnp.zeros_like(qp)
        halves = []
        for sub in range(2):
            head = 2 * pair + sub
            qh = jnp.where(_head_lanes(qp.shape, sub, B_DIM), qp, zero)
            s = [_dot_nt(qh, k_refs[b][:, cols]) + bias_ref[head, :, b * tq:(b + 1) * tq] + pen[b]
                 for b in range(3)]
            m = jnp.maximum(jnp.maximum(jnp.max(s[0], axis=1, keepdims=True),
                                        jnp.max(s[1], axis=1, keepdims=True)),
                            jnp.max(s[2], axis=1, keepdims=True))
            p = [jnp.exp(sb - m) for sb in s]
            l = (jnp.sum(p[0], axis=1, keepdims=True) + jnp.sum(p[1], axis=1, keepdims=True)
                 + jnp.sum(p[2], axis=1, keepdims=True))
            o = (_dot(p[0].astype(_BF), v_refs[0][:, cols]) + _dot(p[1].astype(_BF), v_refs[1][:, cols])
                 + _dot(p[2].astype(_BF), v_refs[2][:, cols]))
            halves.append(o / l)
        outs.append(jnp.where(_head_lanes(halves[0].shape, 0, B_DIM), halves[0], halves[1]))
    o = jnp.concatenate(outs, axis=1)
    o_ref[...] = (_rms(o) * gain_ref[...]).astype(_BF)


def _band_bias(table, tq):
    nk = B_REACH + tq
    qpos = np.arange(tq)[:, None]
    kpos = np.arange(nk)[None, :] - B_REACH
    qc = qpos // CHUNK
    kc = np.floor_divide(kpos, CHUNK)
    valid = (kc <= qc) & (kc >= qc - LEFT_CHUNKS)
    span = nk + tq - 1
    dist = (span - 1 - np.arange(span)) - (tq - 1)
    line = table[:, np.clip(dist, -REL_CLIP, REL_CLIP) + REL_CLIP].astype(_F32)
    line = jnp.pad(line, ((0, 0), (0, 1)))
    skew = jnp.tile(line, (1, tq))[:, :tq * span].reshape(-1, tq, span)
    bias = skew[:, :, tq - 1:]
    return jnp.where(valid[None], bias, MASKED)


def _attn_b_prompt(qb, kb, vb, rel_table, b_gain, *, batch, seq):
    t_all, width = qb.shape
    tq = B_TILE
    assert B_REACH == 2 * tq
    nq = seq // tq
    bias = _band_bias(rel_table, tq)
    qmap = lambda b, i: (b * nq + i, 0)
    kmap = lambda back: (lambda b, i: (b * nq + jnp.maximum(i - back, 0), 0))
    blk = lambda f: pl.BlockSpec((tq, width), f)
    return pl.pallas_call(
        _attn_b_kernel,
        grid=(batch, nq),
        in_specs=[blk(qmap), blk(kmap(2)), blk(kmap(1)), blk(kmap(0)),
                  blk(kmap(2)), blk(kmap(1)), blk(kmap(0)),
                  _resident(bias.shape, lambda b, i: (0, 0, 0)),
                  _resident((1, width), lambda b, i: (0, 0))],
        out_specs=blk(qmap),
        out_shape=jax.ShapeDtypeStruct((t_all, width), _BF),
        compiler_params=_params("arbitrary", "arbitrary"),
        name="band_attn_prompt",
    )(qb, kb, kb, kb, vb, vb, vb, bias, b_gain)


def _attn_a_sample_kernel(lamv_ref, q_ref, kc_ref, vc_ref, kn_ref, vn_ref, gain_ref, o_ref, *, lam_init):
    q = q_ref[...]
    t = q.shape[0]
    hd = 2 * A_DIM
    past = kc_ref.shape[0] // A_HEADS
    lam = _diff_lambda(lamv_ref, lam_init)
    outs = []
    for h in range(A_HEADS):
        cols = slice(h * hd, (h + 1) * hd)
        head_rows = pl.ds(h, past, stride=A_HEADS)
        new_rows = pl.ds(h, t, stride=A_HEADS)
        q12 = jnp.concatenate(_split_maps(q[:, cols]), axis=0)
        kn = kn_ref[new_rows, :].astype(_BF)
        vn = vn_ref[new_rows, :].astype(_BF)
        s_c = _dot_nt(q12, kc_ref[head_rows, :].astype(_BF))
        s_n = _dot_nt(q12, kn)
        m = jnp.maximum(jnp.max(s_c, axis=1, keepdims=True), jnp.max(s_n, axis=1, keepdims=True))
        p_c = jnp.exp(s_c - m)
        p_n = jnp.exp(s_n - m)
        l = jnp.sum(p_c, axis=1, keepdims=True) + jnp.sum(p_n, axis=1, keepdims=True)
        p_c = p_c / l
        p_n = p_n / l
        a_c = p_c[0:t] - lam * p_c[t:2 * t]
        a_n = p_n[0:t] - lam * p_n[t:2 * t]
        o = _dot(a_c.astype(_BF), vc_ref[head_rows, :].astype(_BF)) + _dot(a_n.astype(_BF), vn)
        outs.append(_rms(o) * gain_ref[...] * (1.0 - lam_init))
    o_ref[...] = jnp.concatenate(outs, axis=1).astype(_BF)


def _attn_a_sample(qa, cache_k, cache_v, kn, vn, lamv, a_gain, *, layer, lam_init, t_new):
    t_all, width = qa.shape
    _, nb, cache_rows, hd = cache_k.shape
    tok = pl.BlockSpec((t_new, width), lambda b: (b, 0))
    new = pl.BlockSpec((t_new * A_HEADS, hd), lambda b: (b, 0))
    cache = pl.BlockSpec((None, None, cache_rows, hd), lambda b: (layer, b, 0, 0))
    return pl.pallas_call(
        functools.partial(_attn_a_sample_kernel, lam_init=lam_init),
        grid=(nb,),
        in_specs=[_resident(lamv.shape, lambda b: (0, 0)), tok, cache, cache, new, new,
                  _resident((1, 2 * A_DIM), lambda b: (0, 0))],
        out_specs=tok,
        out_shape=jax.ShapeDtypeStruct((t_all, width), _BF),
        compiler_params=_params("arbitrary"),
        name="diff_attn_sample",
    )(lamv, qa, cache_k, cache_v, kn, vn, a_gain)


def _attn_b_sample_kernel(q_ref, kc_ref, vc_ref, kn_ref, vn_ref, bc_ref, bn_ref, gain_ref,
                          o_ref, ko_ref, vo_ref):
    q = q_ref[...]
    t = q.shape[0]
    keep = kc_ref.shape[0] - t
    zero = jnp.zeros_like(q)
    qall = jnp.concatenate(
        [jnp.where(_head_lanes(q.shape, h, B_DIM), q, zero) for h in range(B_HEADS)], axis=0)
    kc = kc_ref[...]
    vc = vc_ref[...]
    kn = kn_ref[...]
    vn = vn_ref[...]
    s_c = _dot_nt(qall, kc.astype(_BF)) + bc_ref[...]
    s_n = _dot_nt(qall, kn.astype(_BF)) + bn_ref[...]
    m = jnp.maximum(jnp.max(s_c, axis=1, keepdims=True), jnp.max(s_n, axis=1, keepdims=True))
    p_c = jnp.exp(s_c - m)
    p_n = jnp.exp(s_n - m)
    l = jnp.sum(p_c, axis=1, keepdims=True) + jnp.sum(p_n, axis=1, keepdims=True)
    o_all = (_dot(p_c.astype(_BF), vc.astype(_BF)) + _dot(p_n.astype(_BF), vn.astype(_BF))) / l
    o = jnp.zeros((t, q.shape[1]), _F32)
    for h in range(B_HEADS):
        o = jnp.where(_head_lanes(o.shape, h, B_DIM), o_all[h * t:(h + 1) * t], o)
    o_ref[...] = (_rms(o) * gain_ref[...]).astype(_BF)
    ko_ref[0:keep, :] = kc[t:, :]
    ko_ref[keep:, :] = kn
    vo_ref[0:keep, :] = vc[t:, :]
    vo_ref[keep:, :] = vn


def _sample_bias(table, t_new, wb):
    qpos = jnp.arange(t_new)[:, None] + wb
    kpos = jnp.arange(wb + t_new)[None, :]
    d = jnp.clip(qpos - kpos, -REL_CLIP, REL_CLIP) + REL_CLIP
    bias = table[:, d].astype(_F32).reshape(B_HEADS * t_new, wb + t_new)
    return bias[:, :wb], bias[:, wb:]


def _attn_b_sample(qb, cache_k, cache_v, kn, vn, rel_table, b_gain, *, layer, t_new):
    t_all, width = qb.shape
    _, nb, wb, _ = cache_k.shape
    bias_c, bias_n = _sample_bias(rel_table, t_new, wb)
    tok = pl.BlockSpec((t_new, width), lambda b: (b, 0))
    cache = pl.BlockSpec((None, None, wb, width), lambda b: (layer, b, 0, 0))
    state = pl.BlockSpec((None, wb, width), lambda b: (b, 0, 0))
    return pl.pallas_call(
        _attn_b_sample_kernel,
        grid=(nb,),
        in_specs=[tok, cache, cache, tok, tok,
                  _resident(bias_c.shape, lambda b: (0, 0)), _resident(bias_n.shape, lambda b: (0, 0)),
                  _resident((1, width), lambda b: (0, 0))],
        out_specs=[tok, state, state],
        out_shape=[jax.ShapeDtypeStruct((t_all, width), _BF),
                   jax.ShapeDtypeStruct((nb, wb, width), _F32),
                   jax.ShapeDtypeStruct((nb, wb, width), _F32)],
        compiler_params=_params("arbitrary"),
        name="band_attn_sample",
    )(qb, cache_k, cache_v, kn, vn, bias_c, bias_n, b_gain)


def _outproj_kernel(x_ref, oa_ref, ob_ref, w_ref, gt_ref, sh_ref, sc_ref, g_ref, x1_ref, ht_ref):
    half = oa_ref.shape[1]
    y = _dot(oa_ref[...], w_ref[0:half, :]) + _dot(ob_ref[...], w_ref[half:, :])
    x1 = x_ref[...] + gt_ref[...] * y
    x1_ref[...] = x1
    h = _rms(x1) * g_ref[...] * (1.0 + sc_ref[...]) + sh_ref[...]
    ht_ref[...] = h.T.astype(_BF)


def _out_projection(x, oa, ob, w_out_bf, gate, shift, scale, gain, *, per_row, seq):
    t_all, d = x.shape
    half = oa.shape[1]
    tm = min(PEER_TOKENS, t_all)
    mod_spec = _row_mod_specs(per_row, tm, d, max(seq // tm, 1))
    rows = lambda w: pl.BlockSpec((tm, w), lambda t: (t, 0))
    return pl.pallas_call(
        _outproj_kernel,
        grid=(t_all // tm,),
        in_specs=[rows(d), rows(half), rows(half), _resident(w_out_bf.shape, lambda t: (0, 0)),
                  mod_spec, mod_spec, mod_spec, _resident((1, d), lambda t: (0, 0))],
        out_specs=[rows(d), pl.BlockSpec((None, d, tm), lambda t: (t, 0, 0))],
        out_shape=[jax.ShapeDtypeStruct((t_all, d), _F32),
                   jax.ShapeDtypeStruct((t_all // tm, d, tm), _BF)],
        compiler_params=_params("arbitrary"),
        name="out_projection",
    )(x, oa, ob, w_out_bf, gate, shift, scale, gain)


def _top_values(s, k, want_rank=False):
    vals = []
    rank = jnp.full(s.shape, float(k), _F32)
    for i in range(k):
        m = jnp.max(s, axis=0, keepdims=True)
        vals.append(m)
        hit = s == m
        if want_rank:
            rank = jnp.where(hit, float(i), rank)
        s = jnp.where(hit, -jnp.inf, s)
    return (vals, rank) if want_rank else vals


def _stack_rows(vals):
    rows = lax.broadcasted_iota(jnp.int32, (len(vals), LANES), 0)
    out = jnp.zeros((len(vals), LANES), _F32)
    for i, v in enumerate(vals):
        out = jnp.where(rows == i, v, out)
    return out


SUBLANES = 8


def _merge_sort_network(lo, hi):
    def merge(lo, hi, r):
        step = 2 * r
        if step < hi - lo:
            yield from merge(lo, hi, step)
            yield from merge(lo + r, hi, step)
            yield from ((i, i + r) for i in range(lo + r, hi - r, step))
        else:
            yield (lo, lo + r)

    if hi - lo >= 1:
        mid = lo + (hi - lo) // 2
        yield from _merge_sort_network(lo, mid)
        yield from _merge_sort_network(mid + 1, hi)
        yield from merge(lo, hi, 1)


def _compare_exchange(v, i, j):
    a, b = v[i], v[j]
    if b is None:
        return
    if a is None:
        v[i], v[j] = b, None
        return
    v[i], v[j] = jnp.maximum(a, b), jnp.minimum(a, b)


def _top16_replicated(tiles):
    n = PK_TOPK
    v = list(tiles) + [None] * (n - len(tiles))
    for i, j in _merge_sort_network(0, n - 1):
        _compare_exchange(v, i, j)
    shift = SUBLANES // 2
    while shift:
        w = [None if x is None else pltpu.roll(x, shift, 0) for x in v]
        c = []
        for k in range(n):
            a, b = v[k], w[n - 1 - k]
            c.append(b if a is None else a if b is None else jnp.maximum(a, b))
        stride = n // 2
        while stride:
            for i in range(n):
                if not i & stride:
                    _compare_exchange(c, i, i + stride)
            stride //= 2
        v = c
        shift //= 2
    return [jnp.full((SUBLANES, LANES), -jnp.inf, _F32) if x is None else x for x in v]


def _route_kernel(ht_ref, wq_ref, keys_ref, n1_ref, e1_ref, r2_ref, e2_ref, q_scr, s_scr):
    tt = ht_ref.shape[1]
    q_scr[...] = _dot(wq_ref[...], ht_ref[...]).astype(_BF)
    for hm in range(2 * PK_HEADS):
        s_scr[hm] = _dot(keys_ref[hm], q_scr[hm * PK_HALF:(hm + 1) * PK_HALF, :])

    def head(h, carry):
        for c in range(tt // LANES):
            cols = slice(c * LANES, (c + 1) * LANES)
            s1 = s_scr[2 * h, :, cols]
            s2 = s_scr[2 * h + 1, :, cols]
            a = _top16_replicated([s1[r:r + SUBLANES] for r in range(0, N_KEYS, SUBLANES)])
            b, rank2 = _top_values(s2, PK_TOPK, want_rank=True)
            bmat = _stack_rows(b)
            b_lo, b_hi = bmat[0:SUBLANES], bmat[SUBLANES:]
            cand = [a[0] + b_lo, a[0] + b_hi]
            cand += [a[p] + b_lo for p in range(1, 8)]
            cand += [_stack_rows([a[p][0:1] for p in range(8, PK_TOPK)]) + b[0]]
            top = _top16_replicated(cand)
            tau = top[PK_TOPK - 1]
            z = jnp.ones_like(tau)
            for k in range(1, PK_TOPK):
                z = z + jnp.exp(top[k] - top[0])
            z_row = z[0:1]
            count1 = jnp.zeros(s1.shape, _F32)
            for p in range(PK_TOPK):
                hit = jnp.where(a[p] + b_lo >= tau, 1.0, 0.0) + jnp.where(a[p] + b_hi >= tau, 1.0, 0.0)
                reach = jnp.sum(hit, axis=0, keepdims=True)
                count1 = jnp.where(s1 == jnp.tile(a[p], (N_KEYS // SUBLANES, 1)), reach, count1)
            n1_ref[h, :, cols] = count1
            e1_ref[h, :, cols] = jnp.exp(s1 - a[0][0:1]) * (0.5 / z_row)
            r2_ref[h, :, cols] = rank2.astype(_BF)
            e2_ref[h, :, cols] = jnp.exp(s2 - b[0]).astype(_BF)
        return carry

    lax.fori_loop(0, PK_HEADS, head, 0)


def _peer_route(ht, wq_t_bf, keys_bf):
    slabs, d, slab = ht.shape
    t_all = slabs * slab
    tt = min(ROUTE_TOKENS, slab)
    per_slab = slab // tt
    nq = wq_t_bf.shape[0]
    out = lambda dt: jax.ShapeDtypeStruct((PK_HEADS, N_KEYS, t_all), dt)
    tile = pl.BlockSpec((PK_HEADS, N_KEYS, tt), lambda t: (0, 0, t))
    return pl.pallas_call(
        _route_kernel,
        grid=(t_all // tt,),
        in_specs=[pl.BlockSpec((None, d, tt), lambda t: (t // per_slab, 0, t % per_slab)),
                  _resident(wq_t_bf.shape, lambda t: (0, 0)),
                  _resident(keys_bf.shape, lambda t: (0, 0, 0))],
        out_specs=[tile] * 4,
        out_shape=[out(_F32), out(_F32), out(_BF), out(_BF)],
        scratch_shapes=[pltpu.VMEM((nq, tt), _BF), pltpu.VMEM((2 * PK_HEADS, N_KEYS, tt), _F32)],
        compiler_params=_params("arbitrary"),
        name="peer_route",
    )(ht, wq_t_bf, keys_bf)


def _twice_gelu_tanh(x):
    return x * (1.0 + jnp.tanh(x * (0.7978845608028654 + (0.7978845608028654 * 0.044715) * (x * x))))


def _row_to_tile(row, rows):
    packed = jnp.broadcast_to(row, (16, LANES)).astype(_BF)
    return jnp.tile(packed, (rows // 16, 1))


def _peer_kernel(ht_ref, u_ref, vt_ref, n1_ref, e1_ref, r2_ref, e2_ref, x_ref, gt_ref, gf_ref,
                 o_ref, at_scr, wt_scr, acc, r2_scr, e2_scr, *, final_norm):
    e = pl.program_id(1)
    tt = ht_ref.shape[1]
    groups = u_ref.shape[0] // N_KEYS
    span = at_scr.shape[2]
    n_chunks = tt // span

    @pl.when(e == 0)
    def _():
        acc[...] = jnp.zeros(acc.shape, _F32)
        r2_scr[...] = r2_ref[...]
        e2_scr[...] = e2_ref[...]

    for c in range(n_chunks):
        at_scr[c] = _dot(u_ref[...], ht_ref[:, c * span:(c + 1) * span])
    zero = jnp.zeros((N_KEYS, LANES), _BF)

    def gate_block(c, cols, sub_cols):
        for il in range(groups):
            rows = slice(il * N_KEYS, (il + 1) * N_KEYS)
            g = None
            for h in range(PK_HEADS):
                count = _row_to_tile(n1_ref[h, il:il + 1, cols], N_KEYS)
                sel = jnp.where(r2_scr[h, :, cols] < count, e2_scr[h, :, cols], zero)
                term = sel * _row_to_tile(e1_ref[h, il:il + 1, cols], N_KEYS)
                g = term if g is None else g + term
            act = _twice_gelu_tanh(at_scr[c, rows, sub_cols]).astype(_BF)
            wt_scr[c, rows, sub_cols] = g * act

    for c in range(n_chunks):
        def column_block(sub, carry, c=c):
            off = pl.multiple_of(sub * LANES, LANES)
            gate_block(c, pl.ds(c * span + off, LANES), pl.ds(off, LANES))
            return carry

        lax.fori_loop(0, span // LANES, column_block, 0)
    for c in range(n_chunks):
        acc[c] += _dot(vt_ref[...], wt_scr[c])

    @pl.when(e == pl.num_programs(1) - 1)
    def _():
        f = jnp.concatenate([acc[c].T for c in range(n_chunks)], axis=0)
        y = x_ref[...] + gt_ref[...] * f
        if final_norm:
            y = _rms(y) * gf_ref[...]
        o_ref[...] = y


def _peer_experts(ht, u_bf, vt_bf, route, x1, gate, g_final, *, per_row, seq, final_norm):
    tiles, d, tt = ht.shape
    steps, _, eb = vt_bf.shape
    t_all = tiles * tt
    span = min(PEER_CHUNK, tt)
    groups = eb // N_KEYS
    assert groups == 8
    n1, e1, r2, e2 = route
    by_step = lambda a: a.reshape(PK_HEADS, N_KEYS // groups, groups, t_all)
    gate_spec = (pl.BlockSpec((tt, d), lambda t, e: (t, 0)) if per_row else
                 pl.BlockSpec((None, 1, d), lambda t, e: (t // max(seq // tt, 1), 0, 0)))
    step_rows = pl.BlockSpec((PK_HEADS, None, groups, tt), lambda t, e: (0, e, 0, t))
    tile = _resident((PK_HEADS, N_KEYS, tt), lambda t, e: (0, 0, t))
    return pl.pallas_call(
        functools.partial(_peer_kernel, final_norm=final_norm),
        grid=(tiles, steps),
        in_specs=[pl.BlockSpec((None, d, tt), lambda t, e: (t, 0, 0)),
                  pl.BlockSpec((eb, d), lambda t, e: (e, 0)),
                  pl.BlockSpec((None, d, eb), lambda t, e: (e, 0, 0)),
                  step_rows, step_rows, tile, tile,
                  _resident((tt, d), lambda t, e: (t, 0)),
                  gate_spec,
                  _resident((1, d), lambda t, e: (0, 0))],
        out_specs=pl.BlockSpec((tt, d), lambda t, e: (t, 0)),
        out_shape=jax.ShapeDtypeStruct((t_all, d), _F32),
        scratch_shapes=[pltpu.VMEM((tt // span, eb, span), _F32), pltpu.VMEM((tt // span, eb, span), _BF),
                        pltpu.VMEM((tt // span, d, span), _F32),
                        pltpu.VMEM((PK_HEADS, N_KEYS, tt), _BF), pltpu.VMEM((PK_HEADS, N_KEYS, tt), _BF)],
        compiler_params=_params("arbitrary", "arbitrary"),
        name="peer_experts",
    )(ht, u_bf, vt_bf, by_step(n1), by_step(e1), r2, e2, x1, gate, g_final)


def _rope_tables(pos):
    half = ROT_DIM // 2
    inv = ROPE_THETA ** (-jnp.arange(0, ROT_DIM, 2, dtype=_F32) / ROT_DIM)
    ang = pos.astype(_F32)[:, None] * inv[None, :]
    cos, sin = jnp.cos(ang), jnp.sin(ang)
    n = pos.shape[0]
    pad = jnp.zeros((n, A_DIM - ROT_DIM), _F32)
    zeros = jnp.zeros((n, half), _F32)
    c = jnp.concatenate([cos, cos, pad + 1.0], axis=1)
    sa = jnp.concatenate([-sin, zeros, pad], axis=1)
    sb = jnp.concatenate([zeros, sin, pad], axis=1)
    return tuple(jnp.tile(t, (1, LANES // A_DIM)) for t in (c, sa, sb))


def _stream_layer(x, mod, lw, rope_tabs, mixer, *, per_row, seq, final_norm, g_final):
    shift1, scale1, gate1, shift2, scale2, gate2 = mod
    proj = _in_projection(x, shift1, scale1, lw["g_attn"], lw["w_in"], rope_tabs, per_row=per_row, seq=seq)
    qa, kaf, vaf, kab, vab, qb, kbf, vbf, kbb, vbb = proj
    oa, ob, extra = mixer(qa, kaf, vaf, kab, vab, qb, kbf, vbf, kbb, vbb)
    x1, ht = _out_projection(x, oa, ob, lw["w_out"], gate1, shift2, scale2, lw["g_ffn"],
                             per_row=per_row, seq=seq)
    route = _peer_route(ht, lw["pk_wq_t"], lw["pk_keys"])
    x2 = _peer_experts(ht, lw["pk_u"], lw["pk_v_t"], route, x1, gate2, g_final,
                       per_row=per_row, seq=seq, final_norm=final_norm)
    return x2, (kaf, vaf, kbf, vbf) + extra


def kernel(x_prompt, x_sample, c_prompt, c_sample, cache_a_k, cache_a_v, cache_b_k, cache_b_v, w_ada, b_ada, g_attn, g_ffn, w_in, lam_q1, lam_k1, lam_q2, lam_k2, a_gain, rel_bias, b_gain, w_out, pk_wq, pk_keys, pk_u, pk_v, g_final):
    batch, seq, d = x_prompt.shape
    nb, t_new, _ = x_sample.shape
    depth = w_ada.shape[0]
    past = cache_a_k.shape[2]
    wb = cache_b_k.shape[2]
    a_width = A_HEADS * 2 * A_DIM
    b_width = B_HEADS * B_DIM

    c_all = jnp.concatenate([c_prompt, c_sample], axis=0)
    pad_rows = (-c_all.shape[0]) % 16
    c_all = jnp.pad(c_all, ((0, pad_rows), (0, 0)))
    mod = _modulation(c_all, w_ada, b_ada)

    tabs_p = _rope_tables(jnp.arange(seq, dtype=jnp.int32))
    pos_s = past + jnp.arange(t_new, dtype=jnp.int32)
    tabs_s = tuple(jnp.tile(t, (nb, 1)) for t in _rope_tables(pos_s))

    cak = cache_a_k.reshape(depth, nb, past * A_HEADS, 2 * A_DIM)
    cav = cache_a_v.reshape(depth, nb, past * A_HEADS, 2 * A_DIM)
    cbk = cache_b_k.reshape(depth, nb, wb, b_width)
    cbv = cache_b_v.reshape(depth, nb, wb, b_width)

    xp = x_prompt.reshape(batch * seq, d)
    xs = x_sample.reshape(nb * t_new, d)
    g_fin = g_final.reshape(1, d)
    st_p, st_s = [], []
    for l in range(depth):
        lam_init = 0.8 - 0.6 * math.exp(-0.3 * l)
        lamv = jnp.stack([lam_q1[l], lam_k1[l], lam_q2[l], lam_k2[l]])
        lw = {
            "g_attn": g_attn[l].reshape(1, d),
            "g_ffn": g_ffn[l].reshape(1, d),
            "w_in": w_in[l].astype(_BF),
            "w_out": w_out[l].astype(_BF),
            "pk_wq_t": pk_wq[l].T.astype(_BF),
            "pk_keys": pk_keys[l].reshape(2 * PK_HEADS, N_KEYS, PK_HALF).astype(_BF),
            "pk_u": pk_u[l].astype(_BF),
            "pk_v_t": pk_v[l].astype(_BF).reshape(-1, PEER_EXPERTS, d).transpose(0, 2, 1),
        }
        gain_a = a_gain[l].reshape(1, 2 * A_DIM)
        gain_b = b_gain[l].reshape(1, b_width)
        mod_p = tuple(m.reshape(batch, 1, d) for m in jnp.split(mod[l, :batch], 6, axis=-1))
        mod_s = tuple(jnp.repeat(m, t_new, axis=0) for m in jnp.split(mod[l, batch:batch + nb], 6, axis=-1))
        last = l == depth - 1

        def mix_p(qa, kaf, vaf, kab, vab, qb, kbf, vbf, kbb, vbb):
            oa = _attn_a_prompt(qa, kab, vab, lamv, gain_a, batch=batch, seq=seq, lam_init=lam_init)
            ob = _attn_b_prompt(qb, kbb, vbb, rel_bias[l], gain_b, batch=batch, seq=seq)
            return oa, ob, ()

        def mix_s(qa, kaf, vaf, kab, vab, qb, kbf, vbf, kbb, vbb):
            oa = _attn_a_sample(qa, cak, cav, kaf, vaf, lamv, gain_a, layer=l, lam_init=lam_init, t_new=t_new)
            ob, nbk, nbv = _attn_b_sample(qb, cbk, cbv, kbf, vbf, rel_bias[l], gain_b, layer=l, t_new=t_new)
            return oa, ob, (nbk, nbv)

        xp, sp = _stream_layer(xp, mod_p, lw, tabs_p, mix_p, per_row=False, seq=seq,
                               final_norm=last, g_final=g_fin)
        xs, ss = _stream_layer(xs, mod_s, lw, tabs_s, mix_s, per_row=True, seq=t_new,
                               final_norm=last, g_final=g_fin)
        st_p.append(sp)
        st_s.append(ss)

    keep = min(B_REACH, seq)
    a_shape_p = (batch, seq, A_HEADS, 2 * A_DIM)
    b_tail = lambda s: s.reshape(batch, seq, B_HEADS, B_DIM)[:, seq - keep:]
    a_shape_s = (nb, t_new, A_HEADS, 2 * A_DIM)
    b_shape_s = (nb, wb, B_HEADS, B_DIM)
    return (
        xp.reshape(batch, seq, d),
        xs.reshape(nb, t_new, d),
        jnp.stack([s[0].reshape(a_shape_p) for s in st_p]),
        jnp.stack([s[1].reshape(a_shape_p) for s in st_p]),
        jnp.stack([b_tail(s[2]) for s in st_p]),
        jnp.stack([b_tail(s[3]) for s in st_p]),
        jnp.stack([s[0].reshape(a_shape_s) for s in st_s]),
        jnp.stack([s[1].reshape(a_shape_s) for s in st_s]),
        jnp.stack([s[4].reshape(b_shape_s) for s in st_s]),
        jnp.stack([s[5].reshape(b_shape_s) for s in st_s]),
    )
```

```python
import functools
import math

import jax
import jax.numpy as jnp
import numpy as np
from jax import lax
from jax.experimental import pallas as pl
from jax.experimental.pallas import tpu as pltpu

_F32 = jnp.float32
_BF = jnp.bfloat16

CHUNK = 64
A_HEADS = 4
A_DIM = 64
ROT_DIM = A_DIM // 4
ROPE_THETA = 500000.0
B_HEADS = 8
B_DIM = 64
LEFT_CHUNKS = 8
B_REACH = LEFT_CHUNKS * CHUNK
REL_CLIP = 128
PK_HEADS = 8
N_KEYS = 128
PK_TOPK = 16
PK_HALF = 128
EPS = 1e-6

LANES = 128
MASKED = -1e30
VMEM_LIMIT = 56 * 1024 * 1024

A_TILE = 512
B_TILE = 256
PEER_TOKENS = 1024
ROUTE_TOKENS = 512
PEER_EXPERTS = 1024
PEER_CHUNK = 256
ROW_TILE = 512


def _params(*sem):
    return pltpu.CompilerParams(dimension_semantics=sem, vmem_limit_bytes=VMEM_LIMIT)


def _resident(shape, index_map):
    return pl.BlockSpec(shape, index_map, pipeline_mode=pl.Buffered(1))


def _rms(x):
    return x * lax.rsqrt(jnp.mean(x * x, axis=-1, keepdims=True) + EPS)


def _dot(a, b):
    return jnp.dot(a, b, preferred_element_type=_F32)


def _dot_nt(a, b):
    return lax.dot_general(a, b, (((1,), (1,)), ((), ())), preferred_element_type=_F32)


def _mod_kernel(c_ref, w_ref, b_ref, o_ref):
    c = c_ref[...]
    s = c * (1.0 / (1.0 + jnp.exp(-c)))
    o_ref[...] = _dot(s.astype(_BF), w_ref[...].astype(_BF)) + b_ref[...]


def _modulation(c_all, w_ada, b_ada):
    depth, d, n = w_ada.shape
    rows = c_all.shape[0]
    tn = n // 4
    return pl.pallas_call(
        _mod_kernel,
        grid=(depth, n // tn),
        in_specs=[
            pl.BlockSpec((rows, d), lambda l, j: (0, 0)),
            pl.BlockSpec((None, d, tn), lambda l, j: (l, 0, j)),
            pl.BlockSpec((None, 1, tn), lambda l, j: (l, 0, j)),
        ],
        out_specs=pl.BlockSpec((None, rows, tn), lambda l, j: (l, 0, j)),
        out_shape=jax.ShapeDtypeStruct((depth, rows, n), _F32),
        compiler_params=_params("arbitrary", "arbitrary"),
        name="adaln_mod",
    )(c_all, w_ada, b_ada.reshape(depth, 1, n))


def _inproj_kernel(x_ref, sh_ref, sc_ref, g_ref, w_ref, cos_ref, sa_ref, sb_ref,
                   qa_ref, kaf_ref, vaf_ref, kab_ref, vab_ref,
                   qb_ref, kbf_ref, vbf_ref, kbb_ref, vbb_ref, *, width):
    h = _rms(x_ref[...]) * g_ref[...] * (1.0 + sc_ref[...]) + sh_ref[...]
    hb = h.astype(_BF)
    reps = width // LANES
    cos = jnp.tile(cos_ref[...], (1, reps))
    sa = jnp.tile(sa_ref[...], (1, reps))
    sb = jnp.tile(sb_ref[...], (1, reps))

    def rope(z):
        return z * cos + pltpu.roll(z, width - ROT_DIM // 2, 1) * sa + pltpu.roll(z, ROT_DIM // 2, 1) * sb

    def proj(i):
        return _dot(hb, w_ref[:, i * width:(i + 1) * width])

    def store_state(ref, z):
        hd = 2 * A_DIM
        for head in range(A_HEADS):
            ref[pl.ds(head, z.shape[0], stride=A_HEADS), :] = z[:, head * hd:(head + 1) * hd]

    qa_ref[...] = (rope(proj(0)) * (A_DIM ** -0.5)).astype(_BF)
    ka = rope(proj(1))
    store_state(kaf_ref, ka)
    kab_ref[...] = ka.astype(_BF)
    va = proj(2)
    store_state(vaf_ref, va)
    vab_ref[...] = va.astype(_BF)
    qb_ref[...] = (proj(3) * (B_DIM ** -0.5)).astype(_BF)
    kb = proj(4)
    kbf_ref[...] = kb
    kbb_ref[...] = kb.astype(_BF)
    vb = proj(5)
    vbf_ref[...] = vb
    vbb_ref[...] = vb.astype(_BF)


def _row_mod_specs(per_row, tm, d, tiles_per_batch):
    if per_row:
        return pl.BlockSpec((tm, d), lambda t: (t, 0))
    return pl.BlockSpec((None, 1, d), lambda t: (t // tiles_per_batch, 0, 0))


def _in_projection(x, shift, scale, gain, w_in_bf, rope_tabs, *, per_row, seq):
    t_all, d = x.shape
    width = w_in_bf.shape[1] // 6
    tm = min(ROW_TILE, t_all)
    tiles_per_batch = max(seq // tm, 1)
    tab_tiles = rope_tabs[0].shape[0] // tm
    mod_spec = _row_mod_specs(per_row, tm, d, tiles_per_batch)
    tab_spec = pl.BlockSpec((tm, LANES), lambda t: (t % tab_tiles, 0))
    row = lambda dt: jax.ShapeDtypeStruct((t_all, width), dt)
    out_spec = pl.BlockSpec((tm, width), lambda t: (t, 0))
    dts = [_BF, _F32, _F32, _BF, _BF, _BF, _F32, _F32, _BF, _BF]
    out_shapes = [row(dt) for dt in dts]
    out_specs = [out_spec] * 10
    for i in (1, 2):
        out_shapes[i] = jax.ShapeDtypeStruct((t_all * A_HEADS, 2 * A_DIM), _F32)
        out_specs[i] = pl.BlockSpec((tm * A_HEADS, 2 * A_DIM), lambda t: (t, 0))
    return pl.pallas_call(
        functools.partial(_inproj_kernel, width=width),
        grid=(t_all // tm,),
        in_specs=[
            pl.BlockSpec((tm, d), lambda t: (t, 0)),
            mod_spec, mod_spec,
            _resident((1, d), lambda t: (0, 0)),
            _resident(w_in_bf.shape, lambda t: (0, 0)),
            tab_spec, tab_spec, tab_spec,
        ],
        out_specs=out_specs,
        out_shape=out_shapes,
        compiler_params=_params("arbitrary"),
        name="in_projection",
    )(x, shift, scale, gain, w_in_bf, *rope_tabs)


def _diff_lambda(lamv_ref, lam_init):
    v = lamv_ref[...]
    d1 = jnp.sum(v[0:1] * v[1:2], axis=1, keepdims=True)
    d2 = jnp.sum(v[2:3] * v[3:4], axis=1, keepdims=True)
    return jnp.exp(d1) - jnp.exp(d2) + lam_init


def _split_maps(q):
    lane = lax.broadcasted_iota(jnp.int32, q.shape, 1)
    zero = jnp.zeros_like(q)
    return jnp.where(lane < A_DIM, q, zero), jnp.where(lane >= A_DIM, q, zero)


def _attn_a_kernel(lamv_ref, q_ref, k_ref, v_ref, gain_ref, o_ref,
                   sa, sb, m1, l1, acc1, m2, l2, acc2, *, lam_init):
    n_full = pl.program_id(2)
    tq = q_ref.shape[0]
    q1, q2 = _split_maps(q_ref[...])
    for m_scr, l_scr, acc in ((m1, l1, acc1), (m2, l2, acc2)):
        m_scr[...] = jnp.full(m_scr.shape, MASKED, _F32)
        l_scr[...] = jnp.zeros(l_scr.shape, _F32)
        acc[...] = jnp.zeros(acc.shape, _F32)

    def rows(j):
        return pl.ds(pl.multiple_of(j * tq, tq), tq)

    def scores(j, s_scr):
        kb = k_ref[rows(j), :]
        s_scr[0] = _dot_nt(q1, kb)
        s_scr[1] = _dot_nt(q2, kb)

    def update(s, m_scr, l_scr, acc, vb):
        m_prev = m_scr[...]
        m_next = jnp.maximum(m_prev, jnp.max(s, axis=1, keepdims=True))
        p = jnp.exp(s - jnp.tile(m_next, (1, tq // LANES)))
        alpha = jnp.exp(m_prev - m_next)
        part = p[:, 0:LANES]
        for c in range(1, tq // LANES):
            part = part + p[:, c * LANES:(c + 1) * LANES]
        l_scr[...] = alpha * l_scr[...] + part
        acc[...] = acc[...] * alpha + _dot(p.astype(_BF), vb)
        m_scr[...] = m_next

    def process(j, s_scr, diagonal):
        vb = v_ref[rows(j), :]
        for mp, (m_scr, l_scr, acc) in enumerate(((m1, l1, acc1), (m2, l2, acc2))):
            s = s_scr[mp]
            if diagonal:
                qc = lax.broadcasted_iota(jnp.int32, s.shape, 0) // CHUNK
                kc = lax.broadcasted_iota(jnp.int32, s.shape, 1) // CHUNK
                s = jnp.where(kc <= qc, s, MASKED)
            update(s, m_scr, l_scr, acc, vb)

    scores(0, sa)

    def pair(i, carry):
        scores(2 * i + 1, sb)
        process(2 * i, sa, False)
        scores(2 * i + 2, sa)
        process(2 * i + 1, sb, False)
        return carry

    lax.fori_loop(0, n_full // 2, pair, 0)

    @pl.when(n_full % 2 == 1)
    def _():
        scores(n_full, sb)
        process(n_full - 1, sa, False)
        process(n_full, sb, True)

    @pl.when(n_full % 2 == 0)
    def _():
        process(n_full, sa, True)

    lam = _diff_lambda(lamv_ref, lam_init)
    d1 = jnp.sum(l1[...], axis=1, keepdims=True)
    d2 = jnp.sum(l2[...], axis=1, keepdims=True)
    o = acc1[...] / d1 - lam * (acc2[...] / d2)
    o_ref[...] = (_rms(o) * gain_ref[...] * (1.0 - lam_init)).astype(_BF)


def _attn_a_prompt(qa, ka, va, lamv, a_gain, *, batch, seq, lam_init):
    t_all, width = qa.shape
    hd = 2 * A_DIM
    tq = A_TILE
    nq = seq // tq
    return pl.pallas_call(
        functools.partial(_attn_a_kernel, lam_init=lam_init),
        grid=(batch, A_HEADS, nq),
        in_specs=[
            _resident(lamv.shape, lambda b, h, i: (0, 0)),
            pl.BlockSpec((tq, hd), lambda b, h, i: (b * nq + i, h)),
            pl.BlockSpec((seq, hd), lambda b, h, i: (b, h)),
            pl.BlockSpec((seq, hd), lambda b, h, i: (b, h)),
            _resident((1, hd), lambda b, h, i: (0, 0)),
        ],
        out_specs=pl.BlockSpec((tq, hd), lambda b, h, i: (b * nq + i, h)),
        out_shape=jax.ShapeDtypeStruct((t_all, width), _BF),
        scratch_shapes=[pltpu.VMEM((2, tq, tq), _F32)] * 2 + [pltpu.VMEM((tq, hd), _F32)] * 6,
        compiler_params=_params("arbitrary", "arbitrary", "arbitrary"),
        name="diff_attn_prompt",
    )(lamv, qa, ka, va, a_gain)


def _head_lanes(shape, head, dim):
    lane = lax.broadcasted_iota(jnp.int32, shape, 1)
    return (lane >= head * dim) & (lane < (head + 1) * dim)


def _attn_b_kernel(q_ref, k0_ref, k1_ref, k2_ref, v0_ref, v1_ref, v2_ref, bias_ref, gain_ref, o_ref):
    qi = pl.program_id(1)
    tq = q_ref.shape[0]
    k_refs = (k0_ref, k1_ref, k2_ref)
    v_refs = (v0_ref, v1_ref, v2_ref)
    pen = (jnp.where(qi >= 2, 0.0, MASKED), jnp.where(qi >= 1, 0.0, MASKED), 0.0)
    outs = []
    for pair in range(B_HEADS // 2):
        cols = slice(pair * LANES, (pair + 1) * LANES)
        qp = q_ref[:, cols]
        zero = jnp.zeros_like(qp)
        halves = []
        for sub in range(2):
            head = 2 * pair + sub
            qh = jnp.where(_head_lanes(qp.shape, sub, B_DIM), qp, zero)
            s = [_dot_nt(qh, k_refs[b][:, cols]) + bias_ref[head, :, b * tq:(b + 1) * tq] + pen[b]
                 for b in range(3)]
            m = jnp.maximum(jnp.maximum(jnp.max(s[0], axis=1, keepdims=True),
                                        jnp.max(s[1], axis=1, keepdims=True)),
                            jnp.max(s[2], axis=1, keepdims=True))
            p = [jnp.exp(sb - m) for sb in s]
            l = (jnp.sum(p[0], axis=1, keepdims=True) + jnp.sum(p[1], axis=1, keepdims=True)
                 + jnp.sum(p[2], axis=1, keepdims=True))
            o = (_dot(p[0].astype(_BF), v_refs[0][:, cols]) + _dot(p[1].astype(_BF), v_refs[1][:, cols])
                 + _dot(p[2].astype(_BF), v_refs[2][:, cols]))
            halves.append(o / l)
        outs.append(jnp.where(_head_lanes(halves[0].shape, 0, B_DIM), halves[0], halves[1]))
    o = jnp.concatenate(outs, axis=1)
    o_ref[...] = (_rms(o) * gain_ref[...]).astype(_BF)


def _band_bias(table, tq):
    nk = B_REACH + tq
    qpos = np.arange(tq)[:, None]
    kpos = np.arange(nk)[None, :] - B_REACH
    qc = qpos // CHUNK
    kc = np.floor_divide(kpos, CHUNK)
    valid = (kc <= qc) & (kc >= qc - LEFT_CHUNKS)
    span = nk + tq - 1
    dist = (span - 1 - np.arange(span)) - (tq - 1)
    line = table[:, np.clip(dist, -REL_CLIP, REL_CLIP) + REL_CLIP].astype(_F32)
    line = jnp.pad(line, ((0, 0), (0, 1)))
    skew = jnp.tile(line, (1, tq))[:, :tq * span].reshape(-1, tq, span)
    bias = skew[:, :, tq - 1:]
    return jnp.where(valid[None], bias, MASKED)


def _attn_b_prompt(qb, kb, vb, rel_table, b_gain, *, batch, seq):
    t_all, width = qb.shape
    tq = B_TILE
    assert B_REACH == 2 * tq
    nq = seq // tq
    bias = _band_bias(rel_table, tq)
    qmap = lambda b, i: (b * nq + i, 0)
    kmap = lambda back: (lambda b, i: (b * nq + jnp.maximum(i - back, 0), 0))
    blk = lambda f: pl.BlockSpec((tq, width), f)
    return pl.pallas_call(
        _attn_b_kernel,
        grid=(batch, nq),
        in_specs=[blk(qmap), blk(kmap(2)), blk(kmap(1)), blk(kmap(0)),
                  blk(kmap(2)), blk(kmap(1)), blk(kmap(0)),
                  _resident(bias.shape, lambda b, i: (0, 0, 0)),
                  _resident((1, width), lambda b, i: (0, 0))],
        out_specs=blk(qmap),
        out_shape=jax.ShapeDtypeStruct((t_all, width), _BF),
        compiler_params=_params("arbitrary", "arbitrary"),
        name="band_attn_prompt",
    )(qb, kb, kb, kb, vb, vb, vb, bias, b_gain)


def _attn_a_sample_kernel(lamv_ref, q_ref, kc_ref, vc_ref, kn_ref, vn_ref, gain_ref, o_ref, *, lam_init):
    q = q_ref[...]
    t = q.shape[0]
    hd = 2 * A_DIM
    past = kc_ref.shape[0] // A_HEADS
    lam = _diff_lambda(lamv_ref, lam_init)
    outs = []
    for h in range(A_HEADS):
        cols = slice(h * hd, (h + 1) * hd)
        head_rows = pl.ds(h, past, stride=A_HEADS)
        new_rows = pl.ds(h, t, stride=A_HEADS)
        q12 = jnp.concatenate(_split_maps(q[:, cols]), axis=0)
        kn = kn_ref[new_rows, :].astype(_BF)
        vn = vn_ref[new_rows, :].astype(_BF)
        s_c = _dot_nt(q12, kc_ref[head_rows, :].astype(_BF))
        s_n = _dot_nt(q12, kn)
        m = jnp.maximum(jnp.max(s_c, axis=1, keepdims=True), jnp.max(s_n, axis=1, keepdims=True))
        p_c = jnp.exp(s_c - m)
        p_n = jnp.exp(s_n - m)
        l = jnp.sum(p_c, axis=1, keepdims=True) + jnp.sum(p_n, axis=1, keepdims=True)
        p_c = p_c / l
        p_n = p_n / l
        a_c = p_c[0:t] - lam * p_c[t:2 * t]
        a_n = p_n[0:t] - lam * p_n[t:2 * t]
        o = _dot(a_c.astype(_BF), vc_ref[head_rows, :].astype(_BF)) + _dot(a_n.astype(_BF), vn)
        outs.append(_rms(o) * gain_ref[...] * (1.0 - lam_init))
    o_ref[...] = jnp.concatenate(outs, axis=1).astype(_BF)


def _attn_a_sample(qa, cache_k, cache_v, kn, vn, lamv, a_gain, *, layer, lam_init, t_new):
    t_all, width = qa.shape
    _, nb, cache_rows, hd = cache_k.shape
    tok = pl.BlockSpec((t_new, width), lambda b: (b, 0))
    new = pl.BlockSpec((t_new * A_HEADS, hd), lambda b: (b, 0))
    cache = pl.BlockSpec((None, None, cache_rows, hd), lambda b: (layer, b, 0, 0))
    return pl.pallas_call(
        functools.partial(_attn_a_sample_kernel, lam_init=lam_init),
        grid=(nb,),
        in_specs=[_resident(lamv.shape, lambda b: (0, 0)), tok, cache, cache, new, new,
                  _resident((1, 2 * A_DIM), lambda b: (0, 0))],
        out_specs=tok,
        out_shape=jax.ShapeDtypeStruct((t_all, width), _BF),
        compiler_params=_params("arbitrary"),
        name="diff_attn_sample",
    )(lamv, qa, cache_k, cache_v, kn, vn, a_gain)


def _attn_b_sample_kernel(q_ref, kc_ref, vc_ref, kn_ref, vn_ref, bc_ref, bn_ref, gain_ref,
                          o_ref, ko_ref, vo_ref):
    q = q_ref[...]
    t = q.shape[0]
    keep = kc_ref.shape[0] - t
    zero = jnp.zeros_like(q)
    qall = jnp.concatenate(
        [jnp.where(_head_lanes(q.shape, h, B_DIM), q, zero) for h in range(B_HEADS)], axis=0)
    kc = kc_ref[...]
    vc = vc_ref[...]
    kn = kn_ref[...]
    vn = vn_ref[...]
    s_c = _dot_nt(qall, kc.astype(_BF)) + bc_ref[...]
    s_n = _dot_nt(qall, kn.astype(_BF)) + bn_ref[...]
    m = jnp.maximum(jnp.max(s_c, axis=1, keepdims=True), jnp.max(s_n, axis=1, keepdims=True))
    p_c = jnp.exp(s_c - m)
    p_n = jnp.exp(s_n - m)
    l = jnp.sum(p_c, axis=1, keepdims=True) + jnp.sum(p_n, axis=1, keepdims=True)
    o_all = (_dot(p_c.astype(_BF), vc.astype(_BF)) + _dot(p_n.astype(_BF), vn.astype(_BF))) / l
    o = jnp.zeros((t, q.shape[1]), _F32)
    for h in range(B_HEADS):
        o = jnp.where(_head_lanes(o.shape, h, B_DIM), o_all[h * t:(h + 1) * t], o)
    o_ref[...] = (_rms(o) * gain_ref[...]).astype(_BF)
    ko_ref[0:keep, :] = kc[t:, :]
    ko_ref[keep:, :] = kn
    vo_ref[0:keep, :] = vc[t:, :]
    vo_ref[keep:, :] = vn


def _sample_bias(table, t_new, wb):
    qpos = jnp.arange(t_new)[:, None] + wb
    kpos = jnp.arange(wb + t_new)[None, :]
    d = jnp.clip(qpos - kpos, -REL_CLIP, REL_CLIP) + REL_CLIP
    bias = table[:, d].astype(_F32).reshape(B_HEADS * t_new, wb + t_new)
    return bias[:, :wb], bias[:, wb:]


def _attn_b_sample(qb, cache_k, cache_v, kn, vn, rel_table, b_gain, *, layer, t_new):
    t_all, width = qb.shape
    _, nb, wb, _ = cache_k.shape
    bias_c, bias_n = _sample_bias(rel_table, t_new, wb)
    tok = pl.BlockSpec((t_new, width), lambda b: (b, 0))
    cache = pl.BlockSpec((None, None, wb, width), lambda b: (layer, b, 0, 0))
    state = pl.BlockSpec((None, wb, width), lambda b: (b, 0, 0))
    return pl.pallas_call(
        _attn_b_sample_kernel,
        grid=(nb,),
        in_specs=[tok, cache, cache, tok, tok,
                  _resident(bias_c.shape, lambda b: (0, 0)), _resident(bias_n.shape, lambda b: (0, 0)),
                  _resident((1, width), lambda b: (0, 0))],
        out_specs=[tok, state, state],
        out_shape=[jax.ShapeDtypeStruct((t_all, width), _BF),
                   jax.ShapeDtypeStruct((nb, wb, width), _F32),
                   jax.ShapeDtypeStruct((nb, wb, width), _F32)],
        compiler_params=_params("arbitrary"),
        name="band_attn_sample",
    )(qb, cache_k, cache_v, kn, vn, bias_c, bias_n, b_gain)


def _outproj_kernel(x_ref, oa_ref, ob_ref, w_ref, gt_ref, sh_ref, sc_ref, g_ref, x1_ref, ht_ref):
    half = oa_ref.shape[1]
    y = _dot(oa_ref[...], w_ref[0:half, :]) + _dot(ob_ref[...], w_ref[half:, :])
    x1 = x_ref[...] + gt_ref[...] * y
    x1_ref[...] = x1
    h = _rms(x1) * g_ref[...] * (1.0 + sc_ref[...]) + sh_ref[...]
    ht_ref[...] = h.T.astype(_BF)


def _out_projection(x, oa, ob, w_out_bf, gate, shift, scale, gain, *, per_row, seq):
    t_all, d = x.shape
    half = oa.shape[1]
    tm = min(PEER_TOKENS, t_all)
    mod_spec = _row_mod_specs(per_row, tm, d, max(seq // tm, 1))
    rows = lambda w: pl.BlockSpec((tm, w), lambda t: (t, 0))
    return pl.pallas_call(
        _outproj_kernel,
        grid=(t_all // tm,),
        in_specs=[rows(d), rows(half), rows(half), _resident(w_out_bf.shape, lambda t: (0, 0)),
                  mod_spec, mod_spec, mod_spec, _resident((1, d), lambda t: (0, 0))],
        out_specs=[rows(d), pl.BlockSpec((None, d, tm), lambda t: (t, 0, 0))],
        out_shape=[jax.ShapeDtypeStruct((t_all, d), _F32),
                   jax.ShapeDtypeStruct((t_all // tm, d, tm), _BF)],
        compiler_params=_params("arbitrary"),
        name="out_projection",
    )(x, oa, ob, w_out_bf, gate, shift, scale, gain)


def _top_values(s, k, want_rank=False):
    vals = []
    rank = jnp.full(s.shape, float(k), _F32)
    for i in range(k):
        m = jnp.max(s, axis=0, keepdims=True)
        vals.append(m)
        hit = s == m
        if want_rank:
            rank = jnp.where(hit, float(i), rank)
        s = jnp.where(hit, -jnp.inf, s)
    return (vals, rank) if want_rank else vals


def _stack_rows(vals):
    rows = lax.broadcasted_iota(jnp.int32, (len(vals), LANES), 0)
    out = jnp.zeros((len(vals), LANES), _F32)
    for i, v in enumerate(vals):
        out = jnp.where(rows == i, v, out)
    return out


SUBLANES = 8


def _merge_sort_network(lo, hi):
    def merge(lo, hi, r):
        step = 2 * r
        if step < hi - lo:
            yield from merge(lo, hi, step)
            yield from merge(lo + r, hi, step)
            yield from ((i, i + r) for i in range(lo + r, hi - r, step))
        else:
            yield (lo, lo + r)

    if hi - lo >= 1:
        mid = lo + (hi - lo) // 2
        yield from _merge_sort_network(lo, mid)
        yield from _merge_sort_network(mid + 1, hi)
        yield from merge(lo, hi, 1)


def _compare_exchange(v, i, j):
    a, b = v[i], v[j]
    if b is None:
        return
    if a is None:
        v[i], v[j] = b, None
        return
    v[i], v[j] = jnp.maximum(a, b), jnp.minimum(a, b)


def _top16_replicated(tiles):
    n = PK_TOPK
    v = list(tiles) + [None] * (n - len(tiles))
    for i, j in _merge_sort_network(0, n - 1):
        _compare_exchange(v, i, j)
    shift = SUBLANES // 2
    while shift:
        w = [None if x is None else pltpu.roll(x, shift, 0) for x in v]
        c = []
        for k in range(n):
            a, b = v[k], w[n - 1 - k]
            c.append(b if a is None else a if b is None else jnp.maximum(a, b))
        stride = n // 2
        while stride:
            for i in range(n):
                if not i & stride:
                    _compare_exchange(c, i, i + stride)
            stride //= 2
        v = c
        shift //= 2
    return [jnp.full((SUBLANES, LANES), -jnp.inf, _F32) if x is None else x for x in v]


def _route_kernel(ht_ref, wq_ref, keys_ref, n1_ref, e1_ref, r2_ref, e2_ref, q_scr, s_scr):
    tt = ht_ref.shape[1]
    q_scr[...] = _dot(wq_ref[...], ht_ref[...]).astype(_BF)
    for hm in range(2 * PK_HEADS):
        s_scr[hm] = _dot(keys_ref[hm], q_scr[hm * PK_HALF:(hm + 1) * PK_HALF, :])

    def head(h, carry):
        for c in range(tt // LANES):
            cols = slice(c * LANES, (c + 1) * LANES)
            s1 = s_scr[2 * h, :, cols]
            s2 = s_scr[2 * h + 1, :, cols]
            a = _top16_replicated([s1[r:r + SUBLANES] for r in range(0, N_KEYS, SUBLANES)])
            b, rank2 = _top_values(s2, PK_TOPK, want_rank=True)
            bmat = _stack_rows(b)
            b_lo, b_hi = bmat[0:SUBLANES], bmat[SUBLANES:]
            cand = [a[0] + b_lo, a[0] + b_hi]
            cand += [a[p] + b_lo for p in range(1, 8)]
            cand += [_stack_rows([a[p][0:1] for p in range(8, PK_TOPK)]) + b[0]]
            top = _top16_replicated(cand)
            tau = top[PK_TOPK - 1]
            z = jnp.ones_like(tau)
            for k in range(1, PK_TOPK):
                z = z + jnp.exp(top[k] - top[0])
            z_row = z[0:1]
            count1 = jnp.zeros(s1.shape, _F32)
            for p in range(PK_TOPK):
                hit = jnp.where(a[p] + b_lo >= tau, 1.0, 0.0) + jnp.where(a[p] + b_hi >= tau, 1.0, 0.0)
                reach = jnp.sum(hit, axis=0, keepdims=True)
                count1 = jnp.where(s1 == jnp.tile(a[p], (N_KEYS // SUBLANES, 1)), reach, count1)
            n1_ref[h, :, cols] = count1
            e1_ref[h, :, cols] = jnp.exp(s1 - a[0][0:1]) * (0.5 / z_row)
            r2_ref[h, :, cols] = rank2.astype(_BF)
            e2_ref[h, :, cols] = jnp.exp(s2 - b[0]).astype(_BF)
        return carry

    lax.fori_loop(0, PK_HEADS, head, 0)


def _peer_route(ht, wq_t_bf, keys_bf):
    slabs, d, slab = ht.shape
    t_all = slabs * slab
    tt = min(ROUTE_TOKENS, slab)
    per_slab = slab // tt
    nq = wq_t_bf.shape[0]
    out = lambda dt: jax.ShapeDtypeStruct((PK_HEADS, N_KEYS, t_all), dt)
    tile = pl.BlockSpec((PK_HEADS, N_KEYS, tt), lambda t: (0, 0, t))
    return pl.pallas_call(
        _route_kernel,
        grid=(t_all // tt,),
        in_specs=[pl.BlockSpec((None, d, tt), lambda t: (t // per_slab, 0, t % per_slab)),
                  _resident(wq_t_bf.shape, lambda t: (0, 0)),
                  _resident(keys_bf.shape, lambda t: (0, 0, 0))],
        out_specs=[tile] * 4,
        out_shape=[out(_F32), out(_F32), out(_BF), out(_BF)],
        scratch_shapes=[pltpu.VMEM((nq, tt), _BF), pltpu.VMEM((2 * PK_HEADS, N_KEYS, tt), _F32)],
        compiler_params=_params("arbitrary"),
        name="peer_route",
    )(ht, wq_t_bf, keys_bf)


def _twice_gelu_tanh(x):
    return x * (1.0 + jnp.tanh(x * (0.7978845608028654 + (0.7978845608028654 * 0.044715) * (x * x))))


def _row_to_tile(row, rows):
    packed = jnp.broadcast_to(row, (16, LANES)).astype(_BF)
    return jnp.tile(packed, (rows // 16, 1))


def _peer_kernel(ht_ref, u_ref, vt_ref, n1_ref, e1_ref, r2_ref, e2_ref, x_ref, gt_ref, gf_ref,
                 o_ref, at_scr, wt_scr, acc, r2_scr, e2_scr, *, final_norm):
    e = pl.program_id(1)
    tt = ht_ref.shape[1]
    groups = u_ref.shape[0] // N_KEYS
    span = at_scr.shape[2]
    n_chunks = tt // span

    @pl.when(e == 0)
    def _():
        acc[...] = jnp.zeros(acc.shape, _F32)
        r2_scr[...] = r2_ref[...]
        e2_scr[...] = e2_ref[...]

    for c in range(n_chunks):
        at_scr[c] = _dot(u_ref[...], ht_ref[:, c * span:(c + 1) * span])
    zero = jnp.zeros((N_KEYS, LANES), _BF)

    for c in range(n_chunks):
        for sub in range(span // LANES):
            cols = slice(c * span + sub * LANES, c * span + (sub + 1) * LANES)
            sub_cols = slice(sub * LANES, (sub + 1) * LANES)
            for il in range(groups):
                rows = slice(il * N_KEYS, (il + 1) * N_KEYS)
                g = None
                for h in range(PK_HEADS):
                    count = _row_to_tile(n1_ref[h, il:il + 1, cols], N_KEYS)
                    sel = jnp.where(r2_scr[h, :, cols] < count, e2_scr[h, :, cols], zero)
                    term = sel * _row_to_tile(e1_ref[h, il:il + 1, cols], N_KEYS)
                    g = term if g is None else g + term
                act = _twice_gelu_tanh(at_scr[c, rows, sub_cols]).astype(_BF)
                wt_scr[c, rows, sub_cols] = g * act
        acc[c] += _dot(vt_ref[...], wt_scr[c])

    @pl.when(e == pl.num_programs(1) - 1)
    def _():
        f = jnp.concatenate([acc[c].T for c in range(n_chunks)], axis=0)
        y = x_ref[...] + gt_ref[...] * f
        if final_norm:
            y = _rms(y) * gf_ref[...]
        o_ref[...] = y


def _peer_experts(ht, u_bf, vt_bf, route, x1, gate, g_final, *, per_row, seq, final_norm):
    tiles, d, tt = ht.shape
    steps, _, eb = vt_bf.shape
    t_all = tiles * tt
    span = min(PEER_CHUNK, tt)
    groups = eb // N_KEYS
    assert groups == 8
    n1, e1, r2, e2 = route
    by_step = lambda a: a.reshape(PK_HEADS, N_KEYS // groups, groups, t_all)
    gate_spec = (pl.BlockSpec((tt, d), lambda t, e: (t, 0)) if per_row else
                 pl.BlockSpec((None, 1, d), lambda t, e: (t // max(seq // tt, 1), 0, 0)))
    step_rows = pl.BlockSpec((PK_HEADS, None, groups, tt), lambda t, e: (0, e, 0, t))
    tile = _resident((PK_HEADS, N_KEYS, tt), lambda t, e: (0, 0, t))
    return pl.pallas_call(
        functools.partial(_peer_kernel, final_norm=final_norm),
        grid=(tiles, steps),
        in_specs=[pl.BlockSpec((None, d, tt), lambda t, e: (t, 0, 0)),
                  pl.BlockSpec((eb, d), lambda t, e: (e, 0)),
                  pl.BlockSpec((None, d, eb), lambda t, e: (e, 0, 0)),
                  step_rows, step_rows, tile, tile,
                  _resident((tt, d), lambda t, e: (t, 0)),
                  gate_spec,
                  _resident((1, d), lambda t, e: (0, 0))],
        out_specs=pl.BlockSpec((tt, d), lambda t, e: (t, 0)),
        out_shape=jax.ShapeDtypeStruct((t_all, d), _F32),
        scratch_shapes=[pltpu.VMEM((tt // span, eb, span), _F32), pltpu.VMEM((tt // span, eb, span), _BF),
                        pltpu.VMEM((tt // span, d, span), _F32),
                        pltpu.VMEM((PK_HEADS, N_KEYS, tt), _BF), pltpu.VMEM((PK_HEADS, N_KEYS, tt), _BF)],
        compiler_params=_params("arbitrary", "arbitrary"),
        name="peer_experts",
    )(ht, u_bf, vt_bf, by_step(n1), by_step(e1), r2, e2, x1, gate, g_final)


def _rope_tables(pos):
    half = ROT_DIM // 2
    inv = ROPE_THETA ** (-jnp.arange(0, ROT_DIM, 2, dtype=_F32) / ROT_DIM)
    ang = pos.astype(_F32)[:, None] * inv[None, :]
    cos, sin = jnp.cos(ang), jnp.sin(ang)
    n = pos.shape[0]
    pad = jnp.zeros((n, A_DIM - ROT_DIM), _F32)
    zeros = jnp.zeros((n, half), _F32)
    c = jnp.concatenate([cos, cos, pad + 1.0], axis=1)
    sa = jnp.concatenate([-sin, zeros, pad], axis=1)
    sb = jnp.concatenate([zeros, sin, pad], axis=1)
    return tuple(jnp.tile(t, (1, LANES // A_DIM)) for t in (c, sa, sb))


def _stream_layer(x, mod, lw, rope_tabs, mixer, *, per_row, seq, final_norm, g_final):
    shift1, scale1, gate1, shift2, scale2, gate2 = mod
    proj = _in_projection(x, shift1, scale1, lw["g_attn"], lw["w_in"], rope_tabs, per_row=per_row, seq=seq)
    qa, kaf, vaf, kab, vab, qb, kbf, vbf, kbb, vbb = proj
    oa, ob, extra = mixer(qa, kaf, vaf, kab, vab, qb, kbf, vbf, kbb, vbb)
    x1, ht = _out_projection(x, oa, ob, lw["w_out"], gate1, shift2, scale2, lw["g_ffn"],
                             per_row=per_row, seq=seq)
    route = _peer_route(ht, lw["pk_wq_t"], lw["pk_keys"])
    x2 = _peer_experts(ht, lw["pk_u"], lw["pk_v_t"], route, x1, gate2, g_final,
                       per_row=per_row, seq=seq, final_norm=final_norm)
    return x2, (kaf, vaf, kbf, vbf) + extra


def kernel(x_prompt, x_sample, c_prompt, c_sample, cache_a_k, cache_a_v, cache_b_k, cache_b_v, w_ada, b_ada, g_attn, g_ffn, w_in, lam_q1, lam_k1, lam_q2, lam_k2, a_gain, rel_bias, b_gain, w_out, pk_wq, pk_keys, pk_u, pk_v, g_final):
    batch, seq, d = x_prompt.shape
    nb, t_new, _ = x_sample.shape
    depth = w_ada.shape[0]
    past = cache_a_k.shape[2]
    wb = cache_b_k.shape[2]
    a_width = A_HEADS * 2 * A_DIM
    b_width = B_HEADS * B_DIM

    c_all = jnp.concatenate([c_prompt, c_sample], axis=0)
    pad_rows = (-c_all.shape[0]) % 16
    c_all = jnp.pad(c_all, ((0, pad_rows), (0, 0)))
    mod = _modulation(c_all, w_ada, b_ada)

    tabs_p = _rope_tables(jnp.arange(seq, dtype=jnp.int32))
    pos_s = past + jnp.arange(t_new, dtype=jnp.int32)
    tabs_s = tuple(jnp.tile(t, (nb, 1)) for t in _rope_tables(pos_s))

    cak = cache_a_k.reshape(depth, nb, past * A_HEADS, 2 * A_DIM)
    cav = cache_a_v.reshape(depth, nb, past * A_HEADS, 2 * A_DIM)
    cbk = cache_b_k.reshape(depth, nb, wb, b_width)
    cbv = cache_b_v.reshape(depth, nb, wb, b_width)

    xp = x_prompt.reshape(batch * seq, d)
    xs = x_sample.reshape(nb * t_new, d)
    g_fin = g_final.reshape(1, d)
    st_p, st_s = [], []
    for l in range(depth):
        lam_init = 0.8 - 0.6 * math.exp(-0.3 * l)
        lamv = jnp.stack([lam_q1[l], lam_k1[l], lam_q2[l], lam_k2[l]])
        lw = {
            "g_attn": g_attn[l].reshape(1, d),
            "g_ffn": g_ffn[l].reshape(1, d),
            "w_in": w_in[l].astype(_BF),
            "w_out": w_out[l].astype(_BF),
            "pk_wq_t": pk_wq[l].T.astype(_BF),
            "pk_keys": pk_keys[l].reshape(2 * PK_HEADS, N_KEYS, PK_HALF).astype(_BF),
            "pk_u": pk_u[l].astype(_BF),
            "pk_v_t": pk_v[l].astype(_BF).reshape(-1, PEER_EXPERTS, d).transpose(0, 2, 1),
        }
        gain_a = a_gain[l].reshape(1, 2 * A_DIM)
        gain_b = b_gain[l].reshape(1, b_width)
        mod_p = tuple(m.reshape(batch, 1, d) for m in jnp.split(mod[l, :batch], 6, axis=-1))
        mod_s = tuple(jnp.repeat(m, t_new, axis=0) for m in jnp.split(mod[l, batch:batch + nb], 6, axis=-1))
        last = l == depth - 1

        def mix_p(qa, kaf, vaf, kab, vab, qb, kbf, vbf, kbb, vbb):
            oa = _attn_a_prompt(qa, kab, vab, lamv, gain_a, batch=batch, seq=seq, lam_init=lam_init)
            ob = _attn_b_prompt(qb, kbb, vbb, rel_bias[l], gain_b, batch=batch, seq=seq)
            return oa, ob, ()

        def mix_s(qa, kaf, vaf, kab, vab, qb, kbf, vbf, kbb, vbb):
            oa = _attn_a_sample(qa, cak, cav, kaf, vaf, lamv, gain_a, layer=l, lam_init=lam_init, t_new=t_new)
            ob, nbk, nbv = _attn_b_sample(qb, cbk, cbv, kbf, vbf, rel_bias[l], gain_b, layer=l, t_new=t_new)
            return oa, ob, (nbk, nbv)

        xp, sp = _stream_layer(xp, mod_p, lw, tabs_p, mix_p, per_row=False, seq=seq,
                               final_norm=last, g_final=g_fin)
        xs, ss = _stream_layer(xs, mod_s, lw, tabs_s, mix_s, per_row=True, seq=t_new,
                               final_norm=last, g_final=g_fin)
        st_p.append(sp)
        st_s.append(ss)

    keep = min(B_REACH, seq)
    a_shape_p = (batch, seq, A_HEADS, 2 * A_DIM)
    b_tail = lambda s: s.reshape(batch, seq, B_HEADS, B_DIM)[:, seq - keep:]
    a_shape_s = (nb, t_new, A_HEADS, 2 * A_DIM)
    b_shape_s = (nb, wb, B_HEADS, B_DIM)
    return (
        xp.reshape(batch, seq, d),
        xs.reshape(nb, t_new, d),
        jnp.stack([s[0].reshape(a_shape_p) for s in st_p]),
        jnp.stack([s[1].reshape(a_shape_p) for s in st_p]),
        jnp.stack([b_tail(s[2]) for s in st_p]),
        jnp.stack([b_tail(s[3]) for s in st_p]),
        jnp.stack([s[0].reshape(a_shape_s) for s in st_s]),
        jnp.stack([s[1].reshape(a_shape_s) for s in st_s]),
        jnp.stack([s[4].reshape(b_shape_s) for s in st_s]),
        jnp.stack([s[5].reshape(b_shape_s) for s in st_s]),
    )
```

```python
import functools
import math

import jax
import jax.numpy as jnp
import numpy as np
from jax import lax
from jax.experimental import pallas as pl
from jax.experimental.pallas import tpu as pltpu

_F32 = jnp.float32
_BF = jnp.bfloat16

CHUNK = 64
A_HEADS = 4
A_DIM = 64
ROT_DIM = A_DIM // 4
ROPE_THETA = 500000.0
B_HEADS = 8
B_DIM = 64
LEFT_CHUNKS = 8
B_REACH = LEFT_CHUNKS * CHUNK
REL_CLIP = 128
PK_HEADS = 8
N_KEYS = 128
PK_TOPK = 16
PK_HALF = 128
EPS = 1e-6

LANES = 128
MASKED = -1e30
VMEM_LIMIT = 56 * 1024 * 1024

A_TILE = 512
B_TILE = 256
PEER_TOKENS = 1024
ROUTE_TOKENS = 512
PEER_EXPERTS = 1024
PEER_CHUNK = 256
ROW_TILE = 512


def _params(*sem):
    return pltpu.CompilerParams(dimension_semantics=sem, vmem_limit_bytes=VMEM_LIMIT)


def _resident(shape, index_map):
    return pl.BlockSpec(shape, index_map, pipeline_mode=pl.Buffered(1))


def _rms(x):
    return x * lax.rsqrt(jnp.mean(x * x, axis=-1, keepdims=True) + EPS)


def _dot(a, b):
    return jnp.dot(a, b, preferred_element_type=_F32)


def _dot_nt(a, b):
    return lax.dot_general(a, b, (((1,), (1,)), ((), ())), preferred_element_type=_F32)


def _mod_kernel(c_ref, w_ref, b_ref, o_ref):
    c = c_ref[...]
    s = c * (1.0 / (1.0 + jnp.exp(-c)))
    o_ref[...] = _dot(s.astype(_BF), w_ref[...].astype(_BF)) + b_ref[...]


def _modulation(c_all, w_ada, b_ada):
    depth, d, n = w_ada.shape
    rows = c_all.shape[0]
    tn = n // 4
    return pl.pallas_call(
        _mod_kernel,
        grid=(depth, n // tn),
        in_specs=[
            pl.BlockSpec((rows, d), lambda l, j: (0, 0)),
            pl.BlockSpec((None, d, tn), lambda l, j: (l, 0, j)),
            pl.BlockSpec((None, 1, tn), lambda l, j: (l, 0, j)),
        ],
        out_specs=pl.BlockSpec((None, rows, tn), lambda l, j: (l, 0, j)),
        out_shape=jax.ShapeDtypeStruct((depth, rows, n), _F32),
        compiler_params=_params("arbitrary", "arbitrary"),
        name="adaln_mod",
    )(c_all, w_ada, b_ada.reshape(depth, 1, n))


def _inproj_kernel(x_ref, sh_ref, sc_ref, g_ref, w_ref, cos_ref, sa_ref, sb_ref,
                   qa_ref, kaf_ref, vaf_ref, kab_ref, vab_ref,
                   qb_ref, kbf_ref, vbf_ref, kbb_ref, vbb_ref, *, width):
    h = _rms(x_ref[...]) * g_ref[...] * (1.0 + sc_ref[...]) + sh_ref[...]
    hb = h.astype(_BF)
    reps = width // LANES
    cos = jnp.tile(cos_ref[...], (1, reps))
    sa = jnp.tile(sa_ref[...], (1, reps))
    sb = jnp.tile(sb_ref[...], (1, reps))

    def rope(z):
        return z * cos + pltpu.roll(z, width - ROT_DIM // 2, 1) * sa + pltpu.roll(z, ROT_DIM // 2, 1) * sb

    def proj(i):
        return _dot(hb, w_ref[:, i * width:(i + 1) * width])

    def store_state(ref, z):
        hd = 2 * A_DIM
        for head in range(A_HEADS):
            ref[pl.ds(head, z.shape[0], stride=A_HEADS), :] = z[:, head * hd:(head + 1) * hd]

    qa_ref[...] = (rope(proj(0)) * (A_DIM ** -0.5)).astype(_BF)
    ka = rope(proj(1))
    store_state(kaf_ref, ka)
    kab_ref[...] = ka.astype(_BF)
    va = proj(2)
    store_state(vaf_ref, va)
    vab_ref[...] = va.astype(_BF)
    qb_ref[...] = (proj(3) * (B_DIM ** -0.5)).astype(_BF)
    kb = proj(4)
    kbf_ref[...] = kb
    kbb_ref[...] = kb.astype(_BF)
    vb = proj(5)
    vbf_ref[...] = vb
    vbb_ref[...] = vb.astype(_BF)


def _row_mod_specs(per_row, tm, d, tiles_per_batch):
    if per_row:
        return pl.BlockSpec((tm, d), lambda t: (t, 0))
    return pl.BlockSpec((None, 1, d), lambda t: (t // tiles_per_batch, 0, 0))


def _in_projection(x, shift, scale, gain, w_in_bf, rope_tabs, *, per_row, seq):
    t_all, d = x.shape
    width = w_in_bf.shape[1] // 6
    tm = min(ROW_TILE, t_all)
    tiles_per_batch = max(seq // tm, 1)
    tab_tiles = rope_tabs[0].shape[0] // tm
    mod_spec = _row_mod_specs(per_row, tm, d, tiles_per_batch)
    tab_spec = pl.BlockSpec((tm, LANES), lambda t: (t % tab_tiles, 0))
    row = lambda dt: jax.ShapeDtypeStruct((t_all, width), dt)
    out_spec = pl.BlockSpec((tm, width), lambda t: (t, 0))
    dts = [_BF, _F32, _F32, _BF, _BF, _BF, _F32, _F32, _BF, _BF]
    out_shapes = [row(dt) for dt in dts]
    out_specs = [out_spec] * 10
    for i in (1, 2):
        out_shapes[i] = jax.ShapeDtypeStruct((t_all * A_HEADS, 2 * A_DIM), _F32)
        out_specs[i] = pl.BlockSpec((tm * A_HEADS, 2 * A_DIM), lambda t: (t, 0))
    return pl.pallas_call(
        functools.partial(_inproj_kernel, width=width),
        grid=(t_all // tm,),
        in_specs=[
            pl.BlockSpec((tm, d), lambda t: (t, 0)),
            mod_spec, mod_spec,
            _resident((1, d), lambda t: (0, 0)),
            _resident(w_in_bf.shape, lambda t: (0, 0)),
            tab_spec, tab_spec, tab_spec,
        ],
        out_specs=out_specs,
        out_shape=out_shapes,
        compiler_params=_params("arbitrary"),
        name="in_projection",
    )(x, shift, scale, gain, w_in_bf, *rope_tabs)


def _diff_lambda(lamv_ref, lam_init):
    v = lamv_ref[...]
    d1 = jnp.sum(v[0:1] * v[1:2], axis=1, keepdims=True)
    d2 = jnp.sum(v[2:3] * v[3:4], axis=1, keepdims=True)
    return jnp.exp(d1) - jnp.exp(d2) + lam_init


def _split_maps(q):
    lane = lax.broadcasted_iota(jnp.int32, q.shape, 1)
    zero = jnp.zeros_like(q)
    return jnp.where(lane < A_DIM, q, zero), jnp.where(lane >= A_DIM, q, zero)


def _attn_a_kernel(lamv_ref, q_ref, k_ref, v_ref, gain_ref, o_ref,
                   sa, sb, m1, l1, acc1, m2, l2, acc2, *, lam_init):
    n_full = pl.program_id(2)
    tq = q_ref.shape[0]
    q1, q2 = _split_maps(q_ref[...])
    for m_scr, l_scr, acc in ((m1, l1, acc1), (m2, l2, acc2)):
        m_scr[...] = jnp.full(m_scr.shape, MASKED, _F32)
        l_scr[...] = jnp.zeros(l_scr.shape, _F32)
        acc[...] = jnp.zeros(acc.shape, _F32)

    def rows(j):
        return pl.ds(pl.multiple_of(j * tq, tq), tq)

    def scores(j, s_scr):
        kb = k_ref[rows(j), :]
        s_scr[0] = _dot_nt(q1, kb)
        s_scr[1] = _dot_nt(q2, kb)

    def update(s, m_scr, l_scr, acc, vb):
        m_prev = m_scr[...]
        m_next = jnp.maximum(m_prev, jnp.max(s, axis=1, keepdims=True))
        p = jnp.exp(s - jnp.tile(m_next, (1, tq // LANES)))
        alpha = jnp.exp(m_prev - m_next)
        part = p[:, 0:LANES]
        for c in range(1, tq // LANES):
            part = part + p[:, c * LANES:(c + 1) * LANES]
        l_scr[...] = alpha * l_scr[...] + part
        acc[...] = acc[...] * alpha + _dot(p.astype(_BF), vb)
        m_scr[...] = m_next

    def process(j, s_scr, diagonal):
        vb = v_ref[rows(j), :]
        for mp, (m_scr, l_scr, acc) in enumerate(((m1, l1, acc1), (m2, l2, acc2))):
            s = s_scr[mp]
            if diagonal:
                qc = lax.broadcasted_iota(jnp.int32, s.shape, 0) // CHUNK
                kc = lax.broadcasted_iota(jnp.int32, s.shape, 1) // CHUNK
                s = jnp.where(kc <= qc, s, MASKED)
            update(s, m_scr, l_scr, acc, vb)

    scores(0, sa)

    def pair(i, carry):
        scores(2 * i + 1, sb)
        process(2 * i, sa, False)
        scores(2 * i + 2, sa)
        process(2 * i + 1, sb, False)
        return carry

    lax.fori_loop(0, n_full // 2, pair, 0)

    @pl.when(n_full % 2 == 1)
    def _():
        scores(n_full, sb)
        process(n_full - 1, sa, False)
        process(n_full, sb, True)

    @pl.when(n_full % 2 == 0)
    def _():
        process(n_full, sa, True)

    lam = _diff_lambda(lamv_ref, lam_init)
    d1 = jnp.sum(l1[...], axis=1, keepdims=True)
    d2 = jnp.sum(l2[...], axis=1, keepdims=True)
    o = acc1[...] / d1 - lam * (acc2[...] / d2)
    o_ref[...] = (_rms(o) * gain_ref[...] * (1.0 - lam_init)).astype(_BF)


def _attn_a_prompt(qa, ka, va, lamv, a_gain, *, batch, seq, lam_init):
    t_all, width = qa.shape
    hd = 2 * A_DIM
    tq = A_TILE
    nq = seq // tq
    return pl.pallas_call(
        functools.partial(_attn_a_kernel, lam_init=lam_init),
        grid=(batch, A_HEADS, nq),
        in_specs=[
            _resident(lamv.shape, lambda b, h, i: (0, 0)),
            pl.BlockSpec((tq, hd), lambda b, h, i: (b * nq + i, h)),
            pl.BlockSpec((seq, hd), lambda b, h, i: (b, h)),
            pl.BlockSpec((seq, hd), lambda b, h, i: (b, h)),
            _resident((1, hd), lambda b, h, i: (0, 0)),
        ],
        out_specs=pl.BlockSpec((tq, hd), lambda b, h, i: (b * nq + i, h)),
        out_shape=jax.ShapeDtypeStruct((t_all, width), _BF),
        scratch_shapes=[pltpu.VMEM((2, tq, tq), _F32)] * 2 + [pltpu.VMEM((tq, hd), _F32)] * 6,
        compiler_params=_params("arbitrary", "arbitrary", "arbitrary"),
        name="diff_attn_prompt",
    )(lamv, qa, ka, va, a_gain)


def _head_lanes(shape, head, dim):
    lane = lax.broadcasted_iota(jnp.int32, shape, 1)
    return (lane >= head * dim) & (lane < (head + 1) * dim)


def _attn_b_kernel(q_ref, k0_ref, k1_ref, k2_ref, v0_ref, v1_ref, v2_ref, bias_ref, gain_ref, o_ref):
    qi = pl.program_id(1)
    tq = q_ref.shape[0]
    k_refs = (k0_ref, k1_ref, k2_ref)
    v_refs = (v0_ref, v1_ref, v2_ref)
    pen = (jnp.where(qi >= 2, 0.0, MASKED), jnp.where(qi >= 1, 0.0, MASKED), 0.0)
    outs = []
    for pair in range(B_HEADS // 2):
        cols = slice(pair * LANES, (pair + 1) * LANES)
        qp = q_ref[:, cols]
        zero = jnp.zeros_like(qp)
        halves = []
        for sub in range(2):
            head = 2 * pair + sub
            qh = jnp.where(_head_lanes(qp.shape, sub, B_DIM), qp, zero)
            s = [_dot_nt(qh, k_refs[b][:, cols]) + bias_ref[head, :, b * tq:(b + 1) * tq] + pen[b]
                 for b in range(3)]
            m = jnp.maximum(jnp.maximum(jnp.max(s[0], axis=1, keepdims=True),
                                        jnp.max(s[1], axis=1, keepdims=True)),
                            jnp.max(s[2], axis=1, keepdims=True))
            p = [jnp.exp(sb - m) for sb in s]
            l = (jnp.sum(p[0], axis=1, keepdims=True) + jnp.sum(p[1], axis=1, keepdims=True)
                 + jnp.sum(p[2], axis=1, keepdims=True))
            o = (_dot(p[0].astype(_BF), v_refs[0][:, cols]) + _dot(p[1].astype(_BF), v_refs[1][:, cols])
                 + _dot(p[2].astype(_BF), v_refs[2][:, cols]))
            halves.append(o / l)
        outs.append(jnp.where(_head_lanes(halves[0].shape, 0, B_DIM), halves[0], halves[1]))
    o = jnp.concatenate(outs, axis=1)
    o_ref[...] = (_rms(o) * gain_ref[...]).astype(_BF)


def _band_bias(table, tq):
    nk = B_REACH + tq
    qpos = np.arange(tq)[:, None]
    kpos = np.arange(nk)[None, :] - B_REACH
    qc = qpos // CHUNK
    kc = np.floor_divide(kpos, CHUNK)
    valid = (kc <= qc) & (kc >= qc - LEFT_CHUNKS)
    span = nk + tq - 1
    dist = (span - 1 - np.arange(span)) - (tq - 1)
    line = table[:, np.clip(dist, -REL_CLIP, REL_CLIP) + REL_CLIP].astype(_F32)
    line = jnp.pad(line, ((0, 0), (0, 1)))
    skew = jnp.tile(line, (1, tq))[:, :tq * span].reshape(-1, tq, span)
    bias = skew[:, :, tq - 1:]
    return jnp.where(valid[None], bias, MASKED)


def _attn_b_prompt(qb, kb, vb, rel_table, b_gain, *, batch, seq):
    t_all, width = qb.shape
    tq = B_TILE
    assert B_REACH == 2 * tq
    nq = seq // tq
    bias = _band_bias(rel_table, tq)
    qmap = lambda b, i: (b * nq + i, 0)
    kmap = lambda back: (lambda b, i: (b * nq + jnp.maximum(i - back, 0), 0))
    blk = lambda f: pl.BlockSpec((tq, width), f)
    return pl.pallas_call(
        _attn_b_kernel,
        grid=(batch, nq),
        in_specs=[blk(qmap), blk(kmap(2)), blk(kmap(1)), blk(kmap(0)),
                  blk(kmap(2)), blk(kmap(1)), blk(kmap(0)),
                  _resident(bias.shape, lambda b, i: (0, 0, 0)),
                  _resident((1, width), lambda b, i: (0, 0))],
        out_specs=blk(qmap),
        out_shape=jax.ShapeDtypeStruct((t_all, width), _BF),
        compiler_params=_params("arbitrary", "arbitrary"),
        name="band_attn_prompt",
    )(qb, kb, kb, kb, vb, vb, vb, bias, b_gain)


def _attn_a_sample_kernel(lamv_ref, q_ref, kc_ref, vc_ref, kn_ref, vn_ref, gain_ref, o_ref, *, lam_init):
    q = q_ref[...]
    t = q.shape[0]
    hd = 2 * A_DIM
    past = kc_ref.shape[0] // A_HEADS
    lam = _diff_lambda(lamv_ref, lam_init)
    outs = []
    for h in range(A_HEADS):
        cols = slice(h * hd, (h + 1) * hd)
        head_rows = pl.ds(h, past, stride=A_HEADS)
        new_rows = pl.ds(h, t, stride=A_HEADS)
        q12 = jnp.concatenate(_split_maps(q[:, cols]), axis=0)
        kn = kn_ref[new_rows, :].astype(_BF)
        vn = vn_ref[new_rows, :].astype(_BF)
        s_c = _dot_nt(q12, kc_ref[head_rows, :].astype(_BF))
        s_n = _dot_nt(q12, kn)
        m = jnp.maximum(jnp.max(s_c, axis=1, keepdims=True), jnp.max(s_n, axis=1, keepdims=True))
        p_c = jnp.exp(s_c - m)
        p_n = jnp.exp(s_n - m)
        l = jnp.sum(p_c, axis=1, keepdims=True) + jnp.sum(p_n, axis=1, keepdims=True)
        p_c = p_c / l
        p_n = p_n / l
        a_c = p_c[0:t] - lam * p_c[t:2 * t]
        a_n = p_n[0:t] - lam * p_n[t:2 * t]
        o = _dot(a_c.astype(_BF), vc_ref[head_rows, :].astype(_BF)) + _dot(a_n.astype(_BF), vn)
        outs.append(_rms(o) * gain_ref[...] * (1.0 - lam_init))
    o_ref[...] = jnp.concatenate(outs, axis=1).astype(_BF)


def _attn_a_sample(qa, cache_k, cache_v, kn, vn, lamv, a_gain, *, layer, lam_init, t_new):
    t_all, width = qa.shape
    _, nb, cache_rows, hd = cache_k.shape
    tok = pl.BlockSpec((t_new, width), lambda b: (b, 0))
    new = pl.BlockSpec((t_new * A_HEADS, hd), lambda b: (b, 0))
    cache = pl.BlockSpec((None, None, cache_rows, hd), lambda b: (layer, b, 0, 0))
    return pl.pallas_call(
        functools.partial(_attn_a_sample_kernel, lam_init=lam_init),
        grid=(nb,),
        in_specs=[_resident(lamv.shape, lambda b: (0, 0)), tok, cache, cache, new, new,
                  _resident((1, 2 * A_DIM), lambda b: (0, 0))],
        out_specs=tok,
        out_shape=jax.ShapeDtypeStruct((t_all, width), _BF),
        compiler_params=_params("arbitrary"),
        name="diff_attn_sample",
    )(lamv, qa, cache_k, cache_v, kn, vn, a_gain)


def _attn_b_sample_kernel(q_ref, kc_ref, vc_ref, kn_ref, vn_ref, bc_ref, bn_ref, gain_ref,
                          o_ref, ko_ref, vo_ref):
    q = q_ref[...]
    t = q.shape[0]
    keep = kc_ref.shape[0] - t
    zero = jnp.zeros_like(q)
    qall = jnp.concatenate(
        [jnp.where(_head_lanes(q.shape, h, B_DIM), q, zero) for h in range(B_HEADS)], axis=0)
    kc = kc_ref[...]
    vc = vc_ref[...]
    kn = kn_ref[...]
    vn = vn_ref[...]
    s_c = _dot_nt(qall, kc.astype(_BF)) + bc_ref[...]
    s_n = _dot_nt(qall, kn.astype(_BF)) + bn_ref[...]
    m = jnp.maximum(jnp.max(s_c, axis=1, keepdims=True), jnp.max(s_n, axis=1, keepdims=True))
    p_c = jnp.exp(s_c - m)
    p_n = jnp.exp(s_n - m)
    l = jnp.sum(p_c, axis=1, keepdims=True) + jnp.sum(p_n, axis=1, keepdims=True)
    o_all = (_dot(p_c.astype(_BF), vc.astype(_BF)) + _dot(p_n.astype(_BF), vn.astype(_BF))) / l
    o = jnp.zeros((t, q.shape[1]), _F32)
    for h in range(B_HEADS):
        o = jnp.where(_head_lanes(o.shape, h, B_DIM), o_all[h * t:(h + 1) * t], o)
    o_ref[...] = (_rms(o) * gain_ref[...]).astype(_BF)
    ko_ref[0:keep, :] = kc[t:, :]
    ko_ref[keep:, :] = kn
    vo_ref[0:keep, :] = vc[t:, :]
    vo_ref[keep:, :] = vn


def _sample_bias(table, t_new, wb):
    qpos = jnp.arange(t_new)[:, None] + wb
    kpos = jnp.arange(wb + t_new)[None, :]
    d = jnp.clip(qpos - kpos, -REL_CLIP, REL_CLIP) + REL_CLIP
    bias = table[:, d].astype(_F32).reshape(B_HEADS * t_new, wb + t_new)
    return bias[:, :wb], bias[:, wb:]


def _attn_b_sample(qb, cache_k, cache_v, kn, vn, rel_table, b_gain, *, layer, t_new):
    t_all, width = qb.shape
    _, nb, wb, _ = cache_k.shape
    bias_c, bias_n = _sample_bias(rel_table, t_new, wb)
    tok = pl.BlockSpec((t_new, width), lambda b: (b, 0))
    cache = pl.BlockSpec((None, None, wb, width), lambda b: (layer, b, 0, 0))
    state = pl.BlockSpec((None, wb, width), lambda b: (b, 0, 0))
    return pl.pallas_call(
        _attn_b_sample_kernel,
        grid=(nb,),
        in_specs=[tok, cache, cache, tok, tok,
                  _resident(bias_c.shape, lambda b: (0, 0)), _resident(bias_n.shape, lambda b: (0, 0)),
                  _resident((1, width), lambda b: (0, 0))],
        out_specs=[tok, state, state],
        out_shape=[jax.ShapeDtypeStruct((t_all, width), _BF),
                   jax.ShapeDtypeStruct((nb, wb, width), _F32),
                   jax.ShapeDtypeStruct((nb, wb, width), _F32)],
        compiler_params=_params("arbitrary"),
        name="band_attn_sample",
    )(qb, cache_k, cache_v, kn, vn, bias_c, bias_n, b_gain)


def _outproj_kernel(x_ref, oa_ref, ob_ref, w_ref, gt_ref, sh_ref, sc_ref, g_ref, x1_ref, ht_ref):
    half = oa_ref.shape[1]
    y = _dot(oa_ref[...], w_ref[0:half, :]) + _dot(ob_ref[...], w_ref[half:, :])
    x1 = x_ref[...] + gt_ref[...] * y
    x1_ref[...] = x1
    h = _rms(x1) * g_ref[...] * (1.0 + sc_ref[...]) + sh_ref[...]
    ht_ref[...] = h.T.astype(_BF)


def _out_projection(x, oa, ob, w_out_bf, gate, shift, scale, gain, *, per_row, seq):
    t_all, d = x.shape
    half = oa.shape[1]
    tm = min(PEER_TOKENS, t_all)
    mod_spec = _row_mod_specs(per_row, tm, d, max(seq // tm, 1))
    rows = lambda w: pl.BlockSpec((tm, w), lambda t: (t, 0))
    return pl.pallas_call(
        _outproj_kernel,
        grid=(t_all // tm,),
        in_specs=[rows(d), rows(half), rows(half), _resident(w_out_bf.shape, lambda t: (0, 0)),
                  mod_spec, mod_spec, mod_spec, _resident((1, d), lambda t: (0, 0))],
        out_specs=[rows(d), pl.BlockSpec((None, d, tm), lambda t: (t, 0, 0))],
        out_shape=[jax.ShapeDtypeStruct((t_all, d), _F32),
                   jax.ShapeDtypeStruct((t_all // tm, d, tm), _BF)],
        compiler_params=_params("arbitrary"),
        name="out_projection",
    )(x, oa, ob, w_out_bf, gate, shift, scale, gain)


def _top_values(s, k, want_rank=False):
    vals = []
    rank = jnp.full(s.shape, float(k), _F32)
    for i in range(k):
        m = jnp.max(s, axis=0, keepdims=True)
        vals.append(m)
        hit = s == m
        if want_rank:
            rank = jnp.where(hit, float(i), rank)
        s = jnp.where(hit, -jnp.inf, s)
    return (vals, rank) if want_rank else vals


def _stack_rows(vals):
    rows = lax.broadcasted_iota(jnp.int32, (len(vals), LANES), 0)
    out = jnp.zeros((len(vals), LANES), _F32)
    for i, v in enumerate(vals):
        out = jnp.where(rows == i, v, out)
    return out


SUBLANES = 8


def _merge_sort_network(lo, hi):
    def merge(lo, hi, r):
        step = 2 * r
        if step < hi - lo:
            yield from merge(lo, hi, step)
            yield from merge(lo + r, hi, step)
            yield from ((i, i + r) for i in range(lo + r, hi - r, step))
        else:
            yield (lo, lo + r)

    if hi - lo >= 1:
        mid = lo + (hi - lo) // 2
        yield from _merge_sort_network(lo, mid)
        yield from _merge_sort_network(mid + 1, hi)
        yield from merge(lo, hi, 1)


def _compare_exchange(v, i, j):
    a, b = v[i], v[j]
    if b is None:
        return
    if a is None:
        v[i], v[j] = b, None
        return
    v[i], v[j] = jnp.maximum(a, b), jnp.minimum(a, b)


def _top16_replicated(tiles):
    n = PK_TOPK
    v = list(tiles) + [None] * (n - len(tiles))
    for i, j in _merge_sort_network(0, n - 1):
        _compare_exchange(v, i, j)
    shift = SUBLANES // 2
    while shift:
        w = [None if x is None else pltpu.roll(x, shift, 0) for x in v]
        c = []
        for k in range(n):
            a, b = v[k], w[n - 1 - k]
            c.append(b if a is None else a if b is None else jnp.maximum(a, b))
        stride = n // 2
        while stride:
            for i in range(n):
                if not i & stride:
                    _compare_exchange(c, i, i + stride)
            stride //= 2
        v = c
        shift //= 2
    return [jnp.full((SUBLANES, LANES), -jnp.inf, _F32) if x is None else x for x in v]


def _route_kernel(ht_ref, wq_ref, keys_ref, n1_ref, e1_ref, r2_ref, e2_ref, q_scr, s_scr):
    tt = ht_ref.shape[1]
    q_scr[...] = _dot(wq_ref[...], ht_ref[...]).astype(_BF)
    for hm in range(2 * PK_HEADS):
        s_scr[hm] = _dot(keys_ref[hm], q_scr[hm * PK_HALF:(hm + 1) * PK_HALF, :])

    def head(h, carry):
        for c in range(tt // LANES):
            cols = slice(c * LANES, (c + 1) * LANES)
            s1 = s_scr[2 * h, :, cols]
            s2 = s_scr[2 * h + 1, :, cols]
            a = _top16_replicated([s1[r:r + SUBLANES] for r in range(0, N_KEYS, SUBLANES)])
            b, rank2 = _top_values(s2, PK_TOPK, want_rank=True)
            bmat = _stack_rows(b)
            b_lo, b_hi = bmat[0:SUBLANES], bmat[SUBLANES:]
            cand = [a[0] + b_lo, a[0] + b_hi]
            cand += [a[p] + b_lo for p in range(1, 8)]
            cand += [_stack_rows([a[p][0:1] for p in range(8, PK_TOPK)]) + b[0]]
            top = _top16_replicated(cand)
            tau = top[PK_TOPK - 1]
            z = jnp.ones_like(tau)
            for k in range(1, PK_TOPK):
                z = z + jnp.exp(top[k] - top[0])
            z_row = z[0:1]
            count1 = jnp.zeros(s1.shape, _F32)
            for p in range(PK_TOPK):
                hit = jnp.where(a[p] + b_lo >= tau, 1.0, 0.0) + jnp.where(a[p] + b_hi >= tau, 1.0, 0.0)
                reach = jnp.sum(hit, axis=0, keepdims=True)
                count1 = jnp.where(s1 == jnp.tile(a[p], (N_KEYS // SUBLANES, 1)), reach, count1)
            n1_ref[h, :, cols] = count1
            e1_ref[h, :, cols] = jnp.exp(s1 - a[0][0:1]) * (0.5 / z_row)
            r2_ref[h, :, cols] = rank2.astype(_BF)
            e2_ref[h, :, cols] = jnp.exp(s2 - b[0]).astype(_BF)
        return carry

    lax.fori_loop(0, PK_HEADS, head, 0)


def _peer_route(ht, wq_t_bf, keys_bf):
    slabs, d, slab = ht.shape
    t_all = slabs * slab
    tt = min(ROUTE_TOKENS, slab)
    per_slab = slab // tt
    nq = wq_t_bf.shape[0]
    out = lambda dt: jax.ShapeDtypeStruct((PK_HEADS, N_KEYS, t_all), dt)
    tile = pl.BlockSpec((PK_HEADS, N_KEYS, tt), lambda t: (0, 0, t))
    return pl.pallas_call(
        _route_kernel,
        grid=(t_all // tt,),
        in_specs=[pl.BlockSpec((None, d, tt), lambda t: (t // per_slab, 0, t % per_slab)),
                  _resident(wq_t_bf.shape, lambda t: (0, 0)),
                  _resident(keys_bf.shape, lambda t: (0, 0, 0))],
        out_specs=[tile] * 4,
        out_shape=[out(_F32), out(_F32), out(_BF), out(_BF)],
        scratch_shapes=[pltpu.VMEM((nq, tt), _BF), pltpu.VMEM((2 * PK_HEADS, N_KEYS, tt), _F32)],
        compiler_params=_params("arbitrary"),
        name="peer_route",
    )(ht, wq_t_bf, keys_bf)


def _twice_gelu_tanh(x):
    return x * (1.0 + jnp.tanh(x * (0.7978845608028654 + (0.7978845608028654 * 0.044715) * (x * x))))


def _row_to_tile(row, rows):
    packed = jnp.broadcast_to(row, (16, LANES)).astype(_BF)
    return jnp.tile(packed, (rows // 16, 1))


def _peer_kernel(ht_ref, u_ref, vt_ref, n1_ref, e1_ref, r2_ref, e2_ref, x_ref, gt_ref, gf_ref,
                 o_ref, at_scr, wt_scr, acc, r2_scr, e2_scr, *, final_norm):
    e = pl.program_id(1)
    tt = ht_ref.shape[1]
    groups = u_ref.shape[0] // N_KEYS
    span = at_scr.shape[2]
    n_chunks = tt // span

    @pl.when(e == 0)
    def _():
        acc[...] = jnp.zeros(acc.shape, _F32)
        r2_scr[...] = r2_ref[...]
        e2_scr[...] = e2_ref[...]

    for c in range(n_chunks):
        at_scr[c] = _dot(u_ref[...], ht_ref[:, c * span:(c + 1) * span])
    zero = jnp.zeros((N_KEYS, LANES), _BF)

    for c in range(n_chunks):
        for sub in range(span // LANES):
            cols = slice(c * span + sub * LANES, c * span + (sub + 1) * LANES)
            sub_cols = slice(sub * LANES, (sub + 1) * LANES)
            for il in range(groups):
                rows = slice(il * N_KEYS, (il + 1) * N_KEYS)
                g = None
                for h in range(PK_HEADS):
                    count = _row_to_tile(n1_ref[h, il:il + 1, cols], N_KEYS)
                    sel = jnp.where(r2_scr[h, :, cols] < count, e2_scr[h, :, cols], zero)
                    term = sel * _row_to_tile(e1_ref[h, il:il + 1, cols], N_KEYS)
                    g = term if g is None else g + term
                act = _twice_gelu_tanh(at_scr[c, rows, sub_cols]).astype(_BF)
                wt_scr[c, rows, sub_cols] = g * act
        acc[c] += _dot(vt_ref[...], wt_scr[c])

    @pl.when(e == pl.num_programs(1) - 1)
    def _():
        f = jnp.concatenate([acc[c].T for c in range(n_chunks)], axis=0)
        y = x_ref[...] + gt_ref[...] * f
        if final_norm:
            y = _rms(y) * gf_ref[...]
        o_ref[...] = y


def _peer_experts(ht, u_bf, vt_bf, route, x1, gate, g_final, *, layer, per_row, seq, final_norm):
    tiles, d, tt = ht.shape
    _, steps, _, eb = vt_bf.shape
    t_all = tiles * tt
    span = min(PEER_CHUNK, tt)
    groups = eb // N_KEYS
    assert groups == 8
    n1, e1, r2, e2 = route
    by_step = lambda a: a.reshape(PK_HEADS, N_KEYS // groups, groups, t_all)
    gate_spec = (pl.BlockSpec((tt, d), lambda t, e: (t, 0)) if per_row else
                 pl.BlockSpec((None, 1, d), lambda t, e: (t // max(seq // tt, 1), 0, 0)))
    step_rows = pl.BlockSpec((PK_HEADS, None, groups, tt), lambda t, e: (0, e, 0, t))
    tile = _resident((PK_HEADS, N_KEYS, tt), lambda t, e: (0, 0, t))
    return pl.pallas_call(
        functools.partial(_peer_kernel, final_norm=final_norm),
        grid=(tiles, steps),
        in_specs=[pl.BlockSpec((None, d, tt), lambda t, e: (t, 0, 0)),
                  pl.BlockSpec((None, eb, d), lambda t, e: (layer, e, 0)),
                  pl.BlockSpec((None, None, d, eb), lambda t, e: (layer, e, 0, 0)),
                  step_rows, step_rows, tile, tile,
                  _resident((tt, d), lambda t, e: (t, 0)),
                  gate_spec,
                  _resident((1, d), lambda t, e: (0, 0))],
        out_specs=pl.BlockSpec((tt, d), lambda t, e: (t, 0)),
        out_shape=jax.ShapeDtypeStruct((t_all, d), _F32),
        scratch_shapes=[pltpu.VMEM((tt // span, eb, span), _F32), pltpu.VMEM((tt // span, eb, span), _BF),
                        pltpu.VMEM((tt // span, d, span), _F32),
                        pltpu.VMEM((PK_HEADS, N_KEYS, tt), _BF), pltpu.VMEM((PK_HEADS, N_KEYS, tt), _BF)],
        compiler_params=_params("arbitrary", "arbitrary"),
        name="peer_experts",
    )(ht, u_bf, vt_bf, by_step(n1), by_step(e1), r2, e2, x1, gate, g_final)


def _rope_tables(pos):
    half = ROT_DIM // 2
    inv = ROPE_THETA ** (-jnp.arange(0, ROT_DIM, 2, dtype=_F32) / ROT_DIM)
    ang = pos.astype(_F32)[:, None] * inv[None, :]
    cos, sin = jnp.cos(ang), jnp.sin(ang)
    n = pos.shape[0]
    pad = jnp.zeros((n, A_DIM - ROT_DIM), _F32)
    zeros = jnp.zeros((n, half), _F32)
    c = jnp.concatenate([cos, cos, pad + 1.0], axis=1)
    sa = jnp.concatenate([-sin, zeros, pad], axis=1)
    sb = jnp.concatenate([zeros, sin, pad], axis=1)
    return tuple(jnp.tile(t, (1, LANES // A_DIM)) for t in (c, sa, sb))


def _stream_layer(x, mod, lw, tables, rope_tabs, mixer, *, per_row, seq, final_norm, g_final):
    shift1, scale1, gate1, shift2, scale2, gate2 = mod
    proj = _in_projection(x, shift1, scale1, lw["g_attn"], lw["w_in"], rope_tabs, per_row=per_row, seq=seq)
    qa, kaf, vaf, kab, vab, qb, kbf, vbf, kbb, vbb = proj
    oa, ob, extra = mixer(qa, kaf, vaf, kab, vab, qb, kbf, vbf, kbb, vbb)
    x1, ht = _out_projection(x, oa, ob, lw["w_out"], gate1, shift2, scale2, lw["g_ffn"],
                             per_row=per_row, seq=seq)
    route = _peer_route(ht, lw["pk_wq_t"], lw["pk_keys"])
    x2 = _peer_experts(ht, *tables, route, x1, gate2, g_final, layer=lw["layer"],
                       per_row=per_row, seq=seq, final_norm=final_norm)
    return x2, (kaf, vaf, kbf, vbf) + extra


def kernel(x_prompt, x_sample, c_prompt, c_sample, cache_a_k, cache_a_v, cache_b_k, cache_b_v, w_ada, b_ada, g_attn, g_ffn, w_in, lam_q1, lam_k1, lam_q2, lam_k2, a_gain, rel_bias, b_gain, w_out, pk_wq, pk_keys, pk_u, pk_v, g_final):
    batch, seq, d = x_prompt.shape
    nb, t_new, _ = x_sample.shape
    depth = w_ada.shape[0]
    past = cache_a_k.shape[2]
    wb = cache_b_k.shape[2]
    a_width = A_HEADS * 2 * A_DIM
    b_width = B_HEADS * B_DIM

    c_all = jnp.concatenate([c_prompt, c_sample], axis=0)
    pad_rows = (-c_all.shape[0]) % 16
    c_all = jnp.pad(c_all, ((0, pad_rows), (0, 0)))
    mod = _modulation(c_all, w_ada, b_ada)

    tabs_p = _rope_tables(jnp.arange(seq, dtype=jnp.int32))
    pos_s = past + jnp.arange(t_new, dtype=jnp.int32)
    tabs_s = tuple(jnp.tile(t, (nb, 1)) for t in _rope_tables(pos_s))

    cak = cache_a_k.reshape(depth, nb, past * A_HEADS, 2 * A_DIM)
    cav = cache_a_v.reshape(depth, nb, past * A_HEADS, 2 * A_DIM)
    cbk = cache_b_k.reshape(depth, nb, wb, b_width)
    cbv = cache_b_v.reshape(depth, nb, wb, b_width)

    u_bf = pk_u.astype(_BF)
    vt_bf = pk_v.astype(_BF).reshape(depth, -1, PEER_EXPERTS, d).transpose(0, 1, 3, 2)

    xp = x_prompt.reshape(batch * seq, d)
    xs = x_sample.reshape(nb * t_new, d)
    g_fin = g_final.reshape(1, d)
    st_p, st_s = [], []
    for l in range(depth):
        lam_init = 0.8 - 0.6 * math.exp(-0.3 * l)
        lamv = jnp.stack([lam_q1[l], lam_k1[l], lam_q2[l], lam_k2[l]])
        lw = {
            "g_attn": g_attn[l].reshape(1, d),
            "g_ffn": g_ffn[l].reshape(1, d),
            "w_in": w_in[l].astype(_BF),
            "w_out": w_out[l].astype(_BF),
            "pk_wq_t": pk_wq[l].T.astype(_BF),
            "pk_keys": pk_keys[l].reshape(2 * PK_HEADS, N_KEYS, PK_HALF).astype(_BF),
            "layer": l,
        }
        gain_a = a_gain[l].reshape(1, 2 * A_DIM)
        gain_b = b_gain[l].reshape(1, b_width)
        mod_p = tuple(m.reshape(batch, 1, d) for m in jnp.split(mod[l, :batch], 6, axis=-1))
        mod_s = tuple(jnp.repeat(m, t_new, axis=0) for m in jnp.split(mod[l, batch:batch + nb], 6, axis=-1))
        last = l == depth - 1

        def mix_p(qa, kaf, vaf, kab, vab, qb, kbf, vbf, kbb, vbb):
            oa = _attn_a_prompt(qa, kab, vab, lamv, gain_a, batch=batch, seq=seq, lam_init=lam_init)
            ob = _attn_b_prompt(qb, kbb, vbb, rel_bias[l], gain_b, batch=batch, seq=seq)
            return oa, ob, ()

        def mix_s(qa, kaf, vaf, kab, vab, qb, kbf, vbf, kbb, vbb):
            oa = _attn_a_sample(qa, cak, cav, kaf, vaf, lamv, gain_a, layer=l, lam_init=lam_init, t_new=t_new)
            ob, nbk, nbv = _attn_b_sample(qb, cbk, cbv, kbf, vbf, rel_bias[l], gain_b, layer=l, t_new=t_new)
            return oa, ob, (nbk, nbv)

        xp, sp = _stream_layer(xp, mod_p, lw, (u_bf, vt_bf), tabs_p, mix_p, per_row=False, seq=seq,
                               final_norm=last, g_final=g_fin)
        xs, ss = _stream_layer(xs, mod_s, lw, (u_bf, vt_bf), tabs_s, mix_s, per_row=True, seq=t_new,
                               final_norm=last, g_final=g_fin)
        st_p.append(sp)
        st_s.append(ss)

    keep = min(B_REACH, seq)
    a_shape_p = (batch, seq, A_HEADS, 2 * A_DIM)
    b_tail = lambda s: s.reshape(batch, seq, b_width)[:, seq - keep:].reshape(batch, keep, B_HEADS, B_DIM)
    a_shape_s = (nb, t_new, A_HEADS, 2 * A_DIM)
    b_shape_s = (nb, wb, B_HEADS, B_DIM)
    return (
        xp.reshape(batch, seq, d),
        xs.reshape(nb, t_new, d),
        jnp.stack([s[0].reshape(a_shape_p) for s in st_p]),
        jnp.stack([s[1].reshape(a_shape_p) for s in st_p]),
        jnp.stack([b_tail(s[2]) for s in st_p]),
        jnp.stack([b_tail(s[3]) for s in st_p]),
        jnp.stack([s[0].reshape(a_shape_s) for s in st_s]),
        jnp.stack([s[1].reshape(a_shape_s) for s in st_s]),
        jnp.stack([s[4].reshape(b_shape_s) for s in st_s]),
        jnp.stack([s[5].reshape(b_shape_s) for s in st_s]),
    )
```

```python
import functools
import math

import jax
import jax.numpy as jnp
import numpy as np
from jax import lax
from jax.experimental import pallas as pl
from jax.experimental.pallas import tpu as pltpu

_F32 = jnp.float32
_BF = jnp.bfloat16

CHUNK = 64
A_HEADS = 4
A_DIM = 64
ROT_DIM = A_DIM // 4
ROPE_THETA = 500000.0
B_HEADS = 8
B_DIM = 64
LEFT_CHUNKS = 8
B_REACH = LEFT_CHUNKS * CHUNK
REL_CLIP = 128
PK_HEADS = 8
N_KEYS = 128
PK_TOPK = 16
PK_HALF = 128
EPS = 1e-6

LANES = 128
SUBLANES = 8
MASKED = -1e30
VMEM_LIMIT = 56 * 1024 * 1024

A_TILE = 512
B_TILE = 256
PEER_TOKENS = 1024
ROUTE_TOKENS = 512
PEER_EXPERTS = 1024
PEER_CHUNK = 256
ROW_TILE = 512


def _params(*sem):
    return pltpu.CompilerParams(dimension_semantics=sem, vmem_limit_bytes=VMEM_LIMIT)


def _resident(shape, index_map):
    return pl.BlockSpec(shape, index_map, pipeline_mode=pl.Buffered(1))


def _rms(x):
    return x * lax.rsqrt(jnp.mean(x * x, axis=-1, keepdims=True) + EPS)


def _dot(a, b):
    return jnp.dot(a, b, preferred_element_type=_F32)


def _dot_nt(a, b):
    return lax.dot_general(a, b, (((1,), (1,)), ((), ())), preferred_element_type=_F32)


def _mod_kernel(c_ref, w_ref, b_ref, o_ref):
    c = c_ref[...]
    s = c * (1.0 / (1.0 + jnp.exp(-c)))
    o_ref[...] = _dot(s.astype(_BF), w_ref[...].astype(_BF)) + b_ref[...]


def _modulation(c_all, w_ada, b_ada):
    depth, d, n = w_ada.shape
    rows = c_all.shape[0]
    tn = n // 4
    return pl.pallas_call(
        _mod_kernel,
        grid=(depth, n // tn),
        in_specs=[
            pl.BlockSpec((rows, d), lambda l, j: (0, 0)),
            pl.BlockSpec((None, d, tn), lambda l, j: (l, 0, j)),
            pl.BlockSpec((None, 1, tn), lambda l, j: (l, 0, j)),
        ],
        out_specs=pl.BlockSpec((None, rows, tn), lambda l, j: (l, 0, j)),
        out_shape=jax.ShapeDtypeStruct((depth, rows, n), _F32),
        compiler_params=_params("arbitrary", "arbitrary"),
        name="adaln_mod",
    )(c_all, w_ada, b_ada.reshape(depth, 1, n))


def _inproj_kernel(x_ref, sh_ref, sc_ref, g_ref, w_ref, cos_ref, sa_ref, sb_ref,
                   qa_ref, kaf_ref, vaf_ref, kab_ref, vab_ref,
                   qb_ref, kbf_ref, vbf_ref, kbb_ref, vbb_ref, *, width):
    h = _rms(x_ref[...]) * g_ref[...] * (1.0 + sc_ref[...]) + sh_ref[...]
    hb = h.astype(_BF)
    reps = width // LANES
    cos = jnp.tile(cos_ref[...], (1, reps))
    sa = jnp.tile(sa_ref[...], (1, reps))
    sb = jnp.tile(sb_ref[...], (1, reps))

    def rope(z):
        return z * cos + pltpu.roll(z, width - ROT_DIM // 2, 1) * sa + pltpu.roll(z, ROT_DIM // 2, 1) * sb

    def proj(i):
        return _dot(hb, w_ref[:, i * width:(i + 1) * width])

    def store_state(ref, z):
        hd = 2 * A_DIM
        for head in range(A_HEADS):
            ref[pl.ds(head, z.shape[0], stride=A_HEADS), :] = z[:, head * hd:(head + 1) * hd]

    qa_ref[...] = (rope(proj(0)) * (A_DIM ** -0.5)).astype(_BF)
    ka = rope(proj(1))
    store_state(kaf_ref, ka)
    kab_ref[...] = ka.astype(_BF)
    va = proj(2)
    store_state(vaf_ref, va)
    vab_ref[...] = va.astype(_BF)
    qb_ref[...] = (proj(3) * (B_DIM ** -0.5)).astype(_BF)
    kb = proj(4)
    kbf_ref[...] = kb
    kbb_ref[...] = kb.astype(_BF)
    vb = proj(5)
    vbf_ref[...] = vb
    vbb_ref[...] = vb.astype(_BF)


def _row_mod_specs(per_row, tm, d, tiles_per_batch):
    if per_row:
        return pl.BlockSpec((tm, d), lambda t: (t, 0))
    return pl.BlockSpec((None, 1, d), lambda t: (t // tiles_per_batch, 0, 0))


def _in_projection(x, shift, scale, gain, w_in_bf, rope_tabs, *, per_row, seq):
    t_all, d = x.shape
    width = w_in_bf.shape[1] // 6
    tm = min(ROW_TILE, t_all)
    tiles_per_batch = max(seq // tm, 1)
    tab_tiles = rope_tabs[0].shape[0] // tm
    mod_spec = _row_mod_specs(per_row, tm, d, tiles_per_batch)
    tab_spec = pl.BlockSpec((tm, LANES), lambda t: (t % tab_tiles, 0))
    row = lambda dt: jax.ShapeDtypeStruct((t_all, width), dt)
    out_spec = pl.BlockSpec((tm, width), lambda t: (t, 0))
    dts = [_BF, _F32, _F32, _BF, _BF, _BF, _F32, _F32, _BF, _BF]
    out_shapes = [row(dt) for dt in dts]
    out_specs = [out_spec] * 10
    for i in (1, 2):
        out_shapes[i] = jax.ShapeDtypeStruct((t_all * A_HEADS, 2 * A_DIM), _F32)
        out_specs[i] = pl.BlockSpec((tm * A_HEADS, 2 * A_DIM), lambda t: (t, 0))
    return pl.pallas_call(
        functools.partial(_inproj_kernel, width=width),
        grid=(t_all // tm,),
        in_specs=[
            pl.BlockSpec((tm, d), lambda t: (t, 0)),
            mod_spec, mod_spec,
            _resident((1, d), lambda t: (0, 0)),
            _resident(w_in_bf.shape, lambda t: (0, 0)),
            tab_spec, tab_spec, tab_spec,
        ],
        out_specs=out_specs,
        out_shape=out_shapes,
        compiler_params=_params("arbitrary"),
        name="in_projection",
    )(x, shift, scale, gain, w_in_bf, *rope_tabs)


def _diff_lambda(lamv_ref, lam_init):
    v = lamv_ref[...]
    d1 = jnp.sum(v[0:1] * v[1:2], axis=1, keepdims=True)
    d2 = jnp.sum(v[2:3] * v[3:4], axis=1, keepdims=True)
    return jnp.exp(d1) - jnp.exp(d2) + lam_init


def _split_maps(q):
    lane = lax.broadcasted_iota(jnp.int32, q.shape, 1)
    zero = jnp.zeros_like(q)
    return jnp.where(lane < A_DIM, q, zero), jnp.where(lane >= A_DIM, q, zero)


def _attn_a_kernel(lamv_ref, q_ref, k_ref, v_ref, gain_ref, o_ref,
                   sa, sb, m1, l1, acc1, m2, l2, acc2, *, lam_init):
    n_full = pl.program_id(2)
    tq = q_ref.shape[0]
    q1, q2 = _split_maps(q_ref[...])
    for m_scr, l_scr, acc in ((m1, l1, acc1), (m2, l2, acc2)):
        m_scr[...] = jnp.full(m_scr.shape, MASKED, _F32)
        l_scr[...] = jnp.zeros(l_scr.shape, _F32)
        acc[...] = jnp.zeros(acc.shape, _F32)

    def rows(j):
        return pl.ds(pl.multiple_of(j * tq, tq), tq)

    def scores(j, s_scr):
        kb = k_ref[rows(j), :]
        s_scr[0] = _dot_nt(q1, kb)
        s_scr[1] = _dot_nt(q2, kb)

    def update(s, m_scr, l_scr, acc, vb):
        m_prev = m_scr[...]
        m_next = jnp.maximum(m_prev, jnp.max(s, axis=1, keepdims=True))
        p = jnp.exp(s - jnp.tile(m_next, (1, tq // LANES)))
        alpha = jnp.exp(m_prev - m_next)
        part = p[:, 0:LANES]
        for c in range(1, tq // LANES):
            part = part + p[:, c * LANES:(c + 1) * LANES]
        l_scr[...] = alpha * l_scr[...] + part
        acc[...] = acc[...] * alpha + _dot(p.astype(_BF), vb)
        m_scr[...] = m_next

    def process(j, s_scr, diagonal):
        vb = v_ref[rows(j), :]
        for mp, (m_scr, l_scr, acc) in enumerate(((m1, l1, acc1), (m2, l2, acc2))):
            s = s_scr[mp]
            if diagonal:
                qc = lax.broadcasted_iota(jnp.int32, s.shape, 0) // CHUNK
                kc = lax.broadcasted_iota(jnp.int32, s.shape, 1) // CHUNK
                s = jnp.where(kc <= qc, s, MASKED)
            update(s, m_scr, l_scr, acc, vb)

    scores(0, sa)

    def pair(i, carry):
        scores(2 * i + 1, sb)
        process(2 * i, sa, False)
        scores(2 * i + 2, sa)
        process(2 * i + 1, sb, False)
        return carry

    lax.fori_loop(0, n_full // 2, pair, 0)

    @pl.when(n_full % 2 == 1)
    def _():
        scores(n_full, sb)
        process(n_full - 1, sa, False)
        process(n_full, sb, True)

    @pl.when(n_full % 2 == 0)
    def _():
        process(n_full, sa, True)

    lam = _diff_lambda(lamv_ref, lam_init)
    d1 = jnp.sum(l1[...], axis=1, keepdims=True)
    d2 = jnp.sum(l2[...], axis=1, keepdims=True)
    o = acc1[...] / d1 - lam * (acc2[...] / d2)
    o_ref[...] = (_rms(o) * gain_ref[...] * (1.0 - lam_init)).astype(_BF)


def _attn_a_prompt(qa, ka, va, lamv, a_gain, *, batch, seq, lam_init):
    t_all, width = qa.shape
    hd = 2 * A_DIM
    tq = A_TILE
    nq = seq // tq
    return pl.pallas_call(
        functools.partial(_attn_a_kernel, lam_init=lam_init),
        grid=(batch, A_HEADS, nq),
        in_specs=[
            _resident(lamv.shape, lambda b, h, i: (0, 0)),
            pl.BlockSpec((tq, hd), lambda b, h, i: (b * nq + i, h)),
            pl.BlockSpec((seq, hd), lambda b, h, i: (b, h)),
            pl.BlockSpec((seq, hd), lambda b, h, i: (b, h)),
            _resident((1, hd), lambda b, h, i: (0, 0)),
        ],
        out_specs=pl.BlockSpec((tq, hd), lambda b, h, i: (b * nq + i, h)),
        out_shape=jax.ShapeDtypeStruct((t_all, width), _BF),
        scratch_shapes=[pltpu.VMEM((2, tq, tq), _F32)] * 2 + [pltpu.VMEM((tq, hd), _F32)] * 6,
        compiler_params=_params("arbitrary", "arbitrary", "arbitrary"),
        name="diff_attn_prompt",
    )(lamv, qa, ka, va, a_gain)


def _head_lanes(shape, head, dim):
    lane = lax.broadcasted_iota(jnp.int32, shape, 1)
    return (lane >= head * dim) & (lane < (head + 1) * dim)


def _attn_b_kernel(q_ref, k0_ref, k1_ref, k2_ref, v0_ref, v1_ref, v2_ref, bias_ref, gain_ref, o_ref):
    qi = pl.program_id(1)
    tq = q_ref.shape[0]
    k_refs = (k0_ref, k1_ref, k2_ref)
    v_refs = (v0_ref, v1_ref, v2_ref)
    pen = (jnp.where(qi >= 2, 0.0, MASKED), jnp.where(qi >= 1, 0.0, MASKED), 0.0)
    outs = []
    for pair in range(B_HEADS // 2):
        cols = slice(pair * LANES, (pair + 1) * LANES)
        qp = q_ref[:, cols]
        zero = jnp.zeros_like(qp)
        halves = []
        for sub in range(2):
            head = 2 * pair + sub
            qh = jnp.where(_head_lanes(qp.shape, sub, B_DIM), qp, zero)
            s = [_dot_nt(qh, k_refs[b][:, cols]) + bias_ref[head, :, b * tq:(b + 1) * tq] + pen[b]
                 for b in range(3)]
            m = jnp.maximum(jnp.maximum(jnp.max(s[0], axis=1, keepdims=True),
                                        jnp.max(s[1], axis=1, keepdims=True)),
                            jnp.max(s[2], axis=1, keepdims=True))
            p = [jnp.exp(sb - m) for sb in s]
            l = (jnp.sum(p[0], axis=1, keepdims=True) + jnp.sum(p[1], axis=1, keepdims=True)
                 + jnp.sum(p[2], axis=1, keepdims=True))
            o = (_dot(p[0].astype(_BF), v_refs[0][:, cols]) + _dot(p[1].astype(_BF), v_refs[1][:, cols])
                 + _dot(p[2].astype(_BF), v_refs[2][:, cols]))
            halves.append(o / l)
        outs.append(jnp.where(_head_lanes(halves[0].shape, 0, B_DIM), halves[0], halves[1]))
    o = jnp.concatenate(outs, axis=1)
    o_ref[...] = (_rms(o) * gain_ref[...]).astype(_BF)


def _band_bias(table, tq):
    nk = B_REACH + tq
    qpos = np.arange(tq)[:, None]
    kpos = np.arange(nk)[None, :] - B_REACH
    qc = qpos // CHUNK
    kc = np.floor_divide(kpos, CHUNK)
    valid = (kc <= qc) & (kc >= qc - LEFT_CHUNKS)
    span = nk + tq - 1
    dist = (span - 1 - np.arange(span)) - (tq - 1)
    line = table[:, np.clip(dist, -REL_CLIP, REL_CLIP) + REL_CLIP].astype(_F32)
    line = jnp.pad(line, ((0, 0), (0, 1)))
    skew = jnp.tile(line, (1, tq))[:, :tq * span].reshape(-1, tq, span)
    bias = skew[:, :, tq - 1:]
    return jnp.where(valid[None], bias, MASKED)


def _attn_b_prompt(qb, kb, vb, rel_table, b_gain, *, batch, seq):
    t_all, width = qb.shape
    tq = B_TILE
    assert B_REACH == 2 * tq
    nq = seq // tq
    bias = _band_bias(rel_table, tq)
    qmap = lambda b, i: (b * nq + i, 0)
    kmap = lambda back: (lambda b, i: (b * nq + jnp.maximum(i - back, 0), 0))
    blk = lambda f: pl.BlockSpec((tq, width), f)
    return pl.pallas_call(
        _attn_b_kernel,
        grid=(batch, nq),
        in_specs=[blk(qmap), blk(kmap(2)), blk(kmap(1)), blk(kmap(0)),
                  blk(kmap(2)), blk(kmap(1)), blk(kmap(0)),
                  _resident(bias.shape, lambda b, i: (0, 0, 0)),
                  _resident((1, width), lambda b, i: (0, 0))],
        out_specs=blk(qmap),
        out_shape=jax.ShapeDtypeStruct((t_all, width), _BF),
        compiler_params=_params("arbitrary", "arbitrary"),
        name="band_attn_prompt",
    )(qb, kb, kb, kb, vb, vb, vb, bias, b_gain)


def _attn_a_sample_kernel(lamv_ref, q_ref, kc_ref, vc_ref, kn_ref, vn_ref, gain_ref, o_ref, *, lam_init):
    q = q_ref[...]
    t = q.shape[0]
    hd = 2 * A_DIM
    past = kc_ref.shape[0] // A_HEADS
    lam = _diff_lambda(lamv_ref, lam_init)
    outs = []
    for h in range(A_HEADS):
        cols = slice(h * hd, (h + 1) * hd)
        head_rows = pl.ds(h, past, stride=A_HEADS)
        new_rows = pl.ds(h, t, stride=A_HEADS)
        q12 = jnp.concatenate(_split_maps(q[:, cols]), axis=0)
        kn = kn_ref[new_rows, :].astype(_BF)
        vn = vn_ref[new_rows, :].astype(_BF)
        s_c = _dot_nt(q12, kc_ref[head_rows, :].astype(_BF))
        s_n = _dot_nt(q12, kn)
        m = jnp.maximum(jnp.max(s_c, axis=1, keepdims=True), jnp.max(s_n, axis=1, keepdims=True))
        p_c = jnp.exp(s_c - m)
        p_n = jnp.exp(s_n - m)
        l = jnp.sum(p_c, axis=1, keepdims=True) + jnp.sum(p_n, axis=1, keepdims=True)
        p_c = p_c / l
        p_n = p_n / l
        a_c = p_c[0:t] - lam * p_c[t:2 * t]
        a_n = p_n[0:t] - lam * p_n[t:2 * t]
        o = _dot(a_c.astype(_BF), vc_ref[head_rows, :].astype(_BF)) + _dot(a_n.astype(_BF), vn)
        outs.append(_rms(o) * gain_ref[...] * (1.0 - lam_init))
    o_ref[...] = jnp.concatenate(outs, axis=1).astype(_BF)


def _attn_a_sample(qa, cache_k, cache_v, kn, vn, lamv, a_gain, *, layer, lam_init, t_new):
    t_all, width = qa.shape
    _, nb, cache_rows, hd = cache_k.shape
    tok = pl.BlockSpec((t_new, width), lambda b: (b, 0))
    new = pl.BlockSpec((t_new * A_HEADS, hd), lambda b: (b, 0))
    cache = pl.BlockSpec((None, None, cache_rows, hd), lambda b: (layer, b, 0, 0))
    return pl.pallas_call(
        functools.partial(_attn_a_sample_kernel, lam_init=lam_init),
        grid=(nb,),
        in_specs=[_resident(lamv.shape, lambda b: (0, 0)), tok, cache, cache, new, new,
                  _resident((1, 2 * A_DIM), lambda b: (0, 0))],
        out_specs=tok,
        out_shape=jax.ShapeDtypeStruct((t_all, width), _BF),
        compiler_params=_params("arbitrary"),
        name="diff_attn_sample",
    )(lamv, qa, cache_k, cache_v, kn, vn, a_gain)


def _attn_b_sample_kernel(q_ref, kc_ref, vc_ref, kn_ref, vn_ref, bc_ref, bn_ref, gain_ref,
                          o_ref, ko_ref, vo_ref):
    q = q_ref[...]
    t = q.shape[0]
    keep = kc_ref.shape[0] - t
    zero = jnp.zeros_like(q)
    qall = jnp.concatenate(
        [jnp.where(_head_lanes(q.shape, h, B_DIM), q, zero) for h in range(B_HEADS)], axis=0)
    kc = kc_ref[...]
    vc = vc_ref[...]
    kn = kn_ref[...]
    vn = vn_ref[...]
    s_c = _dot_nt(qall, kc.astype(_BF)) + bc_ref[...]
    s_n = _dot_nt(qall, kn.astype(_BF)) + bn_ref[...]
    m = jnp.maximum(jnp.max(s_c, axis=1, keepdims=True), jnp.max(s_n, axis=1, keepdims=True))
    p_c = jnp.exp(s_c - m)
    p_n = jnp.exp(s_n - m)
    l = jnp.sum(p_c, axis=1, keepdims=True) + jnp.sum(p_n, axis=1, keepdims=True)
    o_all = (_dot(p_c.astype(_BF), vc.astype(_BF)) + _dot(p_n.astype(_BF), vn.astype(_BF))) / l
    o = jnp.zeros((t, q.shape[1]), _F32)
    for h in range(B_HEADS):
        o = jnp.where(_head_lanes(o.shape, h, B_DIM), o_all[h * t:(h + 1) * t], o)
    o_ref[...] = (_rms(o) * gain_ref[...]).astype(_BF)
    ko_ref[0:keep, :] = kc[t:, :]
    ko_ref[keep:, :] = kn
    vo_ref[0:keep, :] = vc[t:, :]
    vo_ref[keep:, :] = vn


def _sample_bias(table, t_new, wb):
    qpos = jnp.arange(t_new)[:, None] + wb
    kpos = jnp.arange(wb + t_new)[None, :]
    d = jnp.clip(qpos - kpos, -REL_CLIP, REL_CLIP) + REL_CLIP
    bias = table[:, d].astype(_F32).reshape(B_HEADS * t_new, wb + t_new)
    return bias[:, :wb], bias[:, wb:]


def _attn_b_sample(qb, cache_k, cache_v, kn, vn, rel_table, b_gain, *, layer, t_new):
    t_all, width = qb.shape
    _, nb, wb, _ = cache_k.shape
    bias_c, bias_n = _sample_bias(rel_table, t_new, wb)
    tok = pl.BlockSpec((t_new, width), lambda b: (b, 0))
    cache = pl.BlockSpec((None, None, wb, width), lambda b: (layer, b, 0, 0))
    state = pl.BlockSpec((None, wb, width), lambda b: (b, 0, 0))
    return pl.pallas_call(
        _attn_b_sample_kernel,
        grid=(nb,),
        in_specs=[tok, cache, cache, tok, tok,
                  _resident(bias_c.shape, lambda b: (0, 0)), _resident(bias_n.shape, lambda b: (0, 0)),
                  _resident((1, width), lambda b: (0, 0))],
        out_specs=[tok, state, state],
        out_shape=[jax.ShapeDtypeStruct((t_all, width), _BF),
                   jax.ShapeDtypeStruct((nb, wb, width), _F32),
                   jax.ShapeDtypeStruct((nb, wb, width), _F32)],
        compiler_params=_params("arbitrary"),
        name="band_attn_sample",
    )(qb, cache_k, cache_v, kn, vn, bias_c, bias_n, b_gain)


def _outproj_kernel(x_ref, oa_ref, ob_ref, w_ref, gt_ref, sh_ref, sc_ref, g_ref, x1_ref, ht_ref):
    half = oa_ref.shape[1]
    y = _dot(oa_ref[...], w_ref[0:half, :]) + _dot(ob_ref[...], w_ref[half:, :])
    x1 = x_ref[...] + gt_ref[...] * y
    x1_ref[...] = x1
    h = _rms(x1) * g_ref[...] * (1.0 + sc_ref[...]) + sh_ref[...]
    ht_ref[...] = h.T.astype(_BF)


def _out_projection(x, oa, ob, w_out_bf, gate, shift, scale, gain, *, per_row, seq):
    t_all, d = x.shape
    half = oa.shape[1]
    tm = min(PEER_TOKENS, t_all)
    mod_spec = _row_mod_specs(per_row, tm, d, max(seq // tm, 1))
    rows = lambda w: pl.BlockSpec((tm, w), lambda t: (t, 0))
    return pl.pallas_call(
        _outproj_kernel,
        grid=(t_all // tm,),
        in_specs=[rows(d), rows(half), rows(half), _resident(w_out_bf.shape, lambda t: (0, 0)),
                  mod_spec, mod_spec, mod_spec, _resident((1, d), lambda t: (0, 0))],
        out_specs=[rows(d), pl.BlockSpec((None, d, tm), lambda t: (t, 0, 0))],
        out_shape=[jax.ShapeDtypeStruct((t_all, d), _F32),
                   jax.ShapeDtypeStruct((t_all // tm, d, tm), _BF)],
        compiler_params=_params("arbitrary"),
        name="out_projection",
    )(x, oa, ob, w_out_bf, gate, shift, scale, gain)


def _ranked_top_values(s, k):
    vals = []
    rank = jnp.full(s.shape, float(k), _F32)
    for i in range(k):
        m = jnp.max(s, axis=0, keepdims=True)
        vals.append(m)
        hit = s == m
        rank = jnp.where(hit, float(i), rank)
        s = jnp.where(hit, -jnp.inf, s)
    return vals, rank


def _stack_rows(vals):
    rows = lax.broadcasted_iota(jnp.int32, (len(vals), LANES), 0)
    out = jnp.zeros((len(vals), LANES), _F32)
    for i, v in enumerate(vals):
        out = jnp.where(rows == i, v, out)
    return out


def _merge_sort_network(lo, hi):
    def merge(lo, hi, r):
        step = 2 * r
        if step < hi - lo:
            yield from merge(lo, hi, step)
            yield from merge(lo + r, hi, step)
            yield from ((i, i + r) for i in range(lo + r, hi - r, step))
        else:
            yield (lo, lo + r)

    if hi - lo >= 1:
        mid = lo + (hi - lo) // 2
        yield from _merge_sort_network(lo, mid)
        yield from _merge_sort_network(mid + 1, hi)
        yield from merge(lo, hi, 1)


def _compare_exchange(v, i, j):
    a, b = v[i], v[j]
    if b is None:
        return
    if a is None:
        v[i], v[j] = b, None
        return
    v[i], v[j] = jnp.maximum(a, b), jnp.minimum(a, b)


def _top16_replicated(tiles):
    n = PK_TOPK
    v = list(tiles) + [None] * (n - len(tiles))
    for i, j in _merge_sort_network(0, n - 1):
        _compare_exchange(v, i, j)
    shift = SUBLANES // 2
    while shift:
        w = [None if x is None else pltpu.roll(x, shift, 0) for x in v]
        c = []
        for k in range(n):
            a, b = v[k], w[n - 1 - k]
            c.append(b if a is None else a if b is None else jnp.maximum(a, b))
        stride = n // 2
        while stride:
            for i in range(n):
                if not i & stride:
                    _compare_exchange(c, i, i + stride)
            stride //= 2
        v = c
        shift //= 2
    return [jnp.full((SUBLANES, LANES), -jnp.inf, _F32) if x is None else x for x in v]


def _route_kernel(ht_ref, wq_ref, keys_ref, n1_ref, e1_ref, r2_ref, e2_ref, q_scr, s_scr):
    tt = ht_ref.shape[1]
    q_scr[...] = _dot(wq_ref[...], ht_ref[...]).astype(_BF)
    for hm in range(2 * PK_HEADS):
        s_scr[hm] = _dot(keys_ref[hm], q_scr[hm * PK_HALF:(hm + 1) * PK_HALF, :])

    def head(h, carry):
        for c in range(tt // LANES):
            cols = slice(c * LANES, (c + 1) * LANES)
            s1 = s_scr[2 * h, :, cols]
            s2 = s_scr[2 * h + 1, :, cols]
            a = _top16_replicated([s1[r:r + SUBLANES] for r in range(0, N_KEYS, SUBLANES)])
            b, rank2 = _ranked_top_values(s2, PK_TOPK)
            bmat = _stack_rows(b)
            b_lo, b_hi = bmat[0:SUBLANES], bmat[SUBLANES:]
            cand = [a[0] + b_lo, a[0] + b_hi]
            cand += [a[p] + b_lo for p in range(1, 8)]
            cand += [_stack_rows([a[p][0:1] for p in range(8, PK_TOPK)]) + b[0]]
            top = _top16_replicated(cand)
            tau = top[PK_TOPK - 1]
            z = jnp.ones_like(tau)
            for k in range(1, PK_TOPK):
                z = z + jnp.exp(top[k] - top[0])
            z_row = z[0:1]
            count1 = jnp.zeros(s1.shape, _F32)
            for p in range(PK_TOPK):
                hit = jnp.where(a[p] + b_lo >= tau, 1.0, 0.0) + jnp.where(a[p] + b_hi >= tau, 1.0, 0.0)
                reach = jnp.sum(hit, axis=0, keepdims=True)
                count1 = jnp.where(s1 == jnp.tile(a[p], (N_KEYS // SUBLANES, 1)), reach, count1)
            n1_ref[h, :, cols] = count1
            e1_ref[h, :, cols] = jnp.exp(s1 - a[0][0:1]) * (0.5 / z_row)
            r2_ref[h, :, cols] = rank2.astype(_BF)
            e2_ref[h, :, cols] = jnp.exp(s2 - b[0]).astype(_BF)
        return carry

    lax.fori_loop(0, PK_HEADS, head, 0)


def _peer_route(ht, wq_t_bf, keys_bf):
    slabs, d, slab = ht.shape
    t_all = slabs * slab
    tt = min(ROUTE_TOKENS, slab)
    per_slab = slab // tt
    nq = wq_t_bf.shape[0]
    out = lambda dt: jax.ShapeDtypeStruct((PK_HEADS, N_KEYS, t_all), dt)
    tile = pl.BlockSpec((PK_HEADS, N_KEYS, tt), lambda t: (0, 0, t))
    return pl.pallas_call(
        _route_kernel,
        grid=(t_all // tt,),
        in_specs=[pl.BlockSpec((None, d, tt), lambda t: (t // per_slab, 0, t % per_slab)),
                  _resident(wq_t_bf.shape, lambda t: (0, 0)),
                  _resident(keys_bf.shape, lambda t: (0, 0, 0))],
        out_specs=[tile] * 4,
        out_shape=[out(_F32), out(_F32), out(_BF), out(_BF)],
        scratch_shapes=[pltpu.VMEM((nq, tt), _BF), pltpu.VMEM((2 * PK_HEADS, N_KEYS, tt), _F32)],
        compiler_params=_params("arbitrary"),
        name="peer_route",
    )(ht, wq_t_bf, keys_bf)


def _twice_gelu_tanh(x):
    return x * (1.0 + jnp.tanh(x * (0.7978845608028654 + (0.7978845608028654 * 0.044715) * (x * x))))


def _row_to_tile(row, rows):
    packed = jnp.broadcast_to(row, (16, LANES)).astype(_BF)
    return jnp.tile(packed, (rows // 16, 1))


def _peer_kernel(ht_ref, u_ref, vt_ref, n1_ref, e1_ref, r2_ref, e2_ref, x_ref, gt_ref, gf_ref,
                 o_ref, at_scr, wt_scr, acc, r2_scr, e2_scr, *, final_norm):
    e = pl.program_id(1)
    tt = ht_ref.shape[1]
    groups = u_ref.shape[0] // N_KEYS
    span = at_scr.shape[2]
    n_chunks = tt // span

    @pl.when(e == 0)
    def _():
        acc[...] = jnp.zeros(acc.shape, _F32)
        r2_scr[...] = r2_ref[...]
        e2_scr[...] = e2_ref[...]

    for c in range(n_chunks):
        at_scr[c] = _dot(u_ref[...], ht_ref[:, c * span:(c + 1) * span])
    zero = jnp.zeros((N_KEYS, LANES), _BF)

    for c in range(n_chunks):
        for sub in range(span // LANES):
            cols = slice(c * span + sub * LANES, c * span + (sub + 1) * LANES)
            sub_cols = slice(sub * LANES, (sub + 1) * LANES)
            for il in range(groups):
                rows = slice(il * N_KEYS, (il + 1) * N_KEYS)
                g = None
                for h in range(PK_HEADS):
                    count = _row_to_tile(n1_ref[h, il:il + 1, cols], N_KEYS)
                    sel = jnp.where(r2_scr[h, :, cols] < count, e2_scr[h, :, cols], zero)
                    term = sel * _row_to_tile(e1_ref[h, il:il + 1, cols], N_KEYS)
                    g = term if g is None else g + term
                act = _twice_gelu_tanh(at_scr[c, rows, sub_cols]).astype(_BF)
                wt_scr[c, rows, sub_cols] = g * act
        acc[c] += _dot(vt_ref[...], wt_scr[c])

    @pl.when(e == pl.num_programs(1) - 1)
    def _():
        f = jnp.concatenate([acc[c].T for c in range(n_chunks)], axis=0)
        y = x_ref[...] + gt_ref[...] * f
        if final_norm:
            y = _rms(y) * gf_ref[...]
        o_ref[...] = y


def _peer_experts(ht, u_bf, vt_bf, route, x1, gate, g_final, *, layer, per_row, seq, final_norm):
    tiles, d, tt = ht.shape
    _, steps, _, eb = vt_bf.shape
    t_all = tiles * tt
    span = min(PEER_CHUNK, tt)
    groups = eb // N_KEYS
    assert groups == SUBLANES
    n1, e1, r2, e2 = route
    by_step = lambda a: a.reshape(PK_HEADS, N_KEYS // groups, groups, t_all)
    gate_spec = (pl.BlockSpec((tt, d), lambda t, e: (t, 0)) if per_row else
                 pl.BlockSpec((None, 1, d), lambda t, e: (t // max(seq // tt, 1), 0, 0)))
    step_rows = pl.BlockSpec((PK_HEADS, None, groups, tt), lambda t, e: (0, e, 0, t))
    tile = pl.BlockSpec((PK_HEADS, N_KEYS, tt), lambda t, e: (0, 0, t))
    return pl.pallas_call(
        functools.partial(_peer_kernel, final_norm=final_norm),
        grid=(tiles, steps),
        in_specs=[pl.BlockSpec((None, d, tt), lambda t, e: (t, 0, 0)),
                  pl.BlockSpec((None, eb, d), lambda t, e: (layer, e, 0)),
                  pl.BlockSpec((None, None, d, eb), lambda t, e: (layer, e, 0, 0)),
                  step_rows, step_rows, tile, tile,
                  pl.BlockSpec((tt, d), lambda t, e: (t, 0)),
                  gate_spec,
                  _resident((1, d), lambda t, e: (0, 0))],
        out_specs=pl.BlockSpec((tt, d), lambda t, e: (t, 0)),
        out_shape=jax.ShapeDtypeStruct((t_all, d), _F32),
        scratch_shapes=[pltpu.VMEM((tt // span, eb, span), _F32), pltpu.VMEM((tt // span, eb, span), _BF),
                        pltpu.VMEM((tt // span, d, span), _F32),
                        pltpu.VMEM((PK_HEADS, N_KEYS, tt), _BF), pltpu.VMEM((PK_HEADS, N_KEYS, tt), _BF)],
        compiler_params=_params("arbitrary", "arbitrary"),
        name="peer_experts",
    )(ht, u_bf, vt_bf, by_step(n1), by_step(e1), r2, e2, x1, gate, g_final)


def _rope_tables(pos):
    half = ROT_DIM // 2
    inv = ROPE_THETA ** (-jnp.arange(0, ROT_DIM, 2, dtype=_F32) / ROT_DIM)
    ang = pos.astype(_F32)[:, None] * inv[None, :]
    cos, sin = jnp.cos(ang), jnp.sin(ang)
    n = pos.shape[0]
    pad = jnp.zeros((n, A_DIM - ROT_DIM), _F32)
    zeros = jnp.zeros((n, half), _F32)
    c = jnp.concatenate([cos, cos, pad + 1.0], axis=1)
    sa = jnp.concatenate([-sin, zeros, pad], axis=1)
    sb = jnp.concatenate([zeros, sin, pad], axis=1)
    return tuple(jnp.tile(t, (1, LANES // A_DIM)) for t in (c, sa, sb))


def _stream_layer(x, mod, lw, tables, rope_tabs, mixer, *, per_row, seq, final_norm, g_final):
    shift1, scale1, gate1, shift2, scale2, gate2 = mod
    proj = _in_projection(x, shift1, scale1, lw["g_attn"], lw["w_in"], rope_tabs, per_row=per_row, seq=seq)
    qa, kaf, vaf, kab, vab, qb, kbf, vbf, kbb, vbb = proj
    oa, ob, extra = mixer(qa, kaf, vaf, kab, vab, qb, kbf, vbf, kbb, vbb)
    x1, ht = _out_projection(x, oa, ob, lw["w_out"], gate1, shift2, scale2, lw["g_ffn"],
                             per_row=per_row, seq=seq)
    route = _peer_route(ht, lw["pk_wq_t"], lw["pk_keys"])
    x2 = _peer_experts(ht, *tables, route, x1, gate2, g_final, layer=lw["layer"],
                       per_row=per_row, seq=seq, final_norm=final_norm)
    return x2, (kaf, vaf, kbf, vbf) + extra


def kernel(x_prompt, x_sample, c_prompt, c_sample, cache_a_k, cache_a_v, cache_b_k, cache_b_v, w_ada, b_ada, g_attn, g_ffn, w_in, lam_q1, lam_k1, lam_q2, lam_k2, a_gain, rel_bias, b_gain, w_out, pk_wq, pk_keys, pk_u, pk_v, g_final):
    batch, seq, d = x_prompt.shape
    nb, t_new, _ = x_sample.shape
    depth = w_ada.shape[0]
    past = cache_a_k.shape[2]
    wb = cache_b_k.shape[2]
    a_width = A_HEADS * 2 * A_DIM
    b_width = B_HEADS * B_DIM

    c_all = jnp.concatenate([c_prompt, c_sample], axis=0)
    pad_rows = (-c_all.shape[0]) % 16
    c_all = jnp.pad(c_all, ((0, pad_rows), (0, 0)))
    mod = _modulation(c_all, w_ada, b_ada)

    tabs_p = _rope_tables(jnp.arange(seq, dtype=jnp.int32))
    pos_s = past + jnp.arange(t_new, dtype=jnp.int32)
    tabs_s = tuple(jnp.tile(t, (nb, 1)) for t in _rope_tables(pos_s))

    cak = cache_a_k.reshape(depth, nb, past * A_HEADS, 2 * A_DIM)
    cav = cache_a_v.reshape(depth, nb, past * A_HEADS, 2 * A_DIM)
    cbk = cache_b_k.reshape(depth, nb, wb, b_width)
    cbv = cache_b_v.reshape(depth, nb, wb, b_width)

    u_bf = pk_u.astype(_BF)
    vt_bf = pk_v.astype(_BF).reshape(depth, -1, PEER_EXPERTS, d).transpose(0, 1, 3, 2)

    xp = x_prompt.reshape(batch * seq, d)
    xs = x_sample.reshape(nb * t_new, d)
    g_fin = g_final.reshape(1, d)
    st_p, st_s = [], []
    for l in range(depth):
        lam_init = 0.8 - 0.6 * math.exp(-0.3 * l)
        lamv = jnp.stack([lam_q1[l], lam_k1[l], lam_q2[l], lam_k2[l]])
        lw = {
            "g_attn": g_attn[l].reshape(1, d),
            "g_ffn": g_ffn[l].reshape(1, d),
            "w_in": w_in[l].astype(_BF),
            "w_out": w_out[l].astype(_BF),
            "pk_wq_t": pk_wq[l].T.astype(_BF),
            "pk_keys": pk_keys[l].reshape(2 * PK_HEADS, N_KEYS, PK_HALF).astype(_BF),
            "layer": l,
        }
        gain_a = a_gain[l].reshape(1, 2 * A_DIM)
        gain_b = b_gain[l].reshape(1, b_width)
        mod_p = tuple(m.reshape(batch, 1, d) for m in jnp.split(mod[l, :batch], 6, axis=-1))
        mod_s = tuple(jnp.repeat(m, t_new, axis=0) for m in jnp.split(mod[l, batch:batch + nb], 6, axis=-1))
        last = l == depth - 1

        def mix_p(qa, kaf, vaf, kab, vab, qb, kbf, vbf, kbb, vbb):
            oa = _attn_a_prompt(qa, kab, vab, lamv, gain_a, batch=batch, seq=seq, lam_init=lam_init)
            ob = _attn_b_prompt(qb, kbb, vbb, rel_bias[l], gain_b, batch=batch, seq=seq)
            return oa, ob, ()

        def mix_s(qa, kaf, vaf, kab, vab, qb, kbf, vbf, kbb, vbb):
            oa = _attn_a_sample(qa, cak, cav, kaf, vaf, lamv, gain_a, layer=l, lam_init=lam_init, t_new=t_new)
            ob, nbk, nbv = _attn_b_sample(qb, cbk, cbv, kbf, vbf, rel_bias[l], gain_b, layer=l, t_new=t_new)
            return oa, ob, (nbk, nbv)

        xp, sp = _stream_layer(xp, mod_p, lw, (u_bf, vt_bf), tabs_p, mix_p, per_row=False, seq=seq,
                               final_norm=last, g_final=g_fin)
        xs, ss = _stream_layer(xs, mod_s, lw, (u_bf, vt_bf), tabs_s, mix_s, per_row=True, seq=t_new,
                               final_norm=last, g_final=g_fin)
        st_p.append(sp)
        st_s.append(ss)

    keep = min(B_REACH, seq)
    a_shape_p = (batch, seq, A_HEADS, 2 * A_DIM)
    b_tail = lambda s: s.reshape(batch, seq, b_width)[:, seq - keep:].reshape(batch, keep, B_HEADS, B_DIM)
    a_shape_s = (nb, t_new, A_HEADS, 2 * A_DIM)
    b_shape_s = (nb, wb, B_HEADS, B_DIM)
    return (
        xp.reshape(batch, seq, d),
        xs.reshape(nb, t_new, d),
        jnp.stack([s[0].reshape(a_shape_p) for s in st_p]),
        jnp.stack([s[1].reshape(a_shape_p) for s in st_p]),
        jnp.stack([b_tail(s[2]) for s in st_p]),
        jnp.stack([b_tail(s[3]) for s in st_p]),
        jnp.stack([s[0].reshape(a_shape_s) for s in st_s]),
        jnp.stack([s[1].reshape(a_shape_s) for s in st_s]),
        jnp.stack([s[4].reshape(b_shape_s) for s in st_s]),
        jnp.stack([s[5].reshape(b_shape_s) for s in st_s]),
    )
```

```python
import functools
import math

import jax
import jax.numpy as jnp
import numpy as np
from jax import lax
from jax.experimental import pallas as pl
from jax.experimental.pallas import tpu as pltpu

_F32 = jnp.float32
_BF = jnp.bfloat16

CHUNK = 64
A_HEADS = 4
A_DIM = 64
ROT_DIM = A_DIM // 4
ROPE_THETA = 500000.0
B_HEADS = 8
B_DIM = 64
LEFT_CHUNKS = 8
B_REACH = LEFT_CHUNKS * CHUNK
REL_CLIP = 128
PK_HEADS = 8
N_KEYS = 128
PK_TOPK = 16
PK_HALF = 128
EPS = 1e-6

LANES = 128
SUBLANES = 8
MASKED = -1e30
VMEM_LIMIT = 56 * 1024 * 1024

A_TILE = 512
B_TILE = 256
PEER_TOKENS = 1024
ROUTE_TOKENS = 512
PEER_EXPERTS = 1024
PEER_CHUNK = 256
ROW_TILE = 512


def _params(*sem):
    return pltpu.CompilerParams(dimension_semantics=sem, vmem_limit_bytes=VMEM_LIMIT)


def _resident(shape, index_map):
    return pl.BlockSpec(shape, index_map, pipeline_mode=pl.Buffered(1))


def _rms(x):
    return x * lax.rsqrt(jnp.mean(x * x, axis=-1, keepdims=True) + EPS)


def _dot(a, b):
    return jnp.dot(a, b, preferred_element_type=_F32)


def _dot_nt(a, b):
    return lax.dot_general(a, b, (((1,), (1,)), ((), ())), preferred_element_type=_F32)


def _mod_kernel(c_ref, w_ref, b_ref, o_ref):
    c = c_ref[...]
    s = c * (1.0 / (1.0 + jnp.exp(-c)))
    o_ref[...] = _dot(s.astype(_BF), w_ref[...].astype(_BF)) + b_ref[...]


def _modulation(c_all, w_ada, b_ada):
    depth, d, n = w_ada.shape
    rows = c_all.shape[0]
    tn = n // 4
    return pl.pallas_call(
        _mod_kernel,
        grid=(depth, n // tn),
        in_specs=[
            pl.BlockSpec((rows, d), lambda l, j: (0, 0)),
            pl.BlockSpec((None, d, tn), lambda l, j: (l, 0, j)),
            pl.BlockSpec((None, 1, tn), lambda l, j: (l, 0, j)),
        ],
        out_specs=pl.BlockSpec((None, rows, tn), lambda l, j: (l, 0, j)),
        out_shape=jax.ShapeDtypeStruct((depth, rows, n), _F32),
        compiler_params=_params("arbitrary", "arbitrary"),
        name="adaln_mod",
    )(c_all, w_ada, b_ada.reshape(depth, 1, n))


def _inproj_kernel(x_ref, sh_ref, sc_ref, g_ref, w_ref, cos_ref, sa_ref, sb_ref, *rest, width, carried):
    (qa_ref, kaf_ref, vaf_ref, kab_ref, vab_ref,
     qb_ref, kbf_ref, vbf_ref, kbb_ref, vbb_ref) = rest[2 * carried:]
    h = _rms(x_ref[...]) * g_ref[...] * (1.0 + sc_ref[...]) + sh_ref[...]
    hb = h.astype(_BF)
    reps = width // LANES
    cos = jnp.tile(cos_ref[...], (1, reps))
    sa = jnp.tile(sa_ref[...], (1, reps))
    sb = jnp.tile(sb_ref[...], (1, reps))

    def rope(z):
        return z * cos + pltpu.roll(z, width - ROT_DIM // 2, 1) * sa + pltpu.roll(z, ROT_DIM // 2, 1) * sb

    def proj(i):
        return _dot(hb, w_ref[:, i * width:(i + 1) * width])

    def store_state(ref, z):
        hd = 2 * A_DIM
        for head in range(A_HEADS):
            ref[pl.ds(head, z.shape[0], stride=A_HEADS), :] = z[:, head * hd:(head + 1) * hd]

    qa_ref[...] = (rope(proj(0)) * (A_DIM ** -0.5)).astype(_BF)
    ka = rope(proj(1))
    store_state(kaf_ref, ka)
    kab_ref[...] = ka.astype(_BF)
    va = proj(2)
    store_state(vaf_ref, va)
    vab_ref[...] = va.astype(_BF)
    qb_ref[...] = (proj(3) * (B_DIM ** -0.5)).astype(_BF)
    kb = proj(4)
    kbf_ref[...] = kb
    kbb_ref[...] = kb.astype(_BF)
    vb = proj(5)
    vbf_ref[...] = vb
    vbb_ref[...] = vb.astype(_BF)


def _row_mod_specs(per_row, tm, d, tiles_per_batch):
    if per_row:
        return pl.BlockSpec((tm, d), lambda t: (t, 0))
    return pl.BlockSpec((None, 1, d), lambda t: (t // tiles_per_batch, 0, 0))


def _in_projection(x, shift, scale, gain, w_in_bf, rope_tabs, *, per_row, seq, stack=None):
    t_all, d = x.shape
    width = w_in_bf.shape[1] // 6
    tm = min(ROW_TILE, t_all)
    tiles_per_batch = max(seq // tm, 1)
    tab_tiles = rope_tabs[0].shape[0] // tm
    mod_spec = _row_mod_specs(per_row, tm, d, tiles_per_batch)
    tab_spec = pl.BlockSpec((tm, LANES), lambda t: (t % tab_tiles, 0))
    row = lambda dt: jax.ShapeDtypeStruct((t_all, width), dt)
    out_spec = pl.BlockSpec((tm, width), lambda t: (t, 0))
    dts = [_BF, _F32, _F32, _BF, _BF, _BF, _F32, _F32, _BF, _BF]
    out_shapes = [row(dt) for dt in dts]
    out_specs = [out_spec] * 10
    state_rows, hd = t_all * A_HEADS, 2 * A_DIM
    in_specs = [
        pl.BlockSpec((tm, d), lambda t: (t, 0)),
        mod_spec, mod_spec,
        _resident((1, d), lambda t: (0, 0)),
        _resident(w_in_bf.shape, lambda t: (0, 0)),
        tab_spec, tab_spec, tab_spec,
    ]
    args = [x, shift, scale, gain, w_in_bf, *rope_tabs]
    aliases = {}
    carried = False
    for i in (1, 2):
        if stack is None:
            out_shapes[i] = jax.ShapeDtypeStruct((state_rows, hd), _F32)
            out_specs[i] = pl.BlockSpec((tm * A_HEADS, hd), lambda t: (t, 0))
        else:
            layer, depth, previous = stack
            out_shapes[i] = jax.ShapeDtypeStruct((depth, state_rows, hd), _F32)
            out_specs[i] = pl.BlockSpec((None, tm * A_HEADS, hd), lambda t: (layer, t, 0))
            if previous is not None:
                carried = True
                aliases[len(args)] = i
                in_specs.append(pl.BlockSpec(memory_space=pl.ANY))
                args.append(previous[i - 1])
    return pl.pallas_call(
        functools.partial(_inproj_kernel, width=width, carried=carried),
        grid=(t_all // tm,),
        in_specs=in_specs,
        out_specs=out_specs,
        out_shape=out_shapes,
        input_output_aliases=aliases,
        compiler_params=_params("arbitrary"),
        name="in_projection",
    )(*args)


def _diff_lambda(lamv_ref, lam_init):
    v = lamv_ref[...]
    d1 = jnp.sum(v[0:1] * v[1:2], axis=1, keepdims=True)
    d2 = jnp.sum(v[2:3] * v[3:4], axis=1, keepdims=True)
    return jnp.exp(d1) - jnp.exp(d2) + lam_init


def _split_maps(q):
    lane = lax.broadcasted_iota(jnp.int32, q.shape, 1)
    zero = jnp.zeros_like(q)
    return jnp.where(lane < A_DIM, q, zero), jnp.where(lane >= A_DIM, q, zero)


def _attn_a_kernel(lamv_ref, q_ref, k_ref, v_ref, gain_ref, o_ref,
                   sa, sb, m1, l1, acc1, m2, l2, acc2, *, lam_init):
    n_full = pl.program_id(2)
    tq = q_ref.shape[0]
    q1, q2 = _split_maps(q_ref[...])
    for m_scr, l_scr, acc in ((m1, l1, acc1), (m2, l2, acc2)):
        m_scr[...] = jnp.full(m_scr.shape, MASKED, _F32)
        l_scr[...] = jnp.zeros(l_scr.shape, _F32)
        acc[...] = jnp.zeros(acc.shape, _F32)

    def rows(j):
        return pl.ds(pl.multiple_of(j * tq, tq), tq)

    def scores(j, s_scr):
        kb = k_ref[rows(j), :]
        s_scr[0] = _dot_nt(q1, kb)
        s_scr[1] = _dot_nt(q2, kb)

    def update(s, m_scr, l_scr, acc, vb):
        m_prev = m_scr[...]
        m_next = jnp.maximum(m_prev, jnp.max(s, axis=1, keepdims=True))
        p = jnp.exp(s - jnp.tile(m_next, (1, tq // LANES)))
        alpha = jnp.exp(m_prev - m_next)
        part = p[:, 0:LANES]
        for c in range(1, tq // LANES):
            part = part + p[:, c * LANES:(c + 1) * LANES]
        l_scr[...] = alpha * l_scr[...] + part
        acc[...] = acc[...] * alpha + _dot(p.astype(_BF), vb)
        m_scr[...] = m_next

    def process(j, s_scr, diagonal):
        vb = v_ref[rows(j), :]
        for mp, (m_scr, l_scr, acc) in enumerate(((m1, l1, acc1), (m2, l2, acc2))):
            s = s_scr[mp]
            if diagonal:
                qc = lax.broadcasted_iota(jnp.int32, s.shape, 0) // CHUNK
                kc = lax.broadcasted_iota(jnp.int32, s.shape, 1) // CHUNK
                s = jnp.where(kc <= qc, s, MASKED)
            update(s, m_scr, l_scr, acc, vb)

    scores(0, sa)

    def pair(i, carry):
        scores(2 * i + 1, sb)
        process(2 * i, sa, False)
        scores(2 * i + 2, sa)
        process(2 * i + 1, sb, False)
        return carry

    lax.fori_loop(0, n_full // 2, pair, 0)

    @pl.when(n_full % 2 == 1)
    def _():
        scores(n_full, sb)
        process(n_full - 1, sa, False)
        process(n_full, sb, True)

    @pl.when(n_full % 2 == 0)
    def _():
        process(n_full, sa, True)

    lam = _diff_lambda(lamv_ref, lam_init)
    d1 = jnp.sum(l1[...], axis=1, keepdims=True)
    d2 = jnp.sum(l2[...], axis=1, keepdims=True)
    o = acc1[...] / d1 - lam * (acc2[...] / d2)
    o_ref[...] = (_rms(o) * gain_ref[...] * (1.0 - lam_init)).astype(_BF)


def _attn_a_prompt(qa, ka, va, lamv, a_gain, *, batch, seq, lam_init):
    t_all, width = qa.shape
    hd = 2 * A_DIM
    tq = A_TILE
    nq = seq // tq
    return pl.pallas_call(
        functools.partial(_attn_a_kernel, lam_init=lam_init),
        grid=(batch, A_HEADS, nq),
        in_specs=[
            _resident(lamv.shape, lambda b, h, i: (0, 0)),
            pl.BlockSpec((tq, hd), lambda b, h, i: (b * nq + i, h)),
            pl.BlockSpec((seq, hd), lambda b, h, i: (b, h)),
            pl.BlockSpec((seq, hd), lambda b, h, i: (b, h)),
            _resident((1, hd), lambda b, h, i: (0, 0)),
        ],
        out_specs=pl.BlockSpec((tq, hd), lambda b, h, i: (b * nq + i, h)),
        out_shape=jax.ShapeDtypeStruct((t_all, width), _BF),
        scratch_shapes=[pltpu.VMEM((2, tq, tq), _F32)] * 2 + [pltpu.VMEM((tq, hd), _F32)] * 6,
        compiler_params=_params("arbitrary", "arbitrary", "arbitrary"),
        name="diff_attn_prompt",
    )(lamv, qa, ka, va, a_gain)


def _head_lanes(shape, head, dim):
    lane = lax.broadcasted_iota(jnp.int32, shape, 1)
    return (lane >= head * dim) & (lane < (head + 1) * dim)


def _attn_b_kernel(q_ref, k0_ref, k1_ref, k2_ref, v0_ref, v1_ref, v2_ref, bias_ref, gain_ref, o_ref):
    qi = pl.program_id(1)
    tq = q_ref.shape[0]
    k_refs = (k0_ref, k1_ref, k2_ref)
    v_refs = (v0_ref, v1_ref, v2_ref)
    pen = (jnp.where(qi >= 2, 0.0, MASKED), jnp.where(qi >= 1, 0.0, MASKED), 0.0)
    outs = []
    for pair in range(B_HEADS // 2):
        cols = slice(pair * LANES, (pair + 1) * LANES)
        qp = q_ref[:, cols]
        zero = jnp.zeros_like(qp)
        halves = []
        for sub in range(2):
            head = 2 * pair + sub
            qh = jnp.where(_head_lanes(qp.shape, sub, B_DIM), qp, zero)
            s = [_dot_nt(qh, k_refs[b][:, cols]) + bias_ref[head, :, b * tq:(b + 1) * tq] + pen[b]
                 for b in range(3)]
            m = jnp.maximum(jnp.maximum(jnp.max(s[0], axis=1, keepdims=True),
                                        jnp.max(s[1], axis=1, keepdims=True)),
                            jnp.max(s[2], axis=1, keepdims=True))
            p = [jnp.exp(sb - m) for sb in s]
            l = (jnp.sum(p[0], axis=1, keepdims=True) + jnp.sum(p[1], axis=1, keepdims=True)
                 + jnp.sum(p[2], axis=1, keepdims=True))
            o = (_dot(p[0].astype(_BF), v_refs[0][:, cols]) + _dot(p[1].astype(_BF), v_refs[1][:, cols])
                 + _dot(p[2].astype(_BF), v_refs[2][:, cols]))
            halves.append(o / l)
        outs.append(jnp.where(_head_lanes(halves[0].shape, 0, B_DIM), halves[0], halves[1]))
    o = jnp.concatenate(outs, axis=1)
    o_ref[...] = (_rms(o) * gain_ref[...]).astype(_BF)


def _band_bias(table, tq):
    nk = B_REACH + tq
    qpos = np.arange(tq)[:, None]
    kpos = np.arange(nk)[None, :] - B_REACH
    qc = qpos // CHUNK
    kc = np.floor_divide(kpos, CHUNK)
    valid = (kc <= qc) & (kc >= qc - LEFT_CHUNKS)
    span = nk + tq - 1
    dist = (span - 1 - np.arange(span)) - (tq - 1)
    line = table[:, np.clip(dist, -REL_CLIP, REL_CLIP) + REL_CLIP].astype(_F32)
    line = jnp.pad(line, ((0, 0), (0, 1)))
    skew = jnp.tile(line, (1, tq))[:, :tq * span].reshape(-1, tq, span)
    bias = skew[:, :, tq - 1:]
    return jnp.where(valid[None], bias, MASKED)


def _attn_b_prompt(qb, kb, vb, rel_table, b_gain, *, batch, seq):
    t_all, width = qb.shape
    tq = B_TILE
    assert B_REACH == 2 * tq
    nq = seq // tq
    bias = _band_bias(rel_table, tq)
    qmap = lambda b, i: (b * nq + i, 0)
    kmap = lambda back: (lambda b, i: (b * nq + jnp.maximum(i - back, 0), 0))
    blk = lambda f: pl.BlockSpec((tq, width), f)
    return pl.pallas_call(
        _attn_b_kernel,
        grid=(batch, nq),
        in_specs=[blk(qmap), blk(kmap(2)), blk(kmap(1)), blk(kmap(0)),
                  blk(kmap(2)), blk(kmap(1)), blk(kmap(0)),
                  _resident(bias.shape, lambda b, i: (0, 0, 0)),
                  _resident((1, width), lambda b, i: (0, 0))],
        out_specs=blk(qmap),
        out_shape=jax.ShapeDtypeStruct((t_all, width), _BF),
        compiler_params=_params("arbitrary", "arbitrary"),
        name="band_attn_prompt",
    )(qb, kb, kb, kb, vb, vb, vb, bias, b_gain)


def _attn_a_sample_kernel(lamv_ref, q_ref, kc_ref, vc_ref, kn_ref, vn_ref, gain_ref, o_ref, *, lam_init):
    q = q_ref[...]
    t = q.shape[0]
    hd = 2 * A_DIM
    past = kc_ref.shape[0] // A_HEADS
    lam = _diff_lambda(lamv_ref, lam_init)
    outs = []
    for h in range(A_HEADS):
        cols = slice(h * hd, (h + 1) * hd)
        head_rows = pl.ds(h, past, stride=A_HEADS)
        new_rows = pl.ds(h, t, stride=A_HEADS)
        q12 = jnp.concatenate(_split_maps(q[:, cols]), axis=0)
        kn = kn_ref[new_rows, :].astype(_BF)
        vn = vn_ref[new_rows, :].astype(_BF)
        s_c = _dot_nt(q12, kc_ref[head_rows, :].astype(_BF))
        s_n = _dot_nt(q12, kn)
        m = jnp.maximum(jnp.max(s_c, axis=1, keepdims=True), jnp.max(s_n, axis=1, keepdims=True))
        p_c = jnp.exp(s_c - m)
        p_n = jnp.exp(s_n - m)
        l = jnp.sum(p_c, axis=1, keepdims=True) + jnp.sum(p_n, axis=1, keepdims=True)
        p_c = p_c / l
        p_n = p_n / l
        a_c = p_c[0:t] - lam * p_c[t:2 * t]
        a_n = p_n[0:t] - lam * p_n[t:2 * t]
        o = _dot(a_c.astype(_BF), vc_ref[head_rows, :].astype(_BF)) + _dot(a_n.astype(_BF), vn)
        outs.append(_rms(o) * gain_ref[...] * (1.0 - lam_init))
    o_ref[...] = jnp.concatenate(outs, axis=1).astype(_BF)


def _attn_a_sample(qa, cache_k, cache_v, kn, vn, lamv, a_gain, *, layer, lam_init, t_new):
    t_all, width = qa.shape
    _, nb, cache_rows, hd = cache_k.shape
    tok = pl.BlockSpec((t_new, width), lambda b: (b, 0))
    new = pl.BlockSpec((t_new * A_HEADS, hd), lambda b: (b, 0))
    cache = pl.BlockSpec((None, None, cache_rows, hd), lambda b: (layer, b, 0, 0))
    return pl.pallas_call(
        functools.partial(_attn_a_sample_kernel, lam_init=lam_init),
        grid=(nb,),
        in_specs=[_resident(lamv.shape, lambda b: (0, 0)), tok, cache, cache, new, new,
                  _resident((1, 2 * A_DIM), lambda b: (0, 0))],
        out_specs=tok,
        out_shape=jax.ShapeDtypeStruct((t_all, width), _BF),
        compiler_params=_params("arbitrary"),
        name="diff_attn_sample",
    )(lamv, qa, cache_k, cache_v, kn, vn, a_gain)


def _attn_b_sample_kernel(q_ref, kc_ref, vc_ref, kn_ref, vn_ref, bc_ref, bn_ref, gain_ref,
                          o_ref, ko_ref, vo_ref):
    q = q_ref[...]
    t = q.shape[0]
    keep = kc_ref.shape[0] - t
    zero = jnp.zeros_like(q)
    qall = jnp.concatenate(
        [jnp.where(_head_lanes(q.shape, h, B_DIM), q, zero) for h in range(B_HEADS)], axis=0)
    kc = kc_ref[...]
    vc = vc_ref[...]
    kn = kn_ref[...]
    vn = vn_ref[...]
    s_c = _dot_nt(qall, kc.astype(_BF)) + bc_ref[...]
    s_n = _dot_nt(qall, kn.astype(_BF)) + bn_ref[...]
    m = jnp.maximum(jnp.max(s_c, axis=1, keepdims=True), jnp.max(s_n, axis=1, keepdims=True))
    p_c = jnp.exp(s_c - m)
    p_n = jnp.exp(s_n - m)
    l = jnp.sum(p_c, axis=1, keepdims=True) + jnp.sum(p_n, axis=1, keepdims=True)
    o_all = (_dot(p_c.astype(_BF), vc.astype(_BF)) + _dot(p_n.astype(_BF), vn.astype(_BF))) / l
    o = jnp.zeros((t, q.shape[1]), _F32)
    for h in range(B_HEADS):
        o = jnp.where(_head_lanes(o.shape, h, B_DIM), o_all[h * t:(h + 1) * t], o)
    o_ref[...] = (_rms(o) * gain_ref[...]).astype(_BF)
    ko_ref[0:keep, :] = kc[t:, :]
    ko_ref[keep:, :] = kn
    vo_ref[0:keep, :] = vc[t:, :]
    vo_ref[keep:, :] = vn


def _sample_bias(table, t_new, wb):
    qpos = jnp.arange(t_new)[:, None] + wb
    kpos = jnp.arange(wb + t_new)[None, :]
    d = jnp.clip(qpos - kpos, -REL_CLIP, REL_CLIP) + REL_CLIP
    bias = table[:, d].astype(_F32).reshape(B_HEADS * t_new, wb + t_new)
    return bias[:, :wb], bias[:, wb:]


def _attn_b_sample(qb, cache_k, cache_v, kn, vn, rel_table, b_gain, *, layer, t_new):
    t_all, width = qb.shape
    _, nb, wb, _ = cache_k.shape
    bias_c, bias_n = _sample_bias(rel_table, t_new, wb)
    tok = pl.BlockSpec((t_new, width), lambda b: (b, 0))
    cache = pl.BlockSpec((None, None, wb, width), lambda b: (layer, b, 0, 0))
    state = pl.BlockSpec((None, wb, width), lambda b: (b, 0, 0))
    return pl.pallas_call(
        _attn_b_sample_kernel,
        grid=(nb,),
        in_specs=[tok, cache, cache, tok, tok,
                  _resident(bias_c.shape, lambda b: (0, 0)), _resident(bias_n.shape, lambda b: (0, 0)),
                  _resident((1, width), lambda b: (0, 0))],
        out_specs=[tok, state, state],
        out_shape=[jax.ShapeDtypeStruct((t_all, width), _BF),
                   jax.ShapeDtypeStruct((nb, wb, width), _F32),
                   jax.ShapeDtypeStruct((nb, wb, width), _F32)],
        compiler_params=_params("arbitrary"),
        name="band_attn_sample",
    )(qb, cache_k, cache_v, kn, vn, bias_c, bias_n, b_gain)


def _outproj_kernel(x_ref, oa_ref, ob_ref, w_ref, gt_ref, sh_ref, sc_ref, g_ref, x1_ref, ht_ref):
    half = oa_ref.shape[1]
    y = _dot(oa_ref[...], w_ref[0:half, :]) + _dot(ob_ref[...], w_ref[half:, :])
    x1 = x_ref[...] + gt_ref[...] * y
    x1_ref[...] = x1
    h = _rms(x1) * g_ref[...] * (1.0 + sc_ref[...]) + sh_ref[...]
    ht_ref[...] = h.T.astype(_BF)


def _out_projection(x, oa, ob, w_out_bf, gate, shift, scale, gain, *, per_row, seq):
    t_all, d = x.shape
    half = oa.shape[1]
    tm = min(PEER_TOKENS, t_all)
    mod_spec = _row_mod_specs(per_row, tm, d, max(seq // tm, 1))
    rows = lambda w: pl.BlockSpec((tm, w), lambda t: (t, 0))
    return pl.pallas_call(
        _outproj_kernel,
        grid=(t_all // tm,),
        in_specs=[rows(d), rows(half), rows(half), _resident(w_out_bf.shape, lambda t: (0, 0)),
                  mod_spec, mod_spec, mod_spec, _resident((1, d), lambda t: (0, 0))],
        out_specs=[rows(d), pl.BlockSpec((None, d, tm), lambda t: (t, 0, 0))],
        out_shape=[jax.ShapeDtypeStruct((t_all, d), _F32),
                   jax.ShapeDtypeStruct((t_all // tm, d, tm), _BF)],
        compiler_params=_params("arbitrary"),
        name="out_projection",
    )(x, oa, ob, w_out_bf, gate, shift, scale, gain)


def _ranked_top_values(s, k):
    vals = []
    rank = jnp.full(s.shape, float(k), _F32)
    for i in range(k):
        m = jnp.max(s, axis=0, keepdims=True)
        vals.append(m)
        hit = s == m
        rank = jnp.where(hit, float(i), rank)
        s = jnp.where(hit, -jnp.inf, s)
    return vals, rank


def _stack_rows(vals):
    rows = lax.broadcasted_iota(jnp.int32, (len(vals), LANES), 0)
    out = jnp.zeros((len(vals), LANES), _F32)
    for i, v in enumerate(vals):
        out = jnp.where(rows == i, v, out)
    return out


def _merge_sort_network(lo, hi):
    def merge(lo, hi, r):
        step = 2 * r
        if step < hi - lo:
            yield from merge(lo, hi, step)
            yield from merge(lo + r, hi, step)
            yield from ((i, i + r) for i in range(lo + r, hi - r, step))
        else:
            yield (lo, lo + r)

    if hi - lo >= 1:
        mid = lo + (hi - lo) // 2
        yield from _merge_sort_network(lo, mid)
        yield from _merge_sort_network(mid + 1, hi)
        yield from merge(lo, hi, 1)


def _compare_exchange(v, i, j):
    a, b = v[i], v[j]
    if b is None:
        return
    if a is None:
        v[i], v[j] = b, None
        return
    v[i], v[j] = jnp.maximum(a, b), jnp.minimum(a, b)


def _top16_replicated(tiles):
    n = PK_TOPK
    v = list(tiles) + [None] * (n - len(tiles))
    for i, j in _merge_sort_network(0, n - 1):
        _compare_exchange(v, i, j)
    shift = SUBLANES // 2
    while shift:
        w = [None if x is None else pltpu.roll(x, shift, 0) for x in v]
        c = []
        for k in range(n):
            a, b = v[k], w[n - 1 - k]
            c.append(b if a is None else a if b is None else jnp.maximum(a, b))
        stride = n // 2
        while stride:
            for i in range(n):
                if not i & stride:
                    _compare_exchange(c, i, i + stride)
            stride //= 2
        v = c
        shift //= 2
    return [jnp.full((SUBLANES, LANES), -jnp.inf, _F32) if x is None else x for x in v]


def _route_kernel(ht_ref, wq_ref, keys_ref, n1_ref, e1_ref, r2_ref, e2_ref, q_scr, s_scr):
    tt = ht_ref.shape[1]
    q_scr[...] = _dot(wq_ref[...], ht_ref[...]).astype(_BF)
    for hm in range(2 * PK_HEADS):
        s_scr[hm] = _dot(keys_ref[hm], q_scr[hm * PK_HALF:(hm + 1) * PK_HALF, :])

    def head(h, carry):
        for c in range(tt // LANES):
            cols = slice(c * LANES, (c + 1) * LANES)
            s1 = s_scr[2 * h, :, cols]
            s2 = s_scr[2 * h + 1, :, cols]
            a = _top16_replicated([s1[r:r + SUBLANES] for r in range(0, N_KEYS, SUBLANES)])
            b, rank2 = _ranked_top_values(s2, PK_TOPK)
            bmat = _stack_rows(b)
            b_lo, b_hi = bmat[0:SUBLANES], bmat[SUBLANES:]
            cand = [a[0] + b_lo, a[0] + b_hi]
            cand += [a[p] + b_lo for p in range(1, 8)]
            cand += [_stack_rows([a[p][0:1] for p in range(8, PK_TOPK)]) + b[0]]
            top = _top16_replicated(cand)
            tau = top[PK_TOPK - 1]
            z = jnp.ones_like(tau)
            for k in range(1, PK_TOPK):
                z = z + jnp.exp(top[k] - top[0])
            z_row = z[0:1]
            count1 = jnp.zeros(s1.shape, _F32)
            for p in range(PK_TOPK):
                hit = jnp.where(a[p] + b_lo >= tau, 1.0, 0.0) + jnp.where(a[p] + b_hi >= tau, 1.0, 0.0)
                reach = jnp.sum(hit, axis=0, keepdims=True)
                count1 = jnp.where(s1 == jnp.tile(a[p], (N_KEYS // SUBLANES, 1)), reach, count1)
            n1_ref[h, :, cols] = count1
            e1_ref[h, :, cols] = jnp.exp(s1 - a[0][0:1]) * (0.5 / z_row)
            r2_ref[h, :, cols] = rank2.astype(_BF)
            e2_ref[h, :, cols] = jnp.exp(s2 - b[0]).astype(_BF)
        return carry

    lax.fori_loop(0, PK_HEADS, head, 0)


def _peer_route(ht, wq_t_bf, keys_bf):
    slabs, d, slab = ht.shape
    t_all = slabs * slab
    tt = min(ROUTE_TOKENS, slab)
    per_slab = slab // tt
    nq = wq_t_bf.shape[0]
    out = lambda dt: jax.ShapeDtypeStruct((PK_HEADS, N_KEYS, t_all), dt)
    tile = pl.BlockSpec((PK_HEADS, N_KEYS, tt), lambda t: (0, 0, t))
    return pl.pallas_call(
        _route_kernel,
        grid=(t_all // tt,),
        in_specs=[pl.BlockSpec((None, d, tt), lambda t: (t // per_slab, 0, t % per_slab)),
                  _resident(wq_t_bf.shape, lambda t: (0, 0)),
                  _resident(keys_bf.shape, lambda t: (0, 0, 0))],
        out_specs=[tile] * 4,
        out_shape=[out(_F32), out(_F32), out(_BF), out(_BF)],
        scratch_shapes=[pltpu.VMEM((nq, tt), _BF), pltpu.VMEM((2 * PK_HEADS, N_KEYS, tt), _F32)],
        compiler_params=_params("arbitrary"),
        name="peer_route",
    )(ht, wq_t_bf, keys_bf)


def _twice_gelu_tanh(x):
    return x * (1.0 + jnp.tanh(x * (0.7978845608028654 + (0.7978845608028654 * 0.044715) * (x * x))))


def _row_to_tile(row, rows):
    packed = jnp.broadcast_to(row, (16, LANES)).astype(_BF)
    return jnp.tile(packed, (rows // 16, 1))


def _peer_kernel(ht_ref, u_ref, vt_ref, n1_ref, e1_ref, r2_ref, e2_ref, x_ref, gt_ref, gf_ref,
                 o_ref, at_scr, wt_scr, acc, r2_scr, e2_scr, *, final_norm):
    e = pl.program_id(1)
    tt = ht_ref.shape[1]
    groups = u_ref.shape[0] // N_KEYS
    span = at_scr.shape[2]
    n_chunks = tt // span

    @pl.when(e == 0)
    def _():
        acc[...] = jnp.zeros(acc.shape, _F32)
        r2_scr[...] = r2_ref[...]
        e2_scr[...] = e2_ref[...]

    for c in range(n_chunks):
        at_scr[c] = _dot(u_ref[...], ht_ref[:, c * span:(c + 1) * span])
    zero = jnp.zeros((N_KEYS, LANES), _BF)

    for c in range(n_chunks):
        for sub in range(span // LANES):
            cols = slice(c * span + sub * LANES, c * span + (sub + 1) * LANES)
            sub_cols = slice(sub * LANES, (sub + 1) * LANES)
            for il in range(groups):
                rows = slice(il * N_KEYS, (il + 1) * N_KEYS)
                g = None
                for h in range(PK_HEADS):
                    count = _row_to_tile(n1_ref[h, il:il + 1, cols], N_KEYS)
                    sel = jnp.where(r2_scr[h, :, cols] < count, e2_scr[h, :, cols], zero)
                    term = sel * _row_to_tile(e1_ref[h, il:il + 1, cols], N_KEYS)
                    g = term if g is None else g + term
                act = _twice_gelu_tanh(at_scr[c, rows, sub_cols]).astype(_BF)
                wt_scr[c, rows, sub_cols] = g * act
        acc[c] += _dot(vt_ref[...], wt_scr[c])

    @pl.when(e == pl.num_programs(1) - 1)
    def _():
        f = jnp.concatenate([acc[c].T for c in range(n_chunks)], axis=0)
        y = x_ref[...] + gt_ref[...] * f
        if final_norm:
            y = _rms(y) * gf_ref[...]
        o_ref[...] = y


def _peer_experts(ht, u_bf, vt_bf, route, x1, gate, g_final, *, layer, per_row, seq, final_norm):
    tiles, d, tt = ht.shape
    _, steps, _, eb = vt_bf.shape
    t_all = tiles * tt
    span = min(PEER_CHUNK, tt)
    groups = eb // N_KEYS
    assert groups == SUBLANES
    n1, e1, r2, e2 = route
    by_step = lambda a: a.reshape(PK_HEADS, N_KEYS // groups, groups, t_all)
    gate_spec = (pl.BlockSpec((tt, d), lambda t, e: (t, 0)) if per_row else
                 pl.BlockSpec((None, 1, d), lambda t, e: (t // max(seq // tt, 1), 0, 0)))
    step_rows = pl.BlockSpec((PK_HEADS, None, groups, tt), lambda t, e: (0, e, 0, t))
    tile = pl.BlockSpec((PK_HEADS, N_KEYS, tt), lambda t, e: (0, 0, t))
    return pl.pallas_call(
        functools.partial(_peer_kernel, final_norm=final_norm),
        grid=(tiles, steps),
        in_specs=[pl.BlockSpec((None, d, tt), lambda t, e: (t, 0, 0)),
                  pl.BlockSpec((None, eb, d), lambda t, e: (layer, e, 0)),
                  pl.BlockSpec((None, None, d, eb), lambda t, e: (layer, e, 0, 0)),
                  step_rows, step_rows, tile, tile,
                  pl.BlockSpec((tt, d), lambda t, e: (t, 0)),
                  gate_spec,
                  _resident((1, d), lambda t, e: (0, 0))],
        out_specs=pl.BlockSpec((tt, d), lambda t, e: (t, 0)),
        out_shape=jax.ShapeDtypeStruct((t_all, d), _F32),
        scratch_shapes=[pltpu.VMEM((tt // span, eb, span), _F32), pltpu.VMEM((tt // span, eb, span), _BF),
                        pltpu.VMEM((tt // span, d, span), _F32),
                        pltpu.VMEM((PK_HEADS, N_KEYS, tt), _BF), pltpu.VMEM((PK_HEADS, N_KEYS, tt), _BF)],
        compiler_params=_params("arbitrary", "arbitrary"),
        name="peer_experts",
    )(ht, u_bf, vt_bf, by_step(n1), by_step(e1), r2, e2, x1, gate, g_final)


def _rope_tables(pos):
    half = ROT_DIM // 2
    inv = ROPE_THETA ** (-jnp.arange(0, ROT_DIM, 2, dtype=_F32) / ROT_DIM)
    ang = pos.astype(_F32)[:, None] * inv[None, :]
    cos, sin = jnp.cos(ang), jnp.sin(ang)
    n = pos.shape[0]
    pad = jnp.zeros((n, A_DIM - ROT_DIM), _F32)
    zeros = jnp.zeros((n, half), _F32)
    c = jnp.concatenate([cos, cos, pad + 1.0], axis=1)
    sa = jnp.concatenate([-sin, zeros, pad], axis=1)
    sb = jnp.concatenate([zeros, sin, pad], axis=1)
    return tuple(jnp.tile(t, (1, LANES // A_DIM)) for t in (c, sa, sb))


def _stream_layer(x, mod, lw, tables, rope_tabs, mixer, *, per_row, seq, final_norm, g_final, stack=None):
    shift1, scale1, gate1, shift2, scale2, gate2 = mod
    proj = _in_projection(x, shift1, scale1, lw["g_attn"], lw["w_in"], rope_tabs, per_row=per_row, seq=seq,
                          stack=stack)
    qa, kaf, vaf, kab, vab, qb, kbf, vbf, kbb, vbb = proj
    oa, ob, extra = mixer(qa, kaf, vaf, kab, vab, qb, kbf, vbf, kbb, vbb)
    x1, ht = _out_projection(x, oa, ob, lw["w_out"], gate1, shift2, scale2, lw["g_ffn"],
                             per_row=per_row, seq=seq)
    route = _peer_route(ht, lw["pk_wq_t"], lw["pk_keys"])
    x2 = _peer_experts(ht, *tables, route, x1, gate2, g_final, layer=lw["layer"],
                       per_row=per_row, seq=seq, final_norm=final_norm)
    return x2, (kaf, vaf, kbf, vbf) + extra


def kernel(x_prompt, x_sample, c_prompt, c_sample, cache_a_k, cache_a_v, cache_b_k, cache_b_v, w_ada, b_ada, g_attn, g_ffn, w_in, lam_q1, lam_k1, lam_q2, lam_k2, a_gain, rel_bias, b_gain, w_out, pk_wq, pk_keys, pk_u, pk_v, g_final):
    batch, seq, d = x_prompt.shape
    nb, t_new, _ = x_sample.shape
    depth = w_ada.shape[0]
    past = cache_a_k.shape[2]
    wb = cache_b_k.shape[2]
    a_width = A_HEADS * 2 * A_DIM
    b_width = B_HEADS * B_DIM

    c_all = jnp.concatenate([c_prompt, c_sample], axis=0)
    pad_rows = (-c_all.shape[0]) % 16
    c_all = jnp.pad(c_all, ((0, pad_rows), (0, 0)))
    mod = _modulation(c_all, w_ada, b_ada)

    tabs_p = _rope_tables(jnp.arange(seq, dtype=jnp.int32))
    pos_s = past + jnp.arange(t_new, dtype=jnp.int32)
    tabs_s = tuple(jnp.tile(t, (nb, 1)) for t in _rope_tables(pos_s))

    cak = cache_a_k.reshape(depth, nb, past * A_HEADS, 2 * A_DIM)
    cav = cache_a_v.reshape(depth, nb, past * A_HEADS, 2 * A_DIM)
    cbk = cache_b_k.reshape(depth, nb, wb, b_width)
    cbv = cache_b_v.reshape(depth, nb, wb, b_width)

    u_bf = pk_u.astype(_BF)
    vt_bf = pk_v.astype(_BF).reshape(depth, -1, PEER_EXPERTS, d).transpose(0, 1, 3, 2)

    xp = x_prompt.reshape(batch * seq, d)
    xs = x_sample.reshape(nb * t_new, d)
    g_fin = g_final.reshape(1, d)
    st_p, st_s = [], []
    for l in range(depth):
        lam_init = 0.8 - 0.6 * math.exp(-0.3 * l)
        lamv = jnp.stack([lam_q1[l], lam_k1[l], lam_q2[l], lam_k2[l]])
        lw = {
            "g_attn": g_attn[l].reshape(1, d),
            "g_ffn": g_ffn[l].reshape(1, d),
            "w_in": w_in[l].astype(_BF),
            "w_out": w_out[l].astype(_BF),
            "pk_wq_t": pk_wq[l].T.astype(_BF),
            "pk_keys": pk_keys[l].reshape(2 * PK_HEADS, N_KEYS, PK_HALF).astype(_BF),
            "layer": l,
        }
        gain_a = a_gain[l].reshape(1, 2 * A_DIM)
        gain_b = b_gain[l].reshape(1, b_width)
        mod_p = tuple(m.reshape(batch, 1, d) for m in jnp.split(mod[l, :batch], 6, axis=-1))
        mod_s = tuple(jnp.repeat(m, t_new, axis=0) for m in jnp.split(mod[l, batch:batch + nb], 6, axis=-1))
        last = l == depth - 1

        def mix_p(qa, kaf, vaf, kab, vab, qb, kbf, vbf, kbb, vbb):
            oa = _attn_a_prompt(qa, kab, vab, lamv, gain_a, batch=batch, seq=seq, lam_init=lam_init)
            ob = _attn_b_prompt(qb, kbb, vbb, rel_bias[l], gain_b, batch=batch, seq=seq)
            return oa, ob, ()

        def mix_s(qa, kaf, vaf, kab, vab, qb, kbf, vbf, kbb, vbb):
            oa = _attn_a_sample(qa, cak, cav, kaf, vaf, lamv, gain_a, layer=l, lam_init=lam_init, t_new=t_new)
            ob, nbk, nbv = _attn_b_sample(qb, cbk, cbv, kbf, vbf, rel_bias[l], gain_b, layer=l, t_new=t_new)
            return oa, ob, (nbk, nbv)

        previous = st_p[-1][0:2] if st_p else None
        xp, sp = _stream_layer(xp, mod_p, lw, (u_bf, vt_bf), tabs_p, mix_p, per_row=False, seq=seq,
                               final_norm=last, g_final=g_fin, stack=(l, depth, previous))
        xs, ss = _stream_layer(xs, mod_s, lw, (u_bf, vt_bf), tabs_s, mix_s, per_row=True, seq=t_new,
                               final_norm=last, g_final=g_fin)
        st_p.append(sp)
        st_s.append(ss)

    keep = min(B_REACH, seq)
    a_shape_p = (batch, seq, A_HEADS, 2 * A_DIM)
    b_tail = lambda s: s.reshape(batch, seq, b_width)[:, seq - keep:].reshape(batch, keep, B_HEADS, B_DIM)
    a_shape_s = (nb, t_new, A_HEADS, 2 * A_DIM)
    b_shape_s = (nb, wb, B_HEADS, B_DIM)
    return (
        xp.reshape(batch, seq, d),
        xs.reshape(nb, t_new, d),
        st_p[-1][0].reshape((depth,) + a_shape_p),
        st_p[-1][1].reshape((depth,) + a_shape_p),
        jnp.stack([b_tail(s[2]) for s in st_p]),
        jnp.stack([b_tail(s[3]) for s in st_p]),
        jnp.stack([s[0].reshape(a_shape_s) for s in st_s]),
        jnp.stack([s[1].reshape(a_shape_s) for s in st_s]),
        jnp.stack([s[4].reshape(b_shape_s) for s in st_s]),
        jnp.stack([s[5].reshape(b_shape_s) for s in st_s]),
    )
```

```python
import functools
import math

import jax
import jax.numpy as jnp
import numpy as np
from jax import lax
from jax.experimental import pallas as pl
from jax.experimental.pallas import tpu as pltpu

_F32 = jnp.float32
_BF = jnp.bfloat16

CHUNK = 64
A_HEADS = 4
A_DIM = 64
ROT_DIM = A_DIM // 4
ROPE_THETA = 500000.0
B_HEADS = 8
B_DIM = 64
LEFT_CHUNKS = 8
B_REACH = LEFT_CHUNKS * CHUNK
REL_CLIP = 128
PK_HEADS = 8
N_KEYS = 128
PK_TOPK = 16
PK_HALF = 128
EPS = 1e-6

LANES = 128
SUBLANES = 8
MASKED = -1e30
VMEM_LIMIT = 56 * 1024 * 1024

A_TILE = 512
B_TILE = 256
PEER_TOKENS = 1024
ROUTE_TOKENS = 512
PEER_EXPERTS = 1024
PEER_CHUNK = 256
ROW_TILE = 512


def _params(*sem):
    return pltpu.CompilerParams(dimension_semantics=sem, vmem_limit_bytes=VMEM_LIMIT)


def _resident(shape, index_map):
    return pl.BlockSpec(shape, index_map, pipeline_mode=pl.Buffered(1))


def _rms(x):
    return x * lax.rsqrt(jnp.mean(x * x, axis=-1, keepdims=True) + EPS)


def _dot(a, b):
    return jnp.dot(a, b, preferred_element_type=_F32)


def _dot_nt(a, b):
    return lax.dot_general(a, b, (((1,), (1,)), ((), ())), preferred_element_type=_F32)


def _mod_kernel(c_ref, w_ref, b_ref, o_ref):
    c = c_ref[...]
    s = c * (1.0 / (1.0 + jnp.exp(-c)))
    o_ref[...] = _dot(s.astype(_BF), w_ref[...].astype(_BF)) + b_ref[...]


def _modulation(c_all, w_ada, b_ada):
    depth, d, n = w_ada.shape
    rows = c_all.shape[0]
    tn = n // 4
    return pl.pallas_call(
        _mod_kernel,
        grid=(depth, n // tn),
        in_specs=[
            pl.BlockSpec((rows, d), lambda l, j: (0, 0)),
            pl.BlockSpec((None, d, tn), lambda l, j: (l, 0, j)),
            pl.BlockSpec((None, 1, tn), lambda l, j: (l, 0, j)),
        ],
        out_specs=pl.BlockSpec((None, rows, tn), lambda l, j: (l, 0, j)),
        out_shape=jax.ShapeDtypeStruct((depth, rows, n), _F32),
        compiler_params=_params("arbitrary", "arbitrary"),
        name="adaln_mod",
    )(c_all, w_ada, b_ada.reshape(depth, 1, n))


def _inproj_kernel(x_ref, sh_ref, sc_ref, g_ref, w_ref, cos_ref, sa_ref, sb_ref, *rest, width, carried):
    (qa_ref, kaf_ref, vaf_ref, kab_ref, vab_ref,
     qb_ref, kbf_ref, vbf_ref, kbb_ref, vbb_ref) = rest[2 * carried:]
    h = _rms(x_ref[...]) * g_ref[...] * (1.0 + sc_ref[...]) + sh_ref[...]
    hb = h.astype(_BF)
    reps = width // LANES
    cos = jnp.tile(cos_ref[...], (1, reps))
    sa = jnp.tile(sa_ref[...], (1, reps))
    sb = jnp.tile(sb_ref[...], (1, reps))

    def rope(z):
        return z * cos + pltpu.roll(z, width - ROT_DIM // 2, 1) * sa + pltpu.roll(z, ROT_DIM // 2, 1) * sb

    def proj(i):
        return _dot(hb, w_ref[:, i * width:(i + 1) * width])

    def store_state(ref, z):
        hd = 2 * A_DIM
        for head in range(A_HEADS):
            ref[pl.ds(head, z.shape[0], stride=A_HEADS), :] = z[:, head * hd:(head + 1) * hd]

    qa_ref[...] = (rope(proj(0)) * (A_DIM ** -0.5)).astype(_BF)
    ka = rope(proj(1))
    store_state(kaf_ref, ka)
    kab_ref[...] = ka.astype(_BF)
    va = proj(2)
    store_state(vaf_ref, va)
    vab_ref[...] = va.astype(_BF)
    qb_ref[...] = (proj(3) * (B_DIM ** -0.5)).astype(_BF)
    kb = proj(4)
    kbf_ref[...] = kb
    kbb_ref[...] = kb.astype(_BF)
    vb = proj(5)
    vbf_ref[...] = vb
    vbb_ref[...] = vb.astype(_BF)


def _row_mod_specs(per_row, tm, d, tiles_per_batch):
    if per_row:
        return pl.BlockSpec((tm, d), lambda t: (t, 0))
    return pl.BlockSpec((None, 1, d), lambda t: (t // tiles_per_batch, 0, 0))


def _in_projection(x, shift, scale, gain, w_in_bf, rope_tabs, *, per_row, seq, stack=None):
    t_all, d = x.shape
    width = w_in_bf.shape[1] // 6
    tm = min(ROW_TILE, t_all)
    tiles_per_batch = max(seq // tm, 1)
    tab_tiles = rope_tabs[0].shape[0] // tm
    mod_spec = _row_mod_specs(per_row, tm, d, tiles_per_batch)
    tab_spec = pl.BlockSpec((tm, LANES), lambda t: (t % tab_tiles, 0))
    row = lambda dt: jax.ShapeDtypeStruct((t_all, width), dt)
    out_spec = pl.BlockSpec((tm, width), lambda t: (t, 0))
    dts = [_BF, _F32, _F32, _BF, _BF, _BF, _F32, _F32, _BF, _BF]
    out_shapes = [row(dt) for dt in dts]
    out_specs = [out_spec] * 10
    if not per_row:
        assert tm == min(B_REACH, seq)
        for i in (6, 7):
            out_shapes[i] = jax.ShapeDtypeStruct((t_all // seq * tm, width), _F32)
            out_specs[i] = pl.BlockSpec((tm, width), lambda t: (t // tiles_per_batch, 0))
    state_rows, hd = t_all * A_HEADS, 2 * A_DIM
    in_specs = [
        pl.BlockSpec((tm, d), lambda t: (t, 0)),
        mod_spec, mod_spec,
        _resident((1, d), lambda t: (0, 0)),
        _resident(w_in_bf.shape, lambda t: (0, 0)),
        tab_spec, tab_spec, tab_spec,
    ]
    args = [x, shift, scale, gain, w_in_bf, *rope_tabs]
    aliases = {}
    carried = False
    for i in (1, 2):
        if stack is None:
            out_shapes[i] = jax.ShapeDtypeStruct((state_rows, hd), _F32)
            out_specs[i] = pl.BlockSpec((tm * A_HEADS, hd), lambda t: (t, 0))
        else:
            layer, depth, previous = stack
            out_shapes[i] = jax.ShapeDtypeStruct((depth, state_rows, hd), _F32)
            out_specs[i] = pl.BlockSpec((None, tm * A_HEADS, hd), lambda t: (layer, t, 0))
            if previous is not None:
                carried = True
                aliases[len(args)] = i
                in_specs.append(pl.BlockSpec(memory_space=pl.ANY))
                args.append(previous[i - 1])
    return pl.pallas_call(
        functools.partial(_inproj_kernel, width=width, carried=carried),
        grid=(t_all // tm,),
        in_specs=in_specs,
        out_specs=out_specs,
        out_shape=out_shapes,
        input_output_aliases=aliases,
        compiler_params=_params("arbitrary"),
        name="in_projection",
    )(*args)


def _diff_lambda(lamv_ref, lam_init):
    v = lamv_ref[...]
    d1 = jnp.sum(v[0:1] * v[1:2], axis=1, keepdims=True)
    d2 = jnp.sum(v[2:3] * v[3:4], axis=1, keepdims=True)
    return jnp.exp(d1) - jnp.exp(d2) + lam_init


def _split_maps(q):
    lane = lax.broadcasted_iota(jnp.int32, q.shape, 1)
    zero = jnp.zeros_like(q)
    return jnp.where(lane < A_DIM, q, zero), jnp.where(lane >= A_DIM, q, zero)


def _attn_a_kernel(lamv_ref, q_ref, k_ref, v_ref, gain_ref, o_ref,
                   sa, sb, m1, l1, acc1, m2, l2, acc2, *, lam_init):
    n_full = pl.program_id(2)
    tq = q_ref.shape[0]
    q1, q2 = _split_maps(q_ref[...])
    for m_scr, l_scr, acc in ((m1, l1, acc1), (m2, l2, acc2)):
        m_scr[...] = jnp.full(m_scr.shape, MASKED, _F32)
        l_scr[...] = jnp.zeros(l_scr.shape, _F32)
        acc[...] = jnp.zeros(acc.shape, _F32)

    def rows(j):
        return pl.ds(pl.multiple_of(j * tq, tq), tq)

    def scores(j, s_scr):
        kb = k_ref[rows(j), :]
        s_scr[0] = _dot_nt(q1, kb)
        s_scr[1] = _dot_nt(q2, kb)

    def update(s, m_scr, l_scr, acc, vb):
        m_prev = m_scr[...]
        m_next = jnp.maximum(m_prev, jnp.max(s, axis=1, keepdims=True))
        p = jnp.exp(s - jnp.tile(m_next, (1, tq // LANES)))
        alpha = jnp.exp(m_prev - m_next)
        part = p[:, 0:LANES]
        for c in range(1, tq // LANES):
            part = part + p[:, c * LANES:(c + 1) * LANES]
        l_scr[...] = alpha * l_scr[...] + part
        acc[...] = acc[...] * alpha + _dot(p.astype(_BF), vb)
        m_scr[...] = m_next

    def process(j, s_scr, diagonal):
        vb = v_ref[rows(j), :]
        for mp, (m_scr, l_scr, acc) in enumerate(((m1, l1, acc1), (m2, l2, acc2))):
            s = s_scr[mp]
            if diagonal:
                qc = lax.broadcasted_iota(jnp.int32, s.shape, 0) // CHUNK
                kc = lax.broadcasted_iota(jnp.int32, s.shape, 1) // CHUNK
                s = jnp.where(kc <= qc, s, MASKED)
            update(s, m_scr, l_scr, acc, vb)

    scores(0, sa)

    def pair(i, carry):
        scores(2 * i + 1, sb)
        process(2 * i, sa, False)
        scores(2 * i + 2, sa)
        process(2 * i + 1, sb, False)
        return carry

    lax.fori_loop(0, n_full // 2, pair, 0)

    @pl.when(n_full % 2 == 1)
    def _():
        scores(n_full, sb)
        process(n_full - 1, sa, False)
        process(n_full, sb, True)

    @pl.when(n_full % 2 == 0)
    def _():
        process(n_full, sa, True)

    lam = _diff_lambda(lamv_ref, lam_init)
    d1 = jnp.sum(l1[...], axis=1, keepdims=True)
    d2 = jnp.sum(l2[...], axis=1, keepdims=True)
    o = acc1[...] / d1 - lam * (acc2[...] / d2)
    o_ref[...] = (_rms(o) * gain_ref[...] * (1.0 - lam_init)).astype(_BF)


def _attn_a_prompt(qa, ka, va, lamv, a_gain, *, batch, seq, lam_init):
    t_all, width = qa.shape
    hd = 2 * A_DIM
    tq = A_TILE
    nq = seq // tq
    return pl.pallas_call(
        functools.partial(_attn_a_kernel, lam_init=lam_init),
        grid=(batch, A_HEADS, nq),
        in_specs=[
            _resident(lamv.shape, lambda b, h, i: (0, 0)),
            pl.BlockSpec((tq, hd), lambda b, h, i: (b * nq + i, h)),
            pl.BlockSpec((seq, hd), lambda b, h, i: (b, h)),
            pl.BlockSpec((seq, hd), lambda b, h, i: (b, h)),
            _resident((1, hd), lambda b, h, i: (0, 0)),
        ],
        out_specs=pl.BlockSpec((tq, hd), lambda b, h, i: (b * nq + i, h)),
        out_shape=jax.ShapeDtypeStruct((t_all, width), _BF),
        scratch_shapes=[pltpu.VMEM((2, tq, tq), _F32)] * 2 + [pltpu.VMEM((tq, hd), _F32)] * 6,
        compiler_params=_params("arbitrary", "arbitrary", "arbitrary"),
        name="diff_attn_prompt",
    )(lamv, qa, ka, va, a_gain)


def _head_lanes(shape, head, dim):
    lane = lax.broadcasted_iota(jnp.int32, shape, 1)
    return (lane >= head * dim) & (lane < (head + 1) * dim)


def _attn_b_kernel(q_ref, k0_ref, k1_ref, k2_ref, v0_ref, v1_ref, v2_ref, bias_ref, gain_ref, o_ref):
    qi = pl.program_id(1)
    tq = q_ref.shape[0]
    k_refs = (k0_ref, k1_ref, k2_ref)
    v_refs = (v0_ref, v1_ref, v2_ref)
    pen = (jnp.where(qi >= 2, 0.0, MASKED), jnp.where(qi >= 1, 0.0, MASKED), 0.0)
    outs = []
    for pair in range(B_HEADS // 2):
        cols = slice(pair * LANES, (pair + 1) * LANES)
        qp = q_ref[:, cols]
        zero = jnp.zeros_like(qp)
        halves = []
        for sub in range(2):
            head = 2 * pair + sub
            qh = jnp.where(_head_lanes(qp.shape, sub, B_DIM), qp, zero)
            s = [_dot_nt(qh, k_refs[b][:, cols]) + bias_ref[head, :, b * tq:(b + 1) * tq] + pen[b]
                 for b in range(3)]
            m = jnp.maximum(jnp.maximum(jnp.max(s[0], axis=1, keepdims=True),
                                        jnp.max(s[1], axis=1, keepdims=True)),
                            jnp.max(s[2], axis=1, keepdims=True))
            p = [jnp.exp(sb - m) for sb in s]
            l = (jnp.sum(p[0], axis=1, keepdims=True) + jnp.sum(p[1], axis=1, keepdims=True)
                 + jnp.sum(p[2], axis=1, keepdims=True))
            o = (_dot(p[0].astype(_BF), v_refs[0][:, cols]) + _dot(p[1].astype(_BF), v_refs[1][:, cols])
                 + _dot(p[2].astype(_BF), v_refs[2][:, cols]))
            halves.append(o / l)
        outs.append(jnp.where(_head_lanes(halves[0].shape, 0, B_DIM), halves[0], halves[1]))
    o = jnp.concatenate(outs, axis=1)
    o_ref[...] = (_rms(o) * gain_ref[...]).astype(_BF)


def _band_bias(table, tq):
    nk = B_REACH + tq
    qpos = np.arange(tq)[:, None]
    kpos = np.arange(nk)[None, :] - B_REACH
    qc = qpos // CHUNK
    kc = np.floor_divide(kpos, CHUNK)
    valid = (kc <= qc) & (kc >= qc - LEFT_CHUNKS)
    span = nk + tq - 1
    dist = (span - 1 - np.arange(span)) - (tq - 1)
    line = table[:, np.clip(dist, -REL_CLIP, REL_CLIP) + REL_CLIP].astype(_F32)
    line = jnp.pad(line, ((0, 0), (0, 1)))
    skew = jnp.tile(line, (1, tq))[:, :tq * span].reshape(-1, tq, span)
    bias = skew[:, :, tq - 1:]
    return jnp.where(valid[None], bias, MASKED)


def _attn_b_prompt(qb, kb, vb, rel_table, b_gain, *, batch, seq):
    t_all, width = qb.shape
    tq = B_TILE
    assert B_REACH == 2 * tq
    nq = seq // tq
    bias = _band_bias(rel_table, tq)
    qmap = lambda b, i: (b * nq + i, 0)
    kmap = lambda back: (lambda b, i: (b * nq + jnp.maximum(i - back, 0), 0))
    blk = lambda f: pl.BlockSpec((tq, width), f)
    return pl.pallas_call(
        _attn_b_kernel,
        grid=(batch, nq),
        in_specs=[blk(qmap), blk(kmap(2)), blk(kmap(1)), blk(kmap(0)),
                  blk(kmap(2)), blk(kmap(1)), blk(kmap(0)),
                  _resident(bias.shape, lambda b, i: (0, 0, 0)),
                  _resident((1, width), lambda b, i: (0, 0))],
        out_specs=blk(qmap),
        out_shape=jax.ShapeDtypeStruct((t_all, width), _BF),
        compiler_params=_params("arbitrary", "arbitrary"),
        name="band_attn_prompt",
    )(qb, kb, kb, kb, vb, vb, vb, bias, b_gain)


def _attn_a_sample_kernel(lamv_ref, q_ref, kc_ref, vc_ref, kn_ref, vn_ref, gain_ref, o_ref, *, lam_init):
    q = q_ref[...]
    t = q.shape[0]
    hd = 2 * A_DIM
    past = kc_ref.shape[0] // A_HEADS
    lam = _diff_lambda(lamv_ref, lam_init)
    outs = []
    for h in range(A_HEADS):
        cols = slice(h * hd, (h + 1) * hd)
        head_rows = pl.ds(h, past, stride=A_HEADS)
        new_rows = pl.ds(h, t, stride=A_HEADS)
        q12 = jnp.concatenate(_split_maps(q[:, cols]), axis=0)
        kn = kn_ref[new_rows, :].astype(_BF)
        vn = vn_ref[new_rows, :].astype(_BF)
        s_c = _dot_nt(q12, kc_ref[head_rows, :].astype(_BF))
        s_n = _dot_nt(q12, kn)
        m = jnp.maximum(jnp.max(s_c, axis=1, keepdims=True), jnp.max(s_n, axis=1, keepdims=True))
        p_c = jnp.exp(s_c - m)
        p_n = jnp.exp(s_n - m)
        l = jnp.sum(p_c, axis=1, keepdims=True) + jnp.sum(p_n, axis=1, keepdims=True)
        p_c = p_c / l
        p_n = p_n / l
        a_c = p_c[0:t] - lam * p_c[t:2 * t]
        a_n = p_n[0:t] - lam * p_n[t:2 * t]
        o = _dot(a_c.astype(_BF), vc_ref[head_rows, :].astype(_BF)) + _dot(a_n.astype(_BF), vn)
        outs.append(_rms(o) * gain_ref[...] * (1.0 - lam_init))
    o_ref[...] = jnp.concatenate(outs, axis=1).astype(_BF)


def _attn_a_sample(qa, cache_k, cache_v, kn, vn, lamv, a_gain, *, layer, lam_init, t_new):
    t_all, width = qa.shape
    _, nb, cache_rows, hd = cache_k.shape
    tok = pl.BlockSpec((t_new, width), lambda b: (b, 0))
    new = pl.BlockSpec((t_new * A_HEADS, hd), lambda b: (b, 0))
    cache = pl.BlockSpec((None, None, cache_rows, hd), lambda b: (layer, b, 0, 0))
    return pl.pallas_call(
        functools.partial(_attn_a_sample_kernel, lam_init=lam_init),
        grid=(nb,),
        in_specs=[_resident(lamv.shape, lambda b: (0, 0)), tok, cache, cache, new, new,
                  _resident((1, 2 * A_DIM), lambda b: (0, 0))],
        out_specs=tok,
        out_shape=jax.ShapeDtypeStruct((t_all, width), _BF),
        compiler_params=_params("arbitrary"),
        name="diff_attn_sample",
    )(lamv, qa, cache_k, cache_v, kn, vn, a_gain)


def _attn_b_sample_kernel(q_ref, kc_ref, vc_ref, kn_ref, vn_ref, bc_ref, bn_ref, gain_ref,
                          o_ref, ko_ref, vo_ref):
    q = q_ref[...]
    t = q.shape[0]
    keep = kc_ref.shape[0] - t
    zero = jnp.zeros_like(q)
    qall = jnp.concatenate(
        [jnp.where(_head_lanes(q.shape, h, B_DIM), q, zero) for h in range(B_HEADS)], axis=0)
    kc = kc_ref[...]
    vc = vc_ref[...]
    kn = kn_ref[...]
    vn = vn_ref[...]
    s_c = _dot_nt(qall, kc.astype(_BF)) + bc_ref[...]
    s_n = _dot_nt(qall, kn.astype(_BF)) + bn_ref[...]
    m = jnp.maximum(jnp.max(s_c, axis=1, keepdims=True), jnp.max(s_n, axis=1, keepdims=True))
    p_c = jnp.exp(s_c - m)
    p_n = jnp.exp(s_n - m)
    l = jnp.sum(p_c, axis=1, keepdims=True) + jnp.sum(p_n, axis=1, keepdims=True)
    o_all = (_dot(p_c.astype(_BF), vc.astype(_BF)) + _dot(p_n.astype(_BF), vn.astype(_BF))) / l
    o = jnp.zeros((t, q.shape[1]), _F32)
    for h in range(B_HEADS):
        o = jnp.where(_head_lanes(o.shape, h, B_DIM), o_all[h * t:(h + 1) * t], o)
    o_ref[...] = (_rms(o) * gain_ref[...]).astype(_BF)
    ko_ref[0:keep, :] = kc[t:, :]
    ko_ref[keep:, :] = kn
    vo_ref[0:keep, :] = vc[t:, :]
    vo_ref[keep:, :] = vn


def _sample_bias(table, t_new, wb):
    qpos = jnp.arange(t_new)[:, None] + wb
    kpos = jnp.arange(wb + t_new)[None, :]
    d = jnp.clip(qpos - kpos, -REL_CLIP, REL_CLIP) + REL_CLIP
    bias = table[:, d].astype(_F32).reshape(B_HEADS * t_new, wb + t_new)
    return bias[:, :wb], bias[:, wb:]


def _attn_b_sample(qb, cache_k, cache_v, kn, vn, rel_table, b_gain, *, layer, t_new):
    t_all, width = qb.shape
    _, nb, wb, _ = cache_k.shape
    bias_c, bias_n = _sample_bias(rel_table, t_new, wb)
    tok = pl.BlockSpec((t_new, width), lambda b: (b, 0))
    cache = pl.BlockSpec((None, None, wb, width), lambda b: (layer, b, 0, 0))
    state = pl.BlockSpec((None, wb, width), lambda b: (b, 0, 0))
    return pl.pallas_call(
        _attn_b_sample_kernel,
        grid=(nb,),
        in_specs=[tok, cache, cache, tok, tok,
                  _resident(bias_c.shape, lambda b: (0, 0)), _resident(bias_n.shape, lambda b: (0, 0)),
                  _resident((1, width), lambda b: (0, 0))],
        out_specs=[tok, state, state],
        out_shape=[jax.ShapeDtypeStruct((t_all, width), _BF),
                   jax.ShapeDtypeStruct((nb, wb, width), _F32),
                   jax.ShapeDtypeStruct((nb, wb, width), _F32)],
        compiler_params=_params("arbitrary"),
        name="band_attn_sample",
    )(qb, cache_k, cache_v, kn, vn, bias_c, bias_n, b_gain)


def _outproj_kernel(x_ref, oa_ref, ob_ref, w_ref, gt_ref, sh_ref, sc_ref, g_ref, x1_ref, ht_ref):
    half = oa_ref.shape[1]
    y = _dot(oa_ref[...], w_ref[0:half, :]) + _dot(ob_ref[...], w_ref[half:, :])
    x1 = x_ref[...] + gt_ref[...] * y
    x1_ref[...] = x1
    h = _rms(x1) * g_ref[...] * (1.0 + sc_ref[...]) + sh_ref[...]
    ht_ref[...] = h.T.astype(_BF)


def _out_projection(x, oa, ob, w_out_bf, gate, shift, scale, gain, *, per_row, seq):
    t_all, d = x.shape
    half = oa.shape[1]
    tm = min(PEER_TOKENS, t_all)
    mod_spec = _row_mod_specs(per_row, tm, d, max(seq // tm, 1))
    rows = lambda w: pl.BlockSpec((tm, w), lambda t: (t, 0))
    return pl.pallas_call(
        _outproj_kernel,
        grid=(t_all // tm,),
        in_specs=[rows(d), rows(half), rows(half), _resident(w_out_bf.shape, lambda t: (0, 0)),
                  mod_spec, mod_spec, mod_spec, _resident((1, d), lambda t: (0, 0))],
        out_specs=[rows(d), pl.BlockSpec((None, d, tm), lambda t: (t, 0, 0))],
        out_shape=[jax.ShapeDtypeStruct((t_all, d), _F32),
                   jax.ShapeDtypeStruct((t_all // tm, d, tm), _BF)],
        compiler_params=_params("arbitrary"),
        name="out_projection",
    )(x, oa, ob, w_out_bf, gate, shift, scale, gain)


def _ranked_top_values(s, k):
    vals = []
    rank = jnp.full(s.shape, float(k), _F32)
    for i in range(k):
        m = jnp.max(s, axis=0, keepdims=True)
        vals.append(m)
        hit = s == m
        rank = jnp.where(hit, float(i), rank)
        s = jnp.where(hit, -jnp.inf, s)
    return vals, rank


def _stack_rows(vals):
    rows = lax.broadcasted_iota(jnp.int32, (len(vals), LANES), 0)
    out = jnp.zeros((len(vals), LANES), _F32)
    for i, v in enumerate(vals):
        out = jnp.where(rows == i, v, out)
    return out


def _merge_sort_network(lo, hi):
    def merge(lo, hi, r):
        step = 2 * r
        if step < hi - lo:
            yield from merge(lo, hi, step)
            yield from merge(lo + r, hi, step)
            yield from ((i, i + r) for i in range(lo + r, hi - r, step))
        else:
            yield (lo, lo + r)

    if hi - lo >= 1:
        mid = lo + (hi - lo) // 2
        yield from _merge_sort_network(lo, mid)
        yield from _merge_sort_network(mid + 1, hi)
        yield from merge(lo, hi, 1)


def _compare_exchange(v, i, j):
    a, b = v[i], v[j]
    if b is None:
        return
    if a is None:
        v[i], v[j] = b, None
        return
    v[i], v[j] = jnp.maximum(a, b), jnp.minimum(a, b)


def _top16_replicated(tiles):
    n = PK_TOPK
    v = list(tiles) + [None] * (n - len(tiles))
    for i, j in _merge_sort_network(0, n - 1):
        _compare_exchange(v, i, j)
    shift = SUBLANES // 2
    while shift:
        w = [None if x is None else pltpu.roll(x, shift, 0) for x in v]
        c = []
        for k in range(n):
            a, b = v[k], w[n - 1 - k]
            c.append(b if a is None else a if b is None else jnp.maximum(a, b))
        stride = n // 2
        while stride:
            for i in range(n):
                if not i & stride:
                    _compare_exchange(c, i, i + stride)
            stride //= 2
        v = c
        shift //= 2
    return [jnp.full((SUBLANES, LANES), -jnp.inf, _F32) if x is None else x for x in v]


def _route_kernel(ht_ref, wq_ref, keys_ref, n1_ref, e1_ref, r2_ref, e2_ref, q_scr, s_scr):
    tt = ht_ref.shape[1]
    q_scr[...] = _dot(wq_ref[...], ht_ref[...]).astype(_BF)
    for hm in range(2 * PK_HEADS):
        s_scr[hm] = _dot(keys_ref[hm], q_scr[hm * PK_HALF:(hm + 1) * PK_HALF, :])

    def head(h, carry):
        for c in range(tt // LANES):
            cols = slice(c * LANES, (c + 1) * LANES)
            s1 = s_scr[2 * h, :, cols]
            s2 = s_scr[2 * h + 1, :, cols]
            a = _top16_replicated([s1[r:r + SUBLANES] for r in range(0, N_KEYS, SUBLANES)])
            b, rank2 = _ranked_top_values(s2, PK_TOPK)
            bmat = _stack_rows(b)
            b_lo, b_hi = bmat[0:SUBLANES], bmat[SUBLANES:]
            cand = [a[0] + b_lo, a[0] + b_hi]
            cand += [a[p] + b_lo for p in range(1, 8)]
            cand += [_stack_rows([a[p][0:1] for p in range(8, PK_TOPK)]) + b[0]]
            top = _top16_replicated(cand)
            tau = top[PK_TOPK - 1]
            z = jnp.ones_like(tau)
            for k in range(1, PK_TOPK):
                z = z + jnp.exp(top[k] - top[0])
            z_row = z[0:1]
            count1 = jnp.zeros(s1.shape, _F32)
            for p in range(PK_TOPK):
                hit = jnp.where(a[p] + b_lo >= tau, 1.0, 0.0) + jnp.where(a[p] + b_hi >= tau, 1.0, 0.0)
                reach = jnp.sum(hit, axis=0, keepdims=True)
                count1 = jnp.where(s1 == jnp.tile(a[p], (N_KEYS // SUBLANES, 1)), reach, count1)
            n1_ref[h, :, cols] = count1
            e1_ref[h, :, cols] = jnp.exp(s1 - a[0][0:1]) * (0.5 / z_row)
            r2_ref[h, :, cols] = rank2.astype(_BF)
            e2_ref[h, :, cols] = jnp.exp(s2 - b[0]).astype(_BF)
        return carry

    lax.fori_loop(0, PK_HEADS, head, 0)


def _peer_route(ht, wq_t_bf, keys_bf):
    slabs, d, slab = ht.shape
    t_all = slabs * slab
    tt = min(ROUTE_TOKENS, slab)
    per_slab = slab // tt
    nq = wq_t_bf.shape[0]
    out = lambda dt: jax.ShapeDtypeStruct((PK_HEADS, N_KEYS, t_all), dt)
    tile = pl.BlockSpec((PK_HEADS, N_KEYS, tt), lambda t: (0, 0, t))
    return pl.pallas_call(
        _route_kernel,
        grid=(t_all // tt,),
        in_specs=[pl.BlockSpec((None, d, tt), lambda t: (t // per_slab, 0, t % per_slab)),
                  _resident(wq_t_bf.shape, lambda t: (0, 0)),
                  _resident(keys_bf.shape, lambda t: (0, 0, 0))],
        out_specs=[tile] * 4,
        out_shape=[out(_F32), out(_F32), out(_BF), out(_BF)],
        scratch_shapes=[pltpu.VMEM((nq, tt), _BF), pltpu.VMEM((2 * PK_HEADS, N_KEYS, tt), _F32)],
        compiler_params=_params("arbitrary"),
        name="peer_route",
    )(ht, wq_t_bf, keys_bf)


def _twice_gelu_tanh(x):
    return x * (1.0 + jnp.tanh(x * (0.7978845608028654 + (0.7978845608028654 * 0.044715) * (x * x))))


def _row_to_tile(row, rows):
    packed = jnp.broadcast_to(row, (16, LANES)).astype(_BF)
    return jnp.tile(packed, (rows // 16, 1))


def _peer_kernel(ht_ref, u_ref, vt_ref, n1_ref, e1_ref, r2_ref, e2_ref, x_ref, gt_ref, gf_ref,
                 o_ref, at_scr, wt_scr, acc, r2_scr, e2_scr, *, final_norm):
    e = pl.program_id(1)
    tt = ht_ref.shape[1]
    groups = u_ref.shape[0] // N_KEYS
    span = at_scr.shape[2]
    n_chunks = tt // span

    @pl.when(e == 0)
    def _():
        acc[...] = jnp.zeros(acc.shape, _F32)
        r2_scr[...] = r2_ref[...]
        e2_scr[...] = e2_ref[...]

    for c in range(n_chunks):
        at_scr[c] = _dot(u_ref[...], ht_ref[:, c * span:(c + 1) * span])
    zero = jnp.zeros((N_KEYS, LANES), _BF)

    for c in range(n_chunks):
        for sub in range(span // LANES):
            cols = slice(c * span + sub * LANES, c * span + (sub + 1) * LANES)
            sub_cols = slice(sub * LANES, (sub + 1) * LANES)
            for il in range(groups):
                rows = slice(il * N_KEYS, (il + 1) * N_KEYS)
                g = None
                for h in range(PK_HEADS):
                    count = _row_to_tile(n1_ref[h, il:il + 1, cols], N_KEYS)
                    sel = jnp.where(r2_scr[h, :, cols] < count, e2_scr[h, :, cols], zero)
                    term = sel * _row_to_tile(e1_ref[h, il:il + 1, cols], N_KEYS)
                    g = term if g is None else g + term
                act = _twice_gelu_tanh(at_scr[c, rows, sub_cols]).astype(_BF)
                wt_scr[c, rows, sub_cols] = g * act
        acc[c] += _dot(vt_ref[...], wt_scr[c])

    @pl.when(e == pl.num_programs(1) - 1)
    def _():
        f = jnp.concatenate([acc[c].T for c in range(n_chunks)], axis=0)
        y = x_ref[...] + gt_ref[...] * f
        if final_norm:
            y = _rms(y) * gf_ref[...]
        o_ref[...] = y


def _peer_experts(ht, u_bf, vt_bf, route, x1, gate, g_final, *, layer, per_row, seq, final_norm):
    tiles, d, tt = ht.shape
    _, steps, _, eb = vt_bf.shape
    t_all = tiles * tt
    span = min(PEER_CHUNK, tt)
    groups = eb // N_KEYS
    assert groups == SUBLANES
    n1, e1, r2, e2 = route
    by_step = lambda a: a.reshape(PK_HEADS, N_KEYS // groups, groups, t_all)
    gate_spec = (pl.BlockSpec((tt, d), lambda t, e: (t, 0)) if per_row else
                 pl.BlockSpec((None, 1, d), lambda t, e: (t // max(seq // tt, 1), 0, 0)))
    step_rows = pl.BlockSpec((PK_HEADS, None, groups, tt), lambda t, e: (0, e, 0, t))
    tile = pl.BlockSpec((PK_HEADS, N_KEYS, tt), lambda t, e: (0, 0, t))
    return pl.pallas_call(
        functools.partial(_peer_kernel, final_norm=final_norm),
        grid=(tiles, steps),
        in_specs=[pl.BlockSpec((None, d, tt), lambda t, e: (t, 0, 0)),
                  pl.BlockSpec((None, eb, d), lambda t, e: (layer, e, 0)),
                  pl.BlockSpec((None, None, d, eb), lambda t, e: (layer, e, 0, 0)),
                  step_rows, step_rows, tile, tile,
                  pl.BlockSpec((tt, d), lambda t, e: (t, 0)),
                  gate_spec,
                  _resident((1, d), lambda t, e: (0, 0))],
        out_specs=pl.BlockSpec((tt, d), lambda t, e: (t, 0)),
        out_shape=jax.ShapeDtypeStruct((t_all, d), _F32),
        scratch_shapes=[pltpu.VMEM((tt // span, eb, span), _F32), pltpu.VMEM((tt // span, eb, span), _BF),
                        pltpu.VMEM((tt // span, d, span), _F32),
                        pltpu.VMEM((PK_HEADS, N_KEYS, tt), _BF), pltpu.VMEM((PK_HEADS, N_KEYS, tt), _BF)],
        compiler_params=_params("arbitrary", "arbitrary"),
        name="peer_experts",
    )(ht, u_bf, vt_bf, by_step(n1), by_step(e1), r2, e2, x1, gate, g_final)


def _rope_tables(pos):
    half = ROT_DIM // 2
    inv = ROPE_THETA ** (-jnp.arange(0, ROT_DIM, 2, dtype=_F32) / ROT_DIM)
    ang = pos.astype(_F32)[:, None] * inv[None, :]
    cos, sin = jnp.cos(ang), jnp.sin(ang)
    n = pos.shape[0]
    pad = jnp.zeros((n, A_DIM - ROT_DIM), _F32)
    zeros = jnp.zeros((n, half), _F32)
    c = jnp.concatenate([cos, cos, pad + 1.0], axis=1)
    sa = jnp.concatenate([-sin, zeros, pad], axis=1)
    sb = jnp.concatenate([zeros, sin, pad], axis=1)
    return tuple(jnp.tile(t, (1, LANES // A_DIM)) for t in (c, sa, sb))


def _stream_layer(x, mod, lw, tables, rope_tabs, mixer, *, per_row, seq, final_norm, g_final, stack=None):
    shift1, scale1, gate1, shift2, scale2, gate2 = mod
    proj = _in_projection(x, shift1, scale1, lw["g_attn"], lw["w_in"], rope_tabs, per_row=per_row, seq=seq,
                          stack=stack)
    qa, kaf, vaf, kab, vab, qb, kbf, vbf, kbb, vbb = proj
    oa, ob, extra = mixer(qa, kaf, vaf, kab, vab, qb, kbf, vbf, kbb, vbb)
    x1, ht = _out_projection(x, oa, ob, lw["w_out"], gate1, shift2, scale2, lw["g_ffn"],
                             per_row=per_row, seq=seq)
    route = _peer_route(ht, lw["pk_wq_t"], lw["pk_keys"])
    x2 = _peer_experts(ht, *tables, route, x1, gate2, g_final, layer=lw["layer"],
                       per_row=per_row, seq=seq, final_norm=final_norm)
    return x2, (kaf, vaf, kbf, vbf) + extra


def kernel(x_prompt, x_sample, c_prompt, c_sample, cache_a_k, cache_a_v, cache_b_k, cache_b_v, w_ada, b_ada, g_attn, g_ffn, w_in, lam_q1, lam_k1, lam_q2, lam_k2, a_gain, rel_bias, b_gain, w_out, pk_wq, pk_keys, pk_u, pk_v, g_final):
    batch, seq, d = x_prompt.shape
    nb, t_new, _ = x_sample.shape
    depth = w_ada.shape[0]
    past = cache_a_k.shape[2]
    wb = cache_b_k.shape[2]
    a_width = A_HEADS * 2 * A_DIM
    b_width = B_HEADS * B_DIM

    c_all = jnp.concatenate([c_prompt, c_sample], axis=0)
    pad_rows = (-c_all.shape[0]) % 16
    c_all = jnp.pad(c_all, ((0, pad_rows), (0, 0)))
    mod = _modulation(c_all, w_ada, b_ada)

    tabs_p = _rope_tables(jnp.arange(seq, dtype=jnp.int32))
    pos_s = past + jnp.arange(t_new, dtype=jnp.int32)
    tabs_s = tuple(jnp.tile(t, (nb, 1)) for t in _rope_tables(pos_s))

    cak = cache_a_k.reshape(depth, nb, past * A_HEADS, 2 * A_DIM)
    cav = cache_a_v.reshape(depth, nb, past * A_HEADS, 2 * A_DIM)
    cbk = cache_b_k.reshape(depth, nb, wb, b_width)
    cbv = cache_b_v.reshape(depth, nb, wb, b_width)

    u_bf = pk_u.astype(_BF)
    vt_bf = pk_v.astype(_BF).reshape(depth, -1, PEER_EXPERTS, d).transpose(0, 1, 3, 2)

    xp = x_prompt.reshape(batch * seq, d)
    xs = x_sample.reshape(nb * t_new, d)
    g_fin = g_final.reshape(1, d)
    st_p, st_s = [], []
    for l in range(depth):
        lam_init = 0.8 - 0.6 * math.exp(-0.3 * l)
        lamv = jnp.stack([lam_q1[l], lam_k1[l], lam_q2[l], lam_k2[l]])
        lw = {
            "g_attn": g_attn[l].reshape(1, d),
            "g_ffn": g_ffn[l].reshape(1, d),
            "w_in": w_in[l].astype(_BF),
            "w_out": w_out[l].astype(_BF),
            "pk_wq_t": pk_wq[l].T.astype(_BF),
            "pk_keys": pk_keys[l].reshape(2 * PK_HEADS, N_KEYS, PK_HALF).astype(_BF),
            "layer": l,
        }
        gain_a = a_gain[l].reshape(1, 2 * A_DIM)
        gain_b = b_gain[l].reshape(1, b_width)
        mod_p = tuple(m.reshape(batch, 1, d) for m in jnp.split(mod[l, :batch], 6, axis=-1))
        mod_s = tuple(jnp.repeat(m, t_new, axis=0) for m in jnp.split(mod[l, batch:batch + nb], 6, axis=-1))
        last = l == depth - 1

        def mix_p(qa, kaf, vaf, kab, vab, qb, kbf, vbf, kbb, vbb):
            oa = _attn_a_prompt(qa, kab, vab, lamv, gain_a, batch=batch, seq=seq, lam_init=lam_init)
            ob = _attn_b_prompt(qb, kbb, vbb, rel_bias[l], gain_b, batch=batch, seq=seq)
            return oa, ob, ()

        def mix_s(qa, kaf, vaf, kab, vab, qb, kbf, vbf, kbb, vbb):
            oa = _attn_a_sample(qa, cak, cav, kaf, vaf, lamv, gain_a, layer=l, lam_init=lam_init, t_new=t_new)
            ob, nbk, nbv = _attn_b_sample(qb, cbk, cbv, kbf, vbf, rel_bias[l], gain_b, layer=l, t_new=t_new)
            return oa, ob, (nbk, nbv)

        previous = st_p[-1][0:2] if st_p else None
        xp, sp = _stream_layer(xp, mod_p, lw, (u_bf, vt_bf), tabs_p, mix_p, per_row=False, seq=seq,
                               final_norm=last, g_final=g_fin, stack=(l, depth, previous))
        xs, ss = _stream_layer(xs, mod_s, lw, (u_bf, vt_bf), tabs_s, mix_s, per_row=True, seq=t_new,
                               final_norm=last, g_final=g_fin)
        st_p.append(sp)
        st_s.append(ss)

    keep = min(B_REACH, seq)
    a_shape_p = (batch, seq, A_HEADS, 2 * A_DIM)
    b_tail = lambda s: s.reshape(batch, keep, B_HEADS, B_DIM)
    a_shape_s = (nb, t_new, A_HEADS, 2 * A_DIM)
    b_shape_s = (nb, wb, B_HEADS, B_DIM)
    return (
        xp.reshape(batch, seq, d),
        xs.reshape(nb, t_new, d),
        st_p[-1][0].reshape((depth,) + a_shape_p),
        st_p[-1][1].reshape((depth,) + a_shape_p),
        jnp.stack([b_tail(s[2]) for s in st_p]),
        jnp.stack([b_tail(s[3]) for s in st_p]),
        jnp.stack([s[0].reshape(a_shape_s) for s in st_s]),
        jnp.stack([s[1].reshape(a_shape_s) for s in st_s]),
        jnp.stack([s[4].reshape(b_shape_s) for s in st_s]),
        jnp.stack([s[5].reshape(b_shape_s) for s in st_s]),
    )
```

```python
import functools
import math

import jax
import jax.numpy as jnp
import numpy as np
from jax import lax
from jax.experimental import pallas as pl
from jax.experimental.pallas import tpu as pltpu

_F32 = jnp.float32
_BF = jnp.bfloat16

CHUNK = 64
A_HEADS = 4
A_DIM = 64
ROT_DIM = A_DIM // 4
ROPE_THETA = 500000.0
B_HEADS = 8
B_DIM = 64
LEFT_CHUNKS = 8
B_REACH = LEFT_CHUNKS * CHUNK
REL_CLIP = 128
PK_HEADS = 8
N_KEYS = 128
PK_TOPK = 16
PK_HALF = 128
EPS = 1e-6

LANES = 128
SUBLANES = 8
MASKED = -1e30
VMEM_LIMIT = 56 * 1024 * 1024

A_TILE = 512
B_TILE = 256
PEER_TOKENS = 1024
ROUTE_TOKENS = 512
PEER_EXPERTS = 1024
PEER_CHUNK = 256
ROW_TILE = 512


def _params(*sem):
    return pltpu.CompilerParams(dimension_semantics=sem, vmem_limit_bytes=VMEM_LIMIT)


def _resident(shape, index_map):
    return pl.BlockSpec(shape, index_map, pipeline_mode=pl.Buffered(1))


def _rms(x):
    return x * lax.rsqrt(jnp.mean(x * x, axis=-1, keepdims=True) + EPS)


def _dot(a, b):
    return jnp.dot(a, b, preferred_element_type=_F32)


def _dot_nt(a, b):
    return lax.dot_general(a, b, (((1,), (1,)), ((), ())), preferred_element_type=_F32)


def _mod_kernel(c_ref, w_ref, b_ref, o_ref):
    c = c_ref[...]
    s = c * (1.0 / (1.0 + jnp.exp(-c)))
    o_ref[...] = _dot(s.astype(_BF), w_ref[...].astype(_BF)) + b_ref[...]


def _modulation(c_all, w_ada, b_ada):
    depth, d, n = w_ada.shape
    rows = c_all.shape[0]
    tn = n // 4
    return pl.pallas_call(
        _mod_kernel,
        grid=(depth, n // tn),
        in_specs=[
            pl.BlockSpec((rows, d), lambda l, j: (0, 0)),
            pl.BlockSpec((None, d, tn), lambda l, j: (l, 0, j)),
            pl.BlockSpec((None, 1, tn), lambda l, j: (l, 0, j)),
        ],
        out_specs=pl.BlockSpec((None, rows, tn), lambda l, j: (l, 0, j)),
        out_shape=jax.ShapeDtypeStruct((depth, rows, n), _F32),
        compiler_params=_params("arbitrary", "arbitrary"),
        name="adaln_mod",
    )(c_all, w_ada, b_ada.reshape(depth, 1, n))


def _inproj_kernel(x_ref, sh_ref, sc_ref, g_ref, w_ref, cos_ref, sa_ref, sb_ref, *rest, width, carried):
    (qa_ref, kaf_ref, vaf_ref, kab_ref, vab_ref,
     qb_ref, kbf_ref, vbf_ref, kbb_ref, vbb_ref) = rest[2 * carried:]
    h = _rms(x_ref[...]) * g_ref[...] * (1.0 + sc_ref[...]) + sh_ref[...]
    hb = h.astype(_BF)
    reps = width // LANES
    cos = jnp.tile(cos_ref[...], (1, reps))
    sa = jnp.tile(sa_ref[...], (1, reps))
    sb = jnp.tile(sb_ref[...], (1, reps))

    def rope(z):
        return z * cos + pltpu.roll(z, width - ROT_DIM // 2, 1) * sa + pltpu.roll(z, ROT_DIM // 2, 1) * sb

    def proj(i):
        return _dot(hb, w_ref[:, i * width:(i + 1) * width])

    def store_state(ref, z):
        hd = 2 * A_DIM
        for head in range(A_HEADS):
            ref[pl.ds(head, z.shape[0], stride=A_HEADS), :] = z[:, head * hd:(head + 1) * hd]

    qa_ref[...] = (rope(proj(0)) * (A_DIM ** -0.5)).astype(_BF)
    ka = rope(proj(1))
    store_state(kaf_ref, ka)
    kab_ref[...] = ka.astype(_BF)
    va = proj(2)
    store_state(vaf_ref, va)
    vab_ref[...] = va.astype(_BF)
    qb_ref[...] = (proj(3) * (B_DIM ** -0.5)).astype(_BF)
    kb = proj(4)
    kbf_ref[...] = kb
    kbb_ref[...] = kb.astype(_BF)
    vb = proj(5)
    vbf_ref[...] = vb
    vbb_ref[...] = vb.astype(_BF)


def _row_mod_specs(per_row, tm, d, tiles_per_batch):
    if per_row:
        return pl.BlockSpec((tm, d), lambda t: (t, 0))
    return pl.BlockSpec((None, 1, d), lambda t: (t // tiles_per_batch, 0, 0))


def _in_projection(x, shift, scale, gain, w_in_bf, rope_tabs, *, per_row, seq, stack=None):
    t_all, d = x.shape
    width = w_in_bf.shape[1] // 6
    tm = min(ROW_TILE, t_all)
    tiles_per_batch = max(seq // tm, 1)
    tab_tiles = rope_tabs[0].shape[0] // tm
    mod_spec = _row_mod_specs(per_row, tm, d, tiles_per_batch)
    tab_spec = pl.BlockSpec((tm, LANES), lambda t: (t % tab_tiles, 0))
    row = lambda dt: jax.ShapeDtypeStruct((t_all, width), dt)
    out_spec = pl.BlockSpec((tm, width), lambda t: (t, 0))
    dts = [_BF, _F32, _F32, _BF, _BF, _BF, _F32, _F32, _BF, _BF]
    out_shapes = [row(dt) for dt in dts]
    out_specs = [out_spec] * 10
    state_rows, hd = t_all * A_HEADS, 2 * A_DIM
    in_specs = [
        pl.BlockSpec((tm, d), lambda t: (t, 0)),
        mod_spec, mod_spec,
        _resident((1, d), lambda t: (0, 0)),
        _resident(w_in_bf.shape, lambda t: (0, 0)),
        tab_spec, tab_spec, tab_spec,
    ]
    args = [x, shift, scale, gain, w_in_bf, *rope_tabs]
    aliases = {}
    carried = False
    for i in (1, 2):
        if stack is None:
            out_shapes[i] = jax.ShapeDtypeStruct((state_rows, hd), _F32)
            out_specs[i] = pl.BlockSpec((tm * A_HEADS, hd), lambda t: (t, 0))
        else:
            layer, depth, previous = stack
            out_shapes[i] = jax.ShapeDtypeStruct((depth, state_rows, hd), _F32)
            out_specs[i] = pl.BlockSpec((None, tm * A_HEADS, hd), lambda t: (layer, t, 0))
            if previous is not None:
                carried = True
                aliases[len(args)] = i
                in_specs.append(pl.BlockSpec(memory_space=pl.ANY))
                args.append(previous[i - 1])
    return pl.pallas_call(
        functools.partial(_inproj_kernel, width=width, carried=carried),
        grid=(t_all // tm,),
        in_specs=in_specs,
        out_specs=out_specs,
        out_shape=out_shapes,
        input_output_aliases=aliases,
        compiler_params=_params("arbitrary"),
        name="in_projection",
    )(*args)


def _diff_lambda(lamv_ref, lam_init):
    v = lamv_ref[...]
    d1 = jnp.sum(v[0:1] * v[1:2], axis=1, keepdims=True)
    d2 = jnp.sum(v[2:3] * v[3:4], axis=1, keepdims=True)
    return jnp.exp(d1) - jnp.exp(d2) + lam_init


def _split_maps(q):
    lane = lax.broadcasted_iota(jnp.int32, q.shape, 1)
    zero = jnp.zeros_like(q)
    return jnp.where(lane < A_DIM, q, zero), jnp.where(lane >= A_DIM, q, zero)


def _attn_a_kernel(lamv_ref, q_ref, k_ref, v_ref, gain_ref, o_ref,
                   sa, sb, m1, l1, acc1, m2, l2, acc2, *, lam_init):
    n_full = pl.program_id(2)
    tq = q_ref.shape[0]
    q1, q2 = _split_maps(q_ref[...])
    for m_scr, l_scr, acc in ((m1, l1, acc1), (m2, l2, acc2)):
        m_scr[...] = jnp.full(m_scr.shape, MASKED, _F32)
        l_scr[...] = jnp.zeros(l_scr.shape, _F32)
        acc[...] = jnp.zeros(acc.shape, _F32)

    def rows(j):
        return pl.ds(pl.multiple_of(j * tq, tq), tq)

    def scores(j, s_scr):
        kb = k_ref[rows(j), :]
        s_scr[0] = _dot_nt(q1, kb)
        s_scr[1] = _dot_nt(q2, kb)

    def update(s, m_scr, l_scr, acc, vb):
        m_prev = m_scr[...]
        m_next = jnp.maximum(m_prev, jnp.max(s, axis=1, keepdims=True))
        p = jnp.exp(s - jnp.tile(m_next, (1, tq // LANES)))
        alpha = jnp.exp(m_prev - m_next)
        part = p[:, 0:LANES]
        for c in range(1, tq // LANES):
            part = part + p[:, c * LANES:(c + 1) * LANES]
        l_scr[...] = alpha * l_scr[...] + part
        acc[...] = acc[...] * alpha + _dot(p.astype(_BF), vb)
        m_scr[...] = m_next

    def process(j, s_scr, diagonal):
        vb = v_ref[rows(j), :]
        for mp, (m_scr, l_scr, acc) in enumerate(((m1, l1, acc1), (m2, l2, acc2))):
            s = s_scr[mp]
            if diagonal:
                qc = lax.broadcasted_iota(jnp.int32, s.shape, 0) // CHUNK
                kc = lax.broadcasted_iota(jnp.int32, s.shape, 1) // CHUNK
                s = jnp.where(kc <= qc, s, MASKED)
            update(s, m_scr, l_scr, acc, vb)

    scores(0, sa)

    def pair(i, carry):
        scores(2 * i + 1, sb)
        process(2 * i, sa, False)
        scores(2 * i + 2, sa)
        process(2 * i + 1, sb, False)
        return carry

    lax.fori_loop(0, n_full // 2, pair, 0)

    @pl.when(n_full % 2 == 1)
    def _():
        scores(n_full, sb)
        process(n_full - 1, sa, False)
        process(n_full, sb, True)

    @pl.when(n_full % 2 == 0)
    def _():
        process(n_full, sa, True)

    lam = _diff_lambda(lamv_ref, lam_init)
    d1 = jnp.sum(l1[...], axis=1, keepdims=True)
    d2 = jnp.sum(l2[...], axis=1, keepdims=True)
    o = acc1[...] / d1 - lam * (acc2[...] / d2)
    o_ref[...] = (_rms(o) * gain_ref[...] * (1.0 - lam_init)).astype(_BF)


def _attn_a_prompt(qa, ka, va, lamv, a_gain, *, batch, seq, lam_init):
    t_all, width = qa.shape
    hd = 2 * A_DIM
    tq = A_TILE
    nq = seq // tq
    return pl.pallas_call(
        functools.partial(_attn_a_kernel, lam_init=lam_init),
        grid=(batch, A_HEADS, nq),
        in_specs=[
            _resident(lamv.shape, lambda b, h, i: (0, 0)),
            pl.BlockSpec((tq, hd), lambda b, h, i: (b * nq + i, h)),
            pl.BlockSpec((seq, hd), lambda b, h, i: (b, h)),
            pl.BlockSpec((seq, hd), lambda b, h, i: (b, h)),
            _resident((1, hd), lambda b, h, i: (0, 0)),
        ],
        out_specs=pl.BlockSpec((tq, hd), lambda b, h, i: (b * nq + i, h)),
        out_shape=jax.ShapeDtypeStruct((t_all, width), _BF),
        scratch_shapes=[pltpu.VMEM((2, tq, tq), _F32)] * 2 + [pltpu.VMEM((tq, hd), _F32)] * 6,
        compiler_params=_params("arbitrary", "arbitrary", "arbitrary"),
        name="diff_attn_prompt",
    )(lamv, qa, ka, va, a_gain)


def _head_lanes(shape, head, dim):
    lane = lax.broadcasted_iota(jnp.int32, shape, 1)
    return (lane >= head * dim) & (lane < (head + 1) * dim)


def _attn_b_kernel(q_ref, k0_ref, k1_ref, k2_ref, v0_ref, v1_ref, v2_ref, bias_ref, gain_ref, o_ref):
    qi = pl.program_id(1)
    tq = q_ref.shape[0]
    k_refs = (k0_ref, k1_ref, k2_ref)
    v_refs = (v0_ref, v1_ref, v2_ref)
    pen = (jnp.where(qi >= 2, 0.0, MASKED), jnp.where(qi >= 1, 0.0, MASKED), 0.0)
    outs = []
    for pair in range(B_HEADS // 2):
        cols = slice(pair * LANES, (pair + 1) * LANES)
        qp = q_ref[:, cols]
        zero = jnp.zeros_like(qp)
        halves = []
        for sub in range(2):
            head = 2 * pair + sub
            qh = jnp.where(_head_lanes(qp.shape, sub, B_DIM), qp, zero)
            s = [_dot_nt(qh, k_refs[b][:, cols]) + bias_ref[head, :, b * tq:(b + 1) * tq] + pen[b]
                 for b in range(3)]
            m = jnp.maximum(jnp.maximum(jnp.max(s[0], axis=1, keepdims=True),
                                        jnp.max(s[1], axis=1, keepdims=True)),
                            jnp.max(s[2], axis=1, keepdims=True))
            p = [jnp.exp(sb - m) for sb in s]
            l = (jnp.sum(p[0], axis=1, keepdims=True) + jnp.sum(p[1], axis=1, keepdims=True)
                 + jnp.sum(p[2], axis=1, keepdims=True))
            o = (_dot(p[0].astype(_BF), v_refs[0][:, cols]) + _dot(p[1].astype(_BF), v_refs[1][:, cols])
                 + _dot(p[2].astype(_BF), v_refs[2][:, cols]))
            halves.append(o / l)
        outs.append(jnp.where(_head_lanes(halves[0].shape, 0, B_DIM), halves[0], halves[1]))
    o = jnp.concatenate(outs, axis=1)
    o_ref[...] = (_rms(o) * gain_ref[...]).astype(_BF)


def _band_bias(table, tq):
    nk = B_REACH + tq
    qpos = np.arange(tq)[:, None]
    kpos = np.arange(nk)[None, :] - B_REACH
    qc = qpos // CHUNK
    kc = np.floor_divide(kpos, CHUNK)
    valid = (kc <= qc) & (kc >= qc - LEFT_CHUNKS)
    span = nk + tq - 1
    dist = (span - 1 - np.arange(span)) - (tq - 1)
    line = table[:, np.clip(dist, -REL_CLIP, REL_CLIP) + REL_CLIP].astype(_F32)
    line = jnp.pad(line, ((0, 0), (0, 1)))
    skew = jnp.tile(line, (1, tq))[:, :tq * span].reshape(-1, tq, span)
    bias = skew[:, :, tq - 1:]
    return jnp.where(valid[None], bias, MASKED)


def _attn_b_prompt(qb, kb, vb, rel_table, b_gain, *, batch, seq):
    t_all, width = qb.shape
    tq = B_TILE
    assert B_REACH == 2 * tq
    nq = seq // tq
    bias = _band_bias(rel_table, tq)
    qmap = lambda b, i: (b * nq + i, 0)
    kmap = lambda back: (lambda b, i: (b * nq + jnp.maximum(i - back, 0), 0))
    blk = lambda f: pl.BlockSpec((tq, width), f)
    return pl.pallas_call(
        _attn_b_kernel,
        grid=(batch, nq),
        in_specs=[blk(qmap), blk(kmap(2)), blk(kmap(1)), blk(kmap(0)),
                  blk(kmap(2)), blk(kmap(1)), blk(kmap(0)),
                  _resident(bias.shape, lambda b, i: (0, 0, 0)),
                  _resident((1, width), lambda b, i: (0, 0))],
        out_specs=blk(qmap),
        out_shape=jax.ShapeDtypeStruct((t_all, width), _BF),
        compiler_params=_params("arbitrary", "arbitrary"),
        name="band_attn_prompt",
    )(qb, kb, kb, kb, vb, vb, vb, bias, b_gain)


def _attn_a_sample_kernel(lamv_ref, q_ref, kc_ref, vc_ref, kn_ref, vn_ref, gain_ref, o_ref, *, lam_init):
    q = q_ref[...]
    t = q.shape[0]
    hd = 2 * A_DIM
    past = kc_ref.shape[0] // A_HEADS
    lam = _diff_lambda(lamv_ref, lam_init)
    outs = []
    for h in range(A_HEADS):
        cols = slice(h * hd, (h + 1) * hd)
        head_rows = pl.ds(h, past, stride=A_HEADS)
        new_rows = pl.ds(h, t, stride=A_HEADS)
        q12 = jnp.concatenate(_split_maps(q[:, cols]), axis=0)
        kn = kn_ref[new_rows, :].astype(_BF)
        vn = vn_ref[new_rows, :].astype(_BF)
        s_c = _dot_nt(q12, kc_ref[head_rows, :].astype(_BF))
        s_n = _dot_nt(q12, kn)
        m = jnp.maximum(jnp.max(s_c, axis=1, keepdims=True), jnp.max(s_n, axis=1, keepdims=True))
        p_c = jnp.exp(s_c - m)
        p_n = jnp.exp(s_n - m)
        l = jnp.sum(p_c, axis=1, keepdims=True) + jnp.sum(p_n, axis=1, keepdims=True)
        p_c = p_c / l
        p_n = p_n / l
        a_c = p_c[0:t] - lam * p_c[t:2 * t]
        a_n = p_n[0:t] - lam * p_n[t:2 * t]
        o = _dot(a_c.astype(_BF), vc_ref[head_rows, :].astype(_BF)) + _dot(a_n.astype(_BF), vn)
        outs.append(_rms(o) * gain_ref[...] * (1.0 - lam_init))
    o_ref[...] = jnp.concatenate(outs, axis=1).astype(_BF)


def _attn_a_sample(qa, cache_k, cache_v, kn, vn, lamv, a_gain, *, layer, lam_init, t_new):
    t_all, width = qa.shape
    _, nb, cache_rows, hd = cache_k.shape
    tok = pl.BlockSpec((t_new, width), lambda b: (b, 0))
    new = pl.BlockSpec((t_new * A_HEADS, hd), lambda b: (b, 0))
    cache = pl.BlockSpec((None, None, cache_rows, hd), lambda b: (layer, b, 0, 0))
    return pl.pallas_call(
        functools.partial(_attn_a_sample_kernel, lam_init=lam_init),
        grid=(nb,),
        in_specs=[_resident(lamv.shape, lambda b: (0, 0)), tok, cache, cache, new, new,
                  _resident((1, 2 * A_DIM), lambda b: (0, 0))],
        out_specs=tok,
        out_shape=jax.ShapeDtypeStruct((t_all, width), _BF),
        compiler_params=_params("arbitrary"),
        name="diff_attn_sample",
    )(lamv, qa, cache_k, cache_v, kn, vn, a_gain)


def _attn_b_sample_kernel(q_ref, kc_ref, vc_ref, kn_ref, vn_ref, bc_ref, bn_ref, gain_ref,
                          o_ref, ko_ref, vo_ref):
    q = q_ref[...]
    t = q.shape[0]
    keep = kc_ref.shape[0] - t
    zero = jnp.zeros_like(q)
    qall = jnp.concatenate(
        [jnp.where(_head_lanes(q.shape, h, B_DIM), q, zero) for h in range(B_HEADS)], axis=0)
    kc = kc_ref[...]
    vc = vc_ref[...]
    kn = kn_ref[...]
    vn = vn_ref[...]
    s_c = _dot_nt(qall, kc.astype(_BF)) + bc_ref[...]
    s_n = _dot_nt(qall, kn.astype(_BF)) + bn_ref[...]
    m = jnp.maximum(jnp.max(s_c, axis=1, keepdims=True), jnp.max(s_n, axis=1, keepdims=True))
    p_c = jnp.exp(s_c - m)
    p_n = jnp.exp(s_n - m)
    l = jnp.sum(p_c, axis=1, keepdims=True) + jnp.sum(p_n, axis=1, keepdims=True)
    o_all = (_dot(p_c.astype(_BF), vc.astype(_BF)) + _dot(p_n.astype(_BF), vn.astype(_BF))) / l
    o = jnp.zeros((t, q.shape[1]), _F32)
    for h in range(B_HEADS):
        o = jnp.where(_head_lanes(o.shape, h, B_DIM), o_all[h * t:(h + 1) * t], o)
    o_ref[...] = (_rms(o) * gain_ref[...]).astype(_BF)
    ko_ref[0:keep, :] = kc[t:, :]
    ko_ref[keep:, :] = kn
    vo_ref[0:keep, :] = vc[t:, :]
    vo_ref[keep:, :] = vn


def _sample_bias(table, t_new, wb):
    qpos = jnp.arange(t_new)[:, None] + wb
    kpos = jnp.arange(wb + t_new)[None, :]
    d = jnp.clip(qpos - kpos, -REL_CLIP, REL_CLIP) + REL_CLIP
    bias = table[:, d].astype(_F32).reshape(B_HEADS * t_new, wb + t_new)
    return bias[:, :wb], bias[:, wb:]


def _attn_b_sample(qb, cache_k, cache_v, kn, vn, rel_table, b_gain, *, layer, t_new):
    t_all, width = qb.shape
    _, nb, wb, _ = cache_k.shape
    bias_c, bias_n = _sample_bias(rel_table, t_new, wb)
    tok = pl.BlockSpec((t_new, width), lambda b: (b, 0))
    cache = pl.BlockSpec((None, None, wb, width), lambda b: (layer, b, 0, 0))
    state = pl.BlockSpec((None, wb, width), lambda b: (b, 0, 0))
    return pl.pallas_call(
        _attn_b_sample_kernel,
        grid=(nb,),
        in_specs=[tok, cache, cache, tok, tok,
                  _resident(bias_c.shape, lambda b: (0, 0)), _resident(bias_n.shape, lambda b: (0, 0)),
                  _resident((1, width), lambda b: (0, 0))],
        out_specs=[tok, state, state],
        out_shape=[jax.ShapeDtypeStruct((t_all, width), _BF),
                   jax.ShapeDtypeStruct((nb, wb, width), _F32),
                   jax.ShapeDtypeStruct((nb, wb, width), _F32)],
        compiler_params=_params("arbitrary"),
        name="band_attn_sample",
    )(qb, cache_k, cache_v, kn, vn, bias_c, bias_n, b_gain)


def _outproj_kernel(x_ref, oa_ref, ob_ref, w_ref, gt_ref, sh_ref, sc_ref, g_ref, x1_ref, ht_ref):
    half = oa_ref.shape[1]
    y = _dot(oa_ref[...], w_ref[0:half, :]) + _dot(ob_ref[...], w_ref[half:, :])
    x1 = x_ref[...] + gt_ref[...] * y
    x1_ref[...] = x1
    h = _rms(x1) * g_ref[...] * (1.0 + sc_ref[...]) + sh_ref[...]
    ht_ref[...] = h.T.astype(_BF)


def _out_projection(x, oa, ob, w_out_bf, gate, shift, scale, gain, *, per_row, seq):
    t_all, d = x.shape
    half = oa.shape[1]
    tm = min(PEER_TOKENS, t_all)
    mod_spec = _row_mod_specs(per_row, tm, d, max(seq // tm, 1))
    rows = lambda w: pl.BlockSpec((tm, w), lambda t: (t, 0))
    return pl.pallas_call(
        _outproj_kernel,
        grid=(t_all // tm,),
        in_specs=[rows(d), rows(half), rows(half), _resident(w_out_bf.shape, lambda t: (0, 0)),
                  mod_spec, mod_spec, mod_spec, _resident((1, d), lambda t: (0, 0))],
        out_specs=[rows(d), pl.BlockSpec((None, d, tm), lambda t: (t, 0, 0))],
        out_shape=[jax.ShapeDtypeStruct((t_all, d), _F32),
                   jax.ShapeDtypeStruct((t_all // tm, d, tm), _BF)],
        compiler_params=_params("arbitrary"),
        name="out_projection",
    )(x, oa, ob, w_out_bf, gate, shift, scale, gain)


def _ranked_top_values(s, k):
    vals = []
    rank = jnp.full(s.shape, float(k), _F32)
    for i in range(k):
        m = jnp.max(s, axis=0, keepdims=True)
        vals.append(m)
        hit = s == m
        rank = jnp.where(hit, float(i), rank)
        s = jnp.where(hit, -jnp.inf, s)
    return vals, rank


def _stack_rows(vals):
    rows = lax.broadcasted_iota(jnp.int32, (len(vals), LANES), 0)
    out = jnp.zeros((len(vals), LANES), _F32)
    for i, v in enumerate(vals):
        out = jnp.where(rows == i, v, out)
    return out


def _merge_sort_network(lo, hi):
    def merge(lo, hi, r):
        step = 2 * r
        if step < hi - lo:
            yield from merge(lo, hi, step)
            yield from merge(lo + r, hi, step)
            yield from ((i, i + r) for i in range(lo + r, hi - r, step))
        else:
            yield (lo, lo + r)

    if hi - lo >= 1:
        mid = lo + (hi - lo) // 2
        yield from _merge_sort_network(lo, mid)
        yield from _merge_sort_network(mid + 1, hi)
        yield from merge(lo, hi, 1)


def _compare_exchange(v, i, j):
    a, b = v[i], v[j]
    if b is None:
        return
    if a is None:
        v[i], v[j] = b, None
        return
    v[i], v[j] = jnp.maximum(a, b), jnp.minimum(a, b)


def _top16_replicated(tiles):
    n = PK_TOPK
    v = list(tiles) + [None] * (n - len(tiles))
    for i, j in _merge_sort_network(0, n - 1):
        _compare_exchange(v, i, j)
    shift = SUBLANES // 2
    while shift:
        w = [None if x is None else pltpu.roll(x, shift, 0) for x in v]
        c = []
        for k in range(n):
            a, b = v[k], w[n - 1 - k]
            c.append(b if a is None else a if b is None else jnp.maximum(a, b))
        stride = n // 2
        while stride:
            for i in range(n):
                if not i & stride:
                    _compare_exchange(c, i, i + stride)
            stride //= 2
        v = c
        shift //= 2
    return [jnp.full((SUBLANES, LANES), -jnp.inf, _F32) if x is None else x for x in v]


def _route_kernel(ht_ref, wq_ref, keys_ref, n1_ref, e1_ref, r2_ref, e2_ref, q_scr, s_scr):
    tt = ht_ref.shape[1]
    q_scr[...] = _dot(wq_ref[...], ht_ref[...]).astype(_BF)
    for hm in range(2 * PK_HEADS):
        s_scr[hm] = _dot(keys_ref[hm], q_scr[hm * PK_HALF:(hm + 1) * PK_HALF, :])

    def head(h, carry):
        for c in range(tt // LANES):
            cols = slice(c * LANES, (c + 1) * LANES)
            s1 = s_scr[2 * h, :, cols]
            s2 = s_scr[2 * h + 1, :, cols]
            a = _top16_replicated([s1[r:r + SUBLANES] for r in range(0, N_KEYS, SUBLANES)])
            b, rank2 = _ranked_top_values(s2, PK_TOPK)
            bmat = _stack_rows(b)
            b_lo, b_hi = bmat[0:SUBLANES], bmat[SUBLANES:]
            cand = [a[0] + b_lo, a[0] + b_hi]
            cand += [a[p] + b_lo for p in range(1, 8)]
            cand += [_stack_rows([a[p][0:1] for p in range(8, PK_TOPK)]) + b[0]]
            top = _top16_replicated(cand)
            tau = top[PK_TOPK - 1]
            z = jnp.ones_like(tau)
            for k in range(1, PK_TOPK):
                z = z + jnp.exp(top[k] - top[0])
            z_row = z[0:1]
            count1 = jnp.zeros(s1.shape, _F32)
            for p in range(PK_TOPK):
                hit = jnp.where(a[p] + b_lo >= tau, 1.0, 0.0) + jnp.where(a[p] + b_hi >= tau, 1.0, 0.0)
                reach = jnp.sum(hit, axis=0, keepdims=True)
                count1 = jnp.where(s1 == jnp.tile(a[p], (N_KEYS // SUBLANES, 1)), reach, count1)
            n1_ref[h, :, cols] = count1
            e1_ref[h, :, cols] = jnp.exp(s1 - a[0][0:1]) * (0.5 / z_row)
            r2_ref[h, :, cols] = rank2.astype(_BF)
            e2_ref[h, :, cols] = jnp.exp(s2 - b[0]).astype(_BF)
        return carry

    lax.fori_loop(0, PK_HEADS, head, 0)


def _peer_route(ht, wq_t_bf, keys_bf):
    slabs, d, slab = ht.shape
    t_all = slabs * slab
    tt = min(ROUTE_TOKENS, slab)
    per_slab = slab // tt
    nq = wq_t_bf.shape[0]
    out = lambda dt: jax.ShapeDtypeStruct((PK_HEADS, N_KEYS, t_all), dt)
    tile = pl.BlockSpec((PK_HEADS, N_KEYS, tt), lambda t: (0, 0, t))
    return pl.pallas_call(
        _route_kernel,
        grid=(t_all // tt,),
        in_specs=[pl.BlockSpec((None, d, tt), lambda t: (t // per_slab, 0, t % per_slab)),
                  _resident(wq_t_bf.shape, lambda t: (0, 0)),
                  _resident(keys_bf.shape, lambda t: (0, 0, 0))],
        out_specs=[tile] * 4,
        out_shape=[out(_F32), out(_F32), out(_BF), out(_BF)],
        scratch_shapes=[pltpu.VMEM((nq, tt), _BF), pltpu.VMEM((2 * PK_HEADS, N_KEYS, tt), _F32)],
        compiler_params=_params("arbitrary"),
        name="peer_route",
    )(ht, wq_t_bf, keys_bf)


def _twice_gelu_tanh(x):
    return x * (1.0 + jnp.tanh(x * (0.7978845608028654 + (0.7978845608028654 * 0.044715) * (x * x))))


def _row_to_tile(row, rows):
    packed = jnp.broadcast_to(row, (16, LANES)).astype(_BF)
    return jnp.tile(packed, (rows // 16, 1))


def _peer_kernel(ht_ref, u_ref, vt_ref, n1_ref, e1_ref, r2_ref, e2_ref, x_ref, gt_ref, gf_ref,
                 o_ref, at_scr, wt_scr, acc, r2_scr, e2_scr, *, final_norm):
    e = pl.program_id(1)
    tt = ht_ref.shape[1]
    groups = u_ref.shape[0] // N_KEYS
    span = at_scr.shape[2]
    n_chunks = tt // span

    @pl.when(e == 0)
    def _():
        acc[...] = jnp.zeros(acc.shape, _F32)
        r2_scr[...] = r2_ref[...]
        e2_scr[...] = e2_ref[...]

    for c in range(n_chunks):
        at_scr[c] = _dot(u_ref[...], ht_ref[:, c * span:(c + 1) * span])
    zero = jnp.zeros((N_KEYS, LANES), _BF)

    for c in range(n_chunks):
        for sub in range(span // LANES):
            cols = slice(c * span + sub * LANES, c * span + (sub + 1) * LANES)
            sub_cols = slice(sub * LANES, (sub + 1) * LANES)
            for il in range(groups):
                rows = slice(il * N_KEYS, (il + 1) * N_KEYS)
                g = None
                for h in range(PK_HEADS):
                    count = _row_to_tile(n1_ref[h, il:il + 1, cols], N_KEYS)
                    sel = jnp.where(r2_scr[h, :, cols] < count, e2_scr[h, :, cols], zero)
                    term = sel * _row_to_tile(e1_ref[h, il:il + 1, cols], N_KEYS)
                    g = term if g is None else g + term
                act = _twice_gelu_tanh(at_scr[c, rows, sub_cols]).astype(_BF)
                wt_scr[c, rows, sub_cols] = g * act
        acc[c] += lax.dot_general(vt_ref[...], wt_scr[c], (((0,), (0,)), ((), ())),
                                  preferred_element_type=_F32)

    @pl.when(e == pl.num_programs(1) - 1)
    def _():
        f = jnp.concatenate([acc[c].T for c in range(n_chunks)], axis=0)
        y = x_ref[...] + gt_ref[...] * f
        if final_norm:
            y = _rms(y) * gf_ref[...]
        o_ref[...] = y


def _peer_experts(ht, u_bf, vt_bf, route, x1, gate, g_final, *, layer, per_row, seq, final_norm):
    tiles, d, tt = ht.shape
    eb = PEER_EXPERTS
    steps = vt_bf.shape[1] // eb
    t_all = tiles * tt
    span = min(PEER_CHUNK, tt)
    groups = eb // N_KEYS
    assert groups == SUBLANES
    n1, e1, r2, e2 = route
    by_step = lambda a: a.reshape(PK_HEADS, N_KEYS // groups, groups, t_all)
    gate_spec = (pl.BlockSpec((tt, d), lambda t, e: (t, 0)) if per_row else
                 pl.BlockSpec((None, 1, d), lambda t, e: (t // max(seq // tt, 1), 0, 0)))
    step_rows = pl.BlockSpec((PK_HEADS, None, groups, tt), lambda t, e: (0, e, 0, t))
    tile = pl.BlockSpec((PK_HEADS, N_KEYS, tt), lambda t, e: (0, 0, t))
    return pl.pallas_call(
        functools.partial(_peer_kernel, final_norm=final_norm),
        grid=(tiles, steps),
        in_specs=[pl.BlockSpec((None, d, tt), lambda t, e: (t, 0, 0)),
                  pl.BlockSpec((None, eb, d), lambda t, e: (layer, e, 0)),
                  pl.BlockSpec((None, eb, d), lambda t, e: (layer, e, 0)),
                  step_rows, step_rows, tile, tile,
                  pl.BlockSpec((tt, d), lambda t, e: (t, 0)),
                  gate_spec,
                  _resident((1, d), lambda t, e: (0, 0))],
        out_specs=pl.BlockSpec((tt, d), lambda t, e: (t, 0)),
        out_shape=jax.ShapeDtypeStruct((t_all, d), _F32),
        scratch_shapes=[pltpu.VMEM((tt // span, eb, span), _F32), pltpu.VMEM((tt // span, eb, span), _BF),
                        pltpu.VMEM((tt // span, d, span), _F32),
                        pltpu.VMEM((PK_HEADS, N_KEYS, tt), _BF), pltpu.VMEM((PK_HEADS, N_KEYS, tt), _BF)],
        compiler_params=_params("arbitrary", "arbitrary"),
        name="peer_experts",
    )(ht, u_bf, vt_bf, by_step(n1), by_step(e1), r2, e2, x1, gate, g_final)


def _rope_tables(pos):
    half = ROT_DIM // 2
    inv = ROPE_THETA ** (-jnp.arange(0, ROT_DIM, 2, dtype=_F32) / ROT_DIM)
    ang = pos.astype(_F32)[:, None] * inv[None, :]
    cos, sin = jnp.cos(ang), jnp.sin(ang)
    n = pos.shape[0]
    pad = jnp.zeros((n, A_DIM - ROT_DIM), _F32)
    zeros = jnp.zeros((n, half), _F32)
    c = jnp.concatenate([cos, cos, pad + 1.0], axis=1)
    sa = jnp.concatenate([-sin, zeros, pad], axis=1)
    sb = jnp.concatenate([zeros, sin, pad], axis=1)
    return tuple(jnp.tile(t, (1, LANES // A_DIM)) for t in (c, sa, sb))


def _stream_layer(x, mod, lw, tables, rope_tabs, mixer, *, per_row, seq, final_norm, g_final, stack=None):
    shift1, scale1, gate1, shift2, scale2, gate2 = mod
    proj = _in_projection(x, shift1, scale1, lw["g_attn"], lw["w_in"], rope_tabs, per_row=per_row, seq=seq,
                          stack=stack)
    qa, kaf, vaf, kab, vab, qb, kbf, vbf, kbb, vbb = proj
    oa, ob, extra = mixer(qa, kaf, vaf, kab, vab, qb, kbf, vbf, kbb, vbb)
    x1, ht = _out_projection(x, oa, ob, lw["w_out"], gate1, shift2, scale2, lw["g_ffn"],
                             per_row=per_row, seq=seq)
    route = _peer_route(ht, lw["pk_wq_t"], lw["pk_keys"])
    x2 = _peer_experts(ht, *tables, route, x1, gate2, g_final, layer=lw["layer"],
                       per_row=per_row, seq=seq, final_norm=final_norm)
    return x2, (kaf, vaf, kbf, vbf) + extra


def kernel(x_prompt, x_sample, c_prompt, c_sample, cache_a_k, cache_a_v, cache_b_k, cache_b_v, w_ada, b_ada, g_attn, g_ffn, w_in, lam_q1, lam_k1, lam_q2, lam_k2, a_gain, rel_bias, b_gain, w_out, pk_wq, pk_keys, pk_u, pk_v, g_final):
    batch, seq, d = x_prompt.shape
    nb, t_new, _ = x_sample.shape
    depth = w_ada.shape[0]
    past = cache_a_k.shape[2]
    wb = cache_b_k.shape[2]
    a_width = A_HEADS * 2 * A_DIM
    b_width = B_HEADS * B_DIM

    c_all = jnp.concatenate([c_prompt, c_sample], axis=0)
    pad_rows = (-c_all.shape[0]) % 16
    c_all = jnp.pad(c_all, ((0, pad_rows), (0, 0)))
    mod = _modulation(c_all, w_ada, b_ada)

    tabs_p = _rope_tables(jnp.arange(seq, dtype=jnp.int32))
    pos_s = past + jnp.arange(t_new, dtype=jnp.int32)
    tabs_s = tuple(jnp.tile(t, (nb, 1)) for t in _rope_tables(pos_s))

    cak = cache_a_k.reshape(depth, nb, past * A_HEADS, 2 * A_DIM)
    cav = cache_a_v.reshape(depth, nb, past * A_HEADS, 2 * A_DIM)
    cbk = cache_b_k.reshape(depth, nb, wb, b_width)
    cbv = cache_b_v.reshape(depth, nb, wb, b_width)

    u_bf = pk_u.astype(_BF)
    vt_bf = pk_v.astype(_BF)

    xp = x_prompt.reshape(batch * seq, d)
    xs = x_sample.reshape(nb * t_new, d)
    g_fin = g_final.reshape(1, d)
    st_p, st_s = [], []
    for l in range(depth):
        lam_init = 0.8 - 0.6 * math.exp(-0.3 * l)
        lamv = jnp.stack([lam_q1[l], lam_k1[l], lam_q2[l], lam_k2[l]])
        lw = {
            "g_attn": g_attn[l].reshape(1, d),
            "g_ffn": g_ffn[l].reshape(1, d),
            "w_in": w_in[l].astype(_BF),
            "w_out": w_out[l].astype(_BF),
            "pk_wq_t": pk_wq[l].T.astype(_BF),
            "pk_keys": pk_keys[l].reshape(2 * PK_HEADS, N_KEYS, PK_HALF).astype(_BF),
            "layer": l,
        }
        gain_a = a_gain[l].reshape(1, 2 * A_DIM)
        gain_b = b_gain[l].reshape(1, b_width)
        mod_p = tuple(m.reshape(batch, 1, d) for m in jnp.split(mod[l, :batch], 6, axis=-1))
        mod_s = tuple(jnp.repeat(m, t_new, axis=0) for m in jnp.split(mod[l, batch:batch + nb], 6, axis=-1))
        last = l == depth - 1

        def mix_p(qa, kaf, vaf, kab, vab, qb, kbf, vbf, kbb, vbb):
            oa = _attn_a_prompt(qa, kab, vab, lamv, gain_a, batch=batch, seq=seq, lam_init=lam_init)
            ob = _attn_b_prompt(qb, kbb, vbb, rel_bias[l], gain_b, batch=batch, seq=seq)
            return oa, ob, ()

        def mix_s(qa, kaf, vaf, kab, vab, qb, kbf, vbf, kbb, vbb):
            oa = _attn_a_sample(qa, cak, cav, kaf, vaf, lamv, gain_a, layer=l, lam_init=lam_init, t_new=t_new)
            ob, nbk, nbv = _attn_b_sample(qb, cbk, cbv, kbf, vbf, rel_bias[l], gain_b, layer=l, t_new=t_new)
            return oa, ob, (nbk, nbv)

        previous = st_p[-1][0:2] if st_p else None
        xp, sp = _stream_layer(xp, mod_p, lw, (u_bf, vt_bf), tabs_p, mix_p, per_row=False, seq=seq,
                               final_norm=last, g_final=g_fin, stack=(l, depth, previous))
        xs, ss = _stream_layer(xs, mod_s, lw, (u_bf, vt_bf), tabs_s, mix_s, per_row=True, seq=t_new,
                               final_norm=last, g_final=g_fin)
        st_p.append(sp)
        st_s.append(ss)

    keep = min(B_REACH, seq)
    a_shape_p = (batch, seq, A_HEADS, 2 * A_DIM)
    b_tail = lambda s: s.reshape(batch, seq, b_width)[:, seq - keep:].reshape(batch, keep, B_HEADS, B_DIM)
    a_shape_s = (nb, t_new, A_HEADS, 2 * A_DIM)
    b_shape_s = (nb, wb, B_HEADS, B_DIM)
    return (
        xp.reshape(batch, seq, d),
        xs.reshape(nb, t_new, d),
        st_p[-1][0].reshape((depth,) + a_shape_p),
        st_p[-1][1].reshape((depth,) + a_shape_p),
        jnp.stack([b_tail(s[2]) for s in st_p]),
        jnp.stack([b_tail(s[3]) for s in st_p]),
        jnp.stack([s[0].reshape(a_shape_s) for s in st_s]),
        jnp.stack([s[1].reshape(a_shape_s) for s in st_s]),
        jnp.stack([s[4].reshape(b_shape_s) for s in st_s]),
        jnp.stack([s[5].reshape(b_shape_s) for s in st_s]),
    )
```
